```python
import jax, jax.numpy as jnp
from jax import lax
import numpy as np

D_MODEL = 2048
BATCH = 1
SEQ = 16384
DEPTH = 2

N_MIXERS = 2
NORM_EPS = 1e-6
DN_HEAD_DIM = 128
DN_QK_HEADS = D_MODEL // DN_HEAD_DIM
DN_V_HEADS = 2 * DN_QK_HEADS
DN_QK_DIM = DN_QK_HEADS * DN_HEAD_DIM
DN_V_DIM = DN_V_HEADS * DN_HEAD_DIM
DN_CONV_CH = 2 * DN_QK_DIM + DN_V_DIM
DN_PROJ = DN_CONV_CH + DN_V_DIM + 2 * DN_V_HEADS
DN_CONV = 4
DN_CHUNK = 64
SGU_WIDTH = 2 * D_MODEL
SGU_CHUNK = 128
SGU_GROUPS = 32
SGU_GROUP_DIM = SGU_WIDTH // SGU_GROUPS
N_EXPERTS = 32
TOP_K = 4
EXPERT_DIM = D_MODEL
SWIGLU_LIMIT = 7.0
SWIGLU_ALPHA = 1.702
MOE_BLOCK = 128
N_DN_LAYERS = (DEPTH + 1) // 2
N_SGU_LAYERS = DEPTH // 2

kernel_name = "hybrid_gdn_sgu_moe_adaln"


def rms_norm(x, w):
    xf = x.astype(jnp.float32)
    y = xf * lax.rsqrt(jnp.mean(xf * xf, axis=-1, keepdims=True) + NORM_EPS)
    return (y * w).astype(x.dtype)


def layer_norm(x, w, b):
    xf = x.astype(jnp.float32)
    mu = jnp.mean(xf, axis=-1, keepdims=True)
    var = jnp.mean(jnp.square(xf - mu), axis=-1, keepdims=True)
    return ((xf - mu) * lax.rsqrt(var + NORM_EPS) * w + b).astype(x.dtype)


def l2_normalize(x):
    xf = x.astype(jnp.float32)
    return xf * lax.rsqrt(jnp.sum(xf * xf, axis=-1, keepdims=True) + NORM_EPS)


def causal_depthwise_conv(x, w):
    k, ch = w.shape
    return lax.conv_general_dilated(x, w[:, None, :].astype(x.dtype), window_strides=(1,),
                                    padding=[(k - 1, 0)], dimension_numbers=('NWC', 'WIO', 'NWC'),
                                    feature_group_count=ch)


def chunk_gated_delta_rule(q, k, v, g, beta):
    bsz, seq, nh, dk = q.shape
    dv = v.shape[-1]
    n_chunks = seq // DN_CHUNK

    def chunks(t):
        return t.reshape(bsz, n_chunks, DN_CHUNK, nh, -1).transpose(1, 0, 3, 2, 4)

    qc, kc, vc = chunks(q), chunks(k), chunks(v)
    gc = g.reshape(bsz, n_chunks, DN_CHUNK, nh).transpose(1, 0, 3, 2)
    bc = beta.reshape(bsz, n_chunks, DN_CHUNK, nh).transpose(1, 0, 3, 2)
    g_cum = jnp.cumsum(gc, axis=-1)
    causal = jnp.tril(jnp.ones((DN_CHUNK, DN_CHUNK), bool))
    strict = jnp.tril(jnp.ones((DN_CHUNK, DN_CHUNK), bool), k=-1)
    diff = g_cum[..., :, None] - g_cum[..., None, :]
    decay = jnp.where(causal, jnp.exp(jnp.where(causal, diff, 0.0)), 0.0)
    k_beta = kc * bc[..., None]
    v_beta = vc * bc[..., None]
    lower = jnp.where(strict, jnp.einsum('nbhid,nbhjd->nbhij', k_beta, kc) * decay, 0.0)
    a_mat = jnp.eye(DN_CHUNK, dtype=jnp.float32) + lower
    rhs = jnp.concatenate([v_beta, k_beta * jnp.exp(g_cum)[..., None]], axis=-1)
    sol = lax.linalg.triangular_solve(a_mat, rhs, left_side=True, lower=True, unit_diagonal=True)
    u, w = sol[..., :dv], sol[..., dv:]
    intra = jnp.where(causal, jnp.einsum('nbhid,nbhjd->nbhij', qc, kc) * decay, 0.0)

    def step(state, xs):
        q_i, k_i, u_i, w_i, g_i, a_i = xs
        v_new = u_i - jnp.einsum('bhck,bhkv->bhcv', w_i, state)
        o = (jnp.einsum('bhck,bhkv->bhcv', q_i * jnp.exp(g_i)[..., None], state)
             + jnp.einsum('bhij,bhjv->bhiv', a_i, v_new))
        g_last = g_i[..., -1]
        state = (state * jnp.exp(g_last)[..., None, None]
                 + jnp.einsum('bhck,bhcv->bhkv', k_i * jnp.exp(g_last[..., None] - g_i)[..., None], v_new))
        return state, o

    state0 = jnp.zeros((bsz, nh, dk, dv), jnp.float32)
    _, o = lax.scan(step, state0, (qc, kc, u, w, g_cum, intra))
    return o.transpose(1, 0, 3, 2, 4).reshape(bsz, seq, nh, dv)


def gated_deltanet(h, w_in, conv_w, a_log, dt_bias, o_norm_w, w_out):
    bsz, seq, _ = h.shape
    proj = h @ w_in
    qkv, z, a, b = jnp.split(proj, [DN_CONV_CH, DN_CONV_CH + DN_V_DIM,
                                    DN_CONV_CH + DN_V_DIM + DN_V_HEADS], axis=-1)
    qkv = jax.nn.silu(causal_depthwise_conv(qkv, conv_w))
    q, k, v = jnp.split(qkv, [DN_QK_DIM, 2 * DN_QK_DIM], axis=-1)
    rep = DN_V_HEADS // DN_QK_HEADS
    q = jnp.repeat(l2_normalize(q.reshape(bsz, seq, DN_QK_HEADS, DN_HEAD_DIM)), rep, axis=2) * (DN_HEAD_DIM ** -0.5)
    k = jnp.repeat(l2_normalize(k.reshape(bsz, seq, DN_QK_HEADS, DN_HEAD_DIM)), rep, axis=2)
    v = v.reshape(bsz, seq, DN_V_HEADS, DN_HEAD_DIM).astype(jnp.float32)
    beta = jax.nn.sigmoid(b.astype(jnp.float32))
    g = -jnp.exp(a_log.astype(jnp.float32)) * jax.nn.softplus(a.astype(jnp.float32) + dt_bias.astype(jnp.float32))
    o = chunk_gated_delta_rule(q, k, v, g, beta)
    o = rms_norm(o, o_norm_w) * jax.nn.silu(z.reshape(bsz, seq, DN_V_HEADS, DN_HEAD_DIM).astype(jnp.float32))
    return o.reshape(bsz, seq, DN_V_DIM).astype(h.dtype) @ w_out


def chunked_gmlp(h, w_in, b_in, ln_w, ln_b, w_sp, b_sp, w_out, b_out):
    bsz, seq, _ = h.shape
    n_chunks = seq // SGU_CHUNK
    zz = jax.nn.gelu(h @ w_in + b_in, approximate=False)
    u, v = jnp.split(zz, 2, axis=-1)
    v = layer_norm(v, ln_w, ln_b)
    vc = v.reshape(bsz, n_chunks, SGU_CHUNK, SGU_GROUPS, SGU_GROUP_DIM)
    w_causal = jnp.tril(w_sp)
    sp = jnp.einsum('gts,bnsgd->bntgd', w_causal, vc) + b_sp.T[None, None, :, :, None]
    return (u * sp.reshape(bsz, seq, SGU_WIDTH)) @ w_out + b_out


def moe_ffn(h, layer, w_router, b_router, w_gate_up, b_gate_up, w_down, b_down):
    bsz, seq, d = h.shape
    n_tok = bsz * seq
    xt = h.reshape(n_tok, d)
    logits = (xt @ w_router[layer] + b_router[layer]).astype(jnp.float32)
    top_val, top_idx = lax.top_k(logits, TOP_K)
    gates = jax.nn.softmax(top_val, axis=-1)
    flat_e = top_idx.reshape(-1)
    order = jnp.argsort(flat_e)
    sorted_e = flat_e[order]
    sorted_tok = order // TOP_K
    sorted_gate = gates.reshape(-1)[order]
    counts = jnp.bincount(flat_e, length=N_EXPERTS)
    padded = (counts + MOE_BLOCK - 1) // MOE_BLOCK * MOE_BLOCK
    start = jnp.cumsum(counts) - counts
    pend = jnp.cumsum(padded)
    pstart = pend - padded
    dest = pstart[sorted_e] + (jnp.arange(n_tok * TOP_K) - start[sorted_e])
    n_blocks = (n_tok * TOP_K + N_EXPERTS * (MOE_BLOCK - 1) + MOE_BLOCK - 1) // MOE_BLOCK
    n_rows = n_blocks * MOE_BLOCK
    row_tok = jnp.full((n_rows,), n_tok, jnp.int32).at[dest].set(sorted_tok.astype(jnp.int32))
    row_gate = jnp.zeros((n_rows,), jnp.float32).at[dest].set(sorted_gate)
    block_expert = jnp.clip(jnp.searchsorted(pend, jnp.arange(n_blocks) * MOE_BLOCK, side='right'),
                            0, N_EXPERTS - 1)
    x_pad = jnp.concatenate([xt, jnp.zeros((1, d), xt.dtype)], axis=0)
    xs = x_pad[row_tok].reshape(n_blocks, MOE_BLOCK, d)

    def expert_block(args):
        xb, e = args
        gu = xb @ w_gate_up[layer, e] + b_gate_up[layer, e]
        x_glu = jnp.minimum(gu[:, :EXPERT_DIM], SWIGLU_LIMIT)
        x_lin = jnp.clip(gu[:, EXPERT_DIM:], -SWIGLU_LIMIT, SWIGLU_LIMIT)
        act = x_glu * jax.nn.sigmoid(SWIGLU_ALPHA * x_glu) * (x_lin + 1.0)
        return act @ w_down[layer, e] + b_down[layer, e]

    ys = lax.map(expert_block, (xs, block_expert)).reshape(n_rows, d)
    out = jax.ops.segment_sum(ys.astype(jnp.float32) * row_gate[:, None], row_tok, num_segments=n_tok + 1)[:n_tok]
    return out.reshape(bsz, seq, d).astype(h.dtype)


def setup_inputs(seed: int = 0) -> dict:
    key = jax.random.key(seed)
    ks = jax.random.split(key, 32)

    def nrm(k, shape, scale):
        return scale * jax.random.normal(k, shape, jnp.float32)

    dt = jnp.exp(jax.random.uniform(ks[8], (N_DN_LAYERS, DN_V_HEADS), jnp.float32,
                                    minval=float(np.log(1e-3)), maxval=float(np.log(1e-1))))
    return {
        'x': nrm(ks[0], (BATCH, SEQ, D_MODEL), 1.0),
        'c': nrm(ks[1], (BATCH, D_MODEL), 1.0),
        'ada_w': nrm(ks[2], (DEPTH, D_MODEL, 6 * D_MODEL), 0.5 * D_MODEL ** -0.5),
        'ada_b': nrm(ks[3], (DEPTH, 6 * D_MODEL), 0.02),
        'norm_w': 1.0 + nrm(ks[4], (DEPTH, 2, D_MODEL), 0.05),
        'dn_w_in': nrm(ks[5], (N_DN_LAYERS, D_MODEL, DN_PROJ), D_MODEL ** -0.5),
        'dn_conv_w': nrm(ks[6], (N_DN_LAYERS, DN_CONV, DN_CONV_CH), DN_CONV ** -0.5),
        'dn_a_log': jnp.log(jax.random.uniform(ks[7], (N_DN_LAYERS, DN_V_HEADS), jnp.float32, minval=1.0, maxval=16.0)),
        'dn_dt_bias': dt + jnp.log(-jnp.expm1(-dt)),
        'dn_o_norm_w': 1.0 + nrm(ks[9], (N_DN_LAYERS, DN_HEAD_DIM), 0.05),
        'dn_w_out': nrm(ks[10], (N_DN_LAYERS, DN_V_DIM, D_MODEL), DN_V_DIM ** -0.5),
        'sgu_w_in': nrm(ks[11], (N_SGU_LAYERS, D_MODEL, 2 * SGU_WIDTH), D_MODEL ** -0.5),
        'sgu_b_in': nrm(ks[12], (N_SGU_LAYERS, 2 * SGU_WIDTH), 0.02),
        'sgu_ln_w': 1.0 + nrm(ks[13], (N_SGU_LAYERS, SGU_WIDTH), 0.05),
        'sgu_ln_b': nrm(ks[14], (N_SGU_LAYERS, SGU_WIDTH), 0.02),
        'sgu_w_sp': nrm(ks[15], (N_SGU_LAYERS, SGU_GROUPS, SGU_CHUNK, SGU_CHUNK), 0.5 * SGU_CHUNK ** -0.5),
        'sgu_b_sp': 1.0 + nrm(ks[16], (N_SGU_LAYERS, SGU_GROUPS, SGU_CHUNK), 0.1),
        'sgu_w_out': nrm(ks[17], (N_SGU_LAYERS, SGU_WIDTH, D_MODEL), SGU_WIDTH ** -0.5),
        'sgu_b_out': nrm(ks[18], (N_SGU_LAYERS, D_MODEL), 0.02),
        'moe_w_router': nrm(ks[19], (DEPTH, D_MODEL, N_EXPERTS), D_MODEL ** -0.5),
        'moe_b_router': nrm(ks[20], (DEPTH, N_EXPERTS), 0.01),
        'moe_w_gate_up': nrm(ks[21], (DEPTH, N_EXPERTS, D_MODEL, 2 * EXPERT_DIM), D_MODEL ** -0.5),
        'moe_b_gate_up': nrm(ks[22], (DEPTH, N_EXPERTS, 2 * EXPERT_DIM), 0.02),
        'moe_w_down': nrm(ks[23], (DEPTH, N_EXPERTS, EXPERT_DIM, D_MODEL), EXPERT_DIM ** -0.5),
        'moe_b_down': nrm(ks[24], (DEPTH, N_EXPERTS, D_MODEL), 0.02),
        'final_norm_w': 1.0 + nrm(ks[25], (D_MODEL,), 0.05),
    }


def reference(x, c, ada_w, ada_b, norm_w, dn_w_in, dn_conv_w, dn_a_log, dn_dt_bias, dn_o_norm_w, dn_w_out,
              sgu_w_in, sgu_b_in, sgu_ln_w, sgu_ln_b, sgu_w_sp, sgu_b_sp, sgu_w_out, sgu_b_out,
              moe_w_router, moe_b_router, moe_w_gate_up, moe_b_gate_up, moe_w_down, moe_b_down, final_norm_w):
    c_act = jax.nn.silu(c)
    for i in range(DEPTH):
        mod = (c_act @ ada_w[i] + ada_b[i])[:, None, :]
        sh1, sc1, gt1, sh2, sc2, gt2 = jnp.split(mod, 6, axis=-1)
        h = rms_norm(x, norm_w[i, 0]) * (1.0 + sc1) + sh1
        j = i // N_MIXERS
        if i % N_MIXERS == 0:
            y = gated_deltanet(h, dn_w_in[j], dn_conv_w[j], dn_a_log[j], dn_dt_bias[j], dn_o_norm_w[j], dn_w_out[j])
        else:
            y = chunked_gmlp(h, sgu_w_in[j], sgu_b_in[j], sgu_ln_w[j], sgu_ln_b[j], sgu_w_sp[j], sgu_b_sp[j],
                             sgu_w_out[j], sgu_b_out[j])
        x = x + gt1 * y
        h = rms_norm(x, norm_w[i, 1]) * (1.0 + sc2) + sh2
        x = x + gt2 * moe_ffn(h, i, moe_w_router, moe_b_router, moe_w_gate_up, moe_b_gate_up, moe_w_down, moe_b_down)
    return rms_norm(x, final_norm_w)
```

```python
import functools

import jax
import jax.numpy as jnp
from jax import lax
from jax.experimental import pallas as pl
from jax.experimental.pallas import tpu as pltpu

F32 = jnp.float32
BF16 = jnp.bfloat16
I32 = jnp.int32
HIGHEST = lax.Precision.HIGHEST

D_MODEL = 2048
DEPTH = 2
NORM_EPS = 1e-6
DN_HEAD_DIM = 128
DN_QK_HEADS = 16
DN_V_HEADS = 32
DN_QK_DIM = 2048
DN_V_DIM = 4096
DN_CONV_CH = 8192
DN_CONV = 4
DN_CHUNK = 64
SGU_WIDTH = 4096
SGU_CHUNK = 128
SGU_GROUPS = 32
N_EXPERTS = 32
TOP_K = 4
EXPERT_DIM = 2048
SWIGLU_LIMIT = 7.0
SWIGLU_ALPHA = 1.702

LANES = 128
VMEM_LIMIT = 56 * 1024 * 1024
MOE_TILE = 256
GDN_TILE = 256


def _params(*sem):
    return pltpu.CompilerParams(dimension_semantics=sem, vmem_limit_bytes=VMEM_LIMIT)


def _iota(shape, dim):
    return lax.broadcasted_iota(I32, shape, dim)


def _sigmoid(x):
    return 1.0 / (1.0 + jnp.exp(-x))


def _silu(x):
    return x * _sigmoid(x)


def _ada_kernel(c_ref, w_ref, b_ref, o_ref):
    c = c_ref[...]
    o_ref[...] = jnp.dot(_silu(c), w_ref[...], precision=HIGHEST, preferred_element_type=F32) + b_ref[...]


def ada_mod(c, ada_w, ada_b):
    depth, d, n = ada_w.shape
    tn = 1024
    c8 = jnp.broadcast_to(c, (8, d))
    out = pl.pallas_call(
        _ada_kernel,
        grid=(depth, n // tn),
        in_specs=[
            pl.BlockSpec((8, d), lambda l, j: (0, 0)),
            pl.BlockSpec((None, d, tn), lambda l, j: (l, 0, j)),
            pl.BlockSpec((None, 1, tn), lambda l, j: (l, 0, j)),
        ],
        out_specs=pl.BlockSpec((None, 8, tn), lambda l, j: (l, 0, j)),
        out_shape=jax.ShapeDtypeStruct((depth, 8, n), F32),
        compiler_params=_params("parallel", "parallel"),
        name="ada_mod",
    )(c8, ada_w, ada_b.reshape(depth, 1, n))
    return out[:, 0, :]


def _norm_mod(x, w, sc, sh):
    y = x * lax.rsqrt(jnp.mean(x * x, axis=-1, keepdims=True) + NORM_EPS)
    return (y * w) * (1.0 + sc) + sh


def _norm_mod_kernel(x_ref, w_ref, sc_ref, sh_ref, o_ref):
    o_ref[...] = _norm_mod(x_ref[...], w_ref[...], sc_ref[...], sh_ref[...]).astype(o_ref.dtype)


def norm_mod(x, w, sc, sh, out_dtype):
    t, d = x.shape
    tm = 512
    vec = pl.BlockSpec((1, d), lambda i: (0, 0))
    return pl.pallas_call(
        _norm_mod_kernel,
        grid=(t // tm,),
        in_specs=[pl.BlockSpec((tm, d), lambda i: (i, 0)), vec, vec, vec],
        out_specs=pl.BlockSpec((tm, d), lambda i: (i, 0)),
        out_shape=jax.ShapeDtypeStruct((t, d), out_dtype),
        compiler_params=_params("parallel"),
        name="norm_mod",
    )(x, w, sc, sh)


def _gelu_exact(x):
    return 0.5 * x * (1.0 + lax.erf(x * (2.0 ** -0.5)))


def _mm_kernel(a_ref, w_ref, *refs, mode):
    acc = jnp.dot(a_ref[...], w_ref[...], preferred_element_type=F32)
    if mode == "plain":
        (o_ref,) = refs
        o_ref[...] = acc.astype(o_ref.dtype)
    elif mode == "bias_gelu":
        b_ref, o_ref = refs
        o_ref[...] = _gelu_exact(acc + b_ref[...]).astype(o_ref.dtype)
    else:
        b_ref, res_ref, gate_ref, o_ref = refs
        o_ref[...] = res_ref[...] + gate_ref[...] * (acc + b_ref[...])


def matmul(a, w, *, mode, out_dtype, tm, tn, bias=None, res=None, gate=None):
    m, k = a.shape
    n = w.shape[1]
    assert m % tm == 0 and n % tn == 0
    row = pl.BlockSpec((1, tn), lambda i, j: (0, j))
    in_specs = [pl.BlockSpec((tm, k), lambda i, j: (i, 0)), pl.BlockSpec((k, tn), lambda i, j: (0, j))]
    args = [a, w]
    if mode != "plain":
        in_specs.append(row)
        args.append(bias)
    if mode == "residual":
        in_specs += [pl.BlockSpec((tm, tn), lambda i, j: (i, j)), row]
        args += [res, gate]
    return pl.pallas_call(
        functools.partial(_mm_kernel, mode=mode),
        grid=(m // tm, n // tn),
        in_specs=in_specs,
        out_specs=pl.BlockSpec((tm, tn), lambda i, j: (i, j)),
        out_shape=jax.ShapeDtypeStruct((m, n), out_dtype),
        compiler_params=_params("parallel", "parallel"),
        name="matmul_" + mode,
    )(*args)


def _softplus(x):
    return jnp.maximum(x, 0.0) + jnp.log1p(jnp.exp(-jnp.abs(x)))


def _head_select(n_rows, n_cols, first_row, cols_per_row):
    j = _iota((n_rows, n_cols), 0)
    l = _iota((n_rows, n_cols), 1)
    return jnp.where(j == first_row + l // cols_per_row, 1.0, 0.0).astype(F32)


def _gdn_pre_kernel(q_ref, k_ref, v_ref, qh_ref, kh_ref, vh_ref, wq_ref, wk_ref, wv_ref, ab_ref, alog_ref, dtb_ref,
                    qo_ref, ko_ref, vb_ref, kbg_ref, qg_ref, kd_ref, gcum_ref, beta_ref,
                    win_q, win_k, win_v, gcum_s, gtot_s, beta_s):
    i = pl.program_id(0)
    h = pl.program_id(1)
    tm = q_ref.shape[0]
    halo = qh_ref.shape[0]

    def conv_silu(x_ref, halo_ref, w_ref, win_ref):
        hal = halo_ref[...].astype(F32)
        win_ref[0:halo, :] = jnp.where(i == 0, jnp.zeros_like(hal), hal)
        win_ref[halo:halo + tm, :] = x_ref[...].astype(F32)
        w = w_ref[...]
        acc = None
        for j in range(DN_CONV):
            start = halo - (DN_CONV - 1) + j
            term = win_ref[start:start + tm, :] * w[j:j + 1, :]
            acc = term if acc is None else acc + term
        return _silu(acc)

    def l2norm(x):
        return x * lax.rsqrt(jnp.sum(x * x, axis=-1, keepdims=True) + NORM_EPS)

    @pl.when(h == 0)
    def _():
        ab = ab_ref[...]
        g = -jnp.exp(alog_ref[...]) * _softplus(ab + dtb_ref[...])
        r = _iota((tm, tm), 0)
        c = _iota((tm, tm), 1)
        same = (r // DN_CHUNK) == (c // DN_CHUNK)
        tri = jnp.where(same & (c <= r), 1.0, 0.0).astype(F32)
        blk = jnp.where(same, 1.0, 0.0).astype(F32)
        gcum = jnp.dot(tri, g, precision=HIGHEST, preferred_element_type=F32)
        gcum_s[...] = gcum
        gtot_s[...] = jnp.dot(blk, g, precision=HIGHEST, preferred_element_type=F32)
        beta = _sigmoid(ab)
        beta_s[...] = beta
        gcum_ref[...] = gcum
        beta_ref[...] = beta

    q = l2norm(conv_silu(q_ref, qh_ref, wq_ref, win_q)) * (DN_HEAD_DIM ** -0.5)
    k = l2norm(conv_silu(k_ref, kh_ref, wk_ref, win_k))
    v = conv_silu(v_ref, vh_ref, wv_ref, win_v)
    qo_ref[...] = q.astype(BF16)
    ko_ref[...] = k.astype(BF16)

    two = 2 * DN_HEAD_DIM
    e_g = _head_select(LANES, two, 2 * h, DN_HEAD_DIM)
    e_b = _head_select(LANES, two, DN_V_HEADS + 2 * h, DN_HEAD_DIM)
    gc = jnp.dot(gcum_s[...], e_g, precision=HIGHEST, preferred_element_type=F32)
    gt = jnp.dot(gtot_s[...], e_g, precision=HIGHEST, preferred_element_type=F32)
    be = jnp.dot(beta_s[...], e_b, precision=HIGHEST, preferred_element_type=F32)
    q2 = jnp.concatenate([q, q], axis=-1)
    k2 = jnp.concatenate([k, k], axis=-1)
    eg = jnp.exp(gc)
    vb_ref[...] = (v * be).astype(BF16)
    kbg_ref[...] = (k2 * be * eg).astype(BF16)
    qg_ref[...] = (q2 * eg).astype(BF16)
    kd_ref[...] = (k2 * jnp.exp(gt - gc)).astype(BF16)


def gdn_pre(proj, ab, conv_w, a_log, dt_bias):
    t = proj.shape[0]
    tm = GDN_TILE
    halo = 16
    hb = tm // halo
    hd = DN_HEAD_DIM
    nq = DN_QK_HEADS

    def halo_map(off):
        return lambda i, h: (jnp.maximum(i * hb - 1, 0), off + h)

    pad = jnp.zeros((1, LANES - DN_V_HEADS), F32)
    alog = jnp.concatenate([a_log.reshape(1, -1), pad], axis=1)
    dtb = jnp.concatenate([dt_bias.reshape(1, -1), pad], axis=1)
    big = jax.ShapeDtypeStruct((t, DN_V_DIM), BF16)
    small = jax.ShapeDtypeStruct((t, DN_QK_DIM), BF16)
    gshape = jax.ShapeDtypeStruct((t, LANES), F32)
    big_spec = pl.BlockSpec((tm, 2 * hd), lambda i, h: (i, h))
    small_spec = pl.BlockSpec((tm, hd), lambda i, h: (i, h))
    g_spec = pl.BlockSpec((tm, LANES), lambda i, h: (i, 0))
    row = pl.BlockSpec((1, LANES), lambda i, h: (0, 0))
    return pl.pallas_call(
        _gdn_pre_kernel,
        grid=(t // tm, nq),
        in_specs=[
            pl.BlockSpec((tm, hd), lambda i, h: (i, h)),
            pl.BlockSpec((tm, hd), lambda i, h: (i, nq + h)),
            pl.BlockSpec((tm, 2 * hd), lambda i, h: (i, nq + h)),
            pl.BlockSpec((halo, hd), halo_map(0)),
            pl.BlockSpec((halo, hd), halo_map(nq)),
            pl.BlockSpec((halo, 2 * hd), lambda i, h: (jnp.maximum(i * hb - 1, 0), nq + h)),
            pl.BlockSpec((DN_CONV, hd), lambda i, h: (0, h)),
            pl.BlockSpec((DN_CONV, hd), lambda i, h: (0, nq + h)),
            pl.BlockSpec((DN_CONV, 2 * hd), lambda i, h: (0, nq + h)),
            g_spec, row, row,
        ],
        out_specs=[small_spec, small_spec, big_spec, big_spec, big_spec, big_spec, g_spec, g_spec],
        out_shape=[small, small, big, big, big, big, gshape, gshape],
        scratch_shapes=[
            pltpu.VMEM((tm + halo, hd), F32), pltpu.VMEM((tm + halo, hd), F32), pltpu.VMEM((tm + halo, 2 * hd), F32),
            pltpu.VMEM((tm, LANES), F32), pltpu.VMEM((tm, LANES), F32), pltpu.VMEM((tm, LANES), F32),
        ],
        compiler_params=_params("parallel", "arbitrary"),
        name="gdn_pre",
    )(proj, proj, proj, proj, proj, proj, conv_w, conv_w, conv_w, ab, alog, dtb)


def _unit_lower_inverse(low, n_block):
    n = low.shape[0]
    r = _iota((n, n), 0)
    c = _iota((n, n), 1)
    eye = jnp.where(r == c, 1.0, 0.0).astype(F32)

    def mm(a, b):
        return jnp.dot(a.astype(BF16), b.astype(BF16), preferred_element_type=F32)

    d = jnp.where((r // 8) == (c // 8), low, 0.0)
    x = eye - d
    p = mm(d, d)
    x = x + mm(x, p)
    p = mm(p, p)
    x = x + mm(x, p)
    s = 8
    while s < n_block:
        off = jnp.where(((r // (2 * s)) == (c // (2 * s))) & ((r // s) != (c // s)), low, 0.0)
        x = x - mm(x, mm(off, x))
        s *= 2
    return x


def _gdn_chunk_kernel(q_ref, k_ref, vb_ref, kbg_ref, gcum_ref, beta_ref, u_ref, w_ref, a_ref):
    h = pl.program_id(1)
    tm = q_ref.shape[0]
    cs = DN_CHUNK
    hd = DN_HEAD_DIM
    n2 = 2 * cs
    r = _iota((n2, n2), 0)
    c = _iota((n2, n2), 1)
    same_head = (r // cs) == (c // cs)
    causal = same_head & (c <= r)
    strict = same_head & (c < r)
    e_g = _head_select(LANES, n2, 2 * h, cs)
    e_b = _head_select(LANES, n2, DN_V_HEADS + 2 * h, cs)
    nt = (((1,), (1,)), ((), ()))
    for ci in range(tm // cs):
        rows = slice(ci * cs, (ci + 1) * cs)
        kc = k_ref[rows, :]
        qc = q_ref[rows, :]
        kk = lax.dot_general(kc, kc, nt, preferred_element_type=F32)
        qk = lax.dot_general(qc, kc, nt, preferred_element_type=F32)
        gb = jnp.dot(gcum_ref[rows, :], e_g, precision=HIGHEST, preferred_element_type=F32)
        bb = jnp.dot(beta_ref[rows, :], e_b, precision=HIGHEST, preferred_element_type=F32)
        g_col = jnp.concatenate([gb[:, :cs], gb[:, cs:]], axis=0)
        g_col = jnp.concatenate([g_col, g_col], axis=1)
        b_col = jnp.concatenate([bb[:, :cs], bb[:, cs:]], axis=0)
        b_col = jnp.concatenate([b_col, b_col], axis=1)
        diff = g_col - g_col.T
        decay = jnp.where(causal, jnp.exp(jnp.where(causal, diff, 0.0)), 0.0)
        kk2 = jnp.concatenate([jnp.concatenate([kk, kk], axis=1)] * 2, axis=0)
        qk2 = jnp.concatenate([jnp.concatenate([qk, qk], axis=1)] * 2, axis=0)
        low = jnp.where(strict, b_col * kk2 * decay, 0.0)
        tinv = _unit_lower_inverse(low, cs).astype(BF16)
        intra = jnp.where(causal, qk2 * decay, 0.0)
        a_ref[rows, :] = jnp.where(_iota((cs, n2), 1) < cs, intra[:cs], intra[cs:]).astype(BF16)
        v2 = jnp.concatenate([vb_ref[rows, :hd], vb_ref[rows, hd:]], axis=0)
        k2 = jnp.concatenate([kbg_ref[rows, :hd], kbg_ref[rows, hd:]], axis=0)
        u2 = jnp.dot(tinv, v2, preferred_element_type=F32).astype(BF16)
        w2 = jnp.dot(tinv, k2, preferred_element_type=F32).astype(BF16)
        u_ref[rows, :hd] = u2[:cs]
        u_ref[rows, hd:] = u2[cs:]
        w_ref[rows, :hd] = w2[:cs]
        w_ref[rows, hd:] = w2[cs:]


def gdn_chunk(q, k, vb, kbg, gcum, beta):
    t = q.shape[0]
    tm = GDN_TILE
    hd = DN_HEAD_DIM
    big = jax.ShapeDtypeStruct((t, DN_V_DIM), BF16)
    big_spec = pl.BlockSpec((tm, 2 * hd), lambda i, h: (i, h))
    small_spec = pl.BlockSpec((tm, hd), lambda i, h: (i, h))
    g_spec = pl.BlockSpec((tm, LANES), lambda i, h: (i, 0))
    return pl.pallas_call(
        _gdn_chunk_kernel,
        grid=(t // tm, DN_QK_HEADS),
        in_specs=[small_spec, small_spec, big_spec, big_spec, g_spec, g_spec],
        out_specs=[big_spec, big_spec, small_spec],
        out_shape=[big, big, jax.ShapeDtypeStruct((t, DN_V_HEADS * DN_CHUNK), BF16)],
        compiler_params=_params("parallel", "parallel"),
        name="gdn_chunk",
    )(q, k, vb, kbg, gcum, beta)


def _gdn_scan_kernel(glast_ref, qg_ref, kd_ref, u_ref, w_ref, a_ref, z_ref, onw_ref, og_ref, s_ref):
    ci = pl.program_id(0)
    cs = DN_CHUNK
    hd = DN_HEAD_DIM

    @pl.when(ci == 0)
    def _():
        s_ref[...] = jnp.zeros_like(s_ref)

    onw = onw_ref[...]
    tn = (((0,), (0,)), ((), ()))
    lane = _iota((cs, 2 * cs), 1)
    for pair in range(DN_V_HEADS // 2):
        heads = (2 * pair, 2 * pair + 1)
        a2 = a_ref[:, pair * 2 * cs:(pair + 1) * 2 * cs]
        qs, v_news = [], []
        for hv in heads:
            cols = slice(hv * hd, (hv + 1) * hd)
            state = s_ref[hv]
            wq = jnp.concatenate([w_ref[:, cols], qg_ref[:, cols]], axis=0)
            ws = jnp.dot(wq, state.astype(BF16), preferred_element_type=F32)
            v_new = (u_ref[:, cols].astype(F32) - ws[:cs]).astype(BF16)
            decay = jnp.exp(jnp.full((1, hd), glast_ref[ci, hv], F32))
            s_ref[hv] = state * decay + lax.dot_general(kd_ref[:, cols], v_new, tn, preferred_element_type=F32)
            qs.append(ws[cs:])
            v_news.append(v_new)
        v2 = jnp.concatenate(v_news, axis=0)
        for s, hv in enumerate(heads):
            cols = slice(hv * hd, (hv + 1) * hd)
            a_s = jnp.where((lane // cs) == s, a2, jnp.zeros_like(a2))
            o = qs[s] + jnp.dot(a_s, v2, preferred_element_type=F32)
            on = o * lax.rsqrt(jnp.mean(o * o, axis=-1, keepdims=True) + NORM_EPS) * onw
            og_ref[:, cols] = (on * _silu(z_ref[:, cols].astype(F32))).astype(BF16)


def gdn_scan(glast, qg, kd, u, w, intra, proj, o_norm_w):
    t = qg.shape[0]
    cs = DN_CHUNK
    big_spec = pl.BlockSpec((cs, DN_V_DIM), lambda c, g: (c, 0))
    return pl.pallas_call(
        _gdn_scan_kernel,
        grid_spec=pltpu.PrefetchScalarGridSpec(
            num_scalar_prefetch=1,
            grid=(t // cs,),
            in_specs=[big_spec, big_spec, big_spec, big_spec,
                      pl.BlockSpec((cs, DN_V_HEADS * cs), lambda c, g: (c, 0)),
                      pl.BlockSpec((cs, DN_V_DIM), lambda c, g: (c, DN_CONV_CH // DN_V_DIM)),
                      pl.BlockSpec((1, DN_HEAD_DIM), lambda c, g: (0, 0))],
            out_specs=big_spec,
            scratch_shapes=[pltpu.VMEM((DN_V_HEADS, DN_HEAD_DIM, DN_HEAD_DIM), F32)],
        ),
        out_shape=jax.ShapeDtypeStruct((t, DN_V_DIM), BF16),
        compiler_params=_params("arbitrary"),
        name="gdn_scan",
    )(glast, qg, kd, u, w, intra, proj, o_norm_w.reshape(1, -1))


def gated_deltanet_block(x, h, gate, w_in, conv_w, a_log, dt_bias, o_norm_w, w_out):
    t = x.shape[0]
    n_main = DN_CONV_CH + DN_V_DIM
    w_main = w_in[:, :n_main].astype(BF16)
    w_ab = jnp.pad(w_in[:, n_main:], ((0, 0), (0, LANES - 2 * DN_V_HEADS))).astype(BF16)
    proj = matmul(h, w_main, mode="plain", out_dtype=BF16, tm=1024, tn=512)
    ab = matmul(h, w_ab, mode="plain", out_dtype=F32, tm=1024, tn=LANES)
    q, k, vb, kbg, qg, kd, gcum, beta = gdn_pre(proj, ab, conv_w, a_log, dt_bias)
    u, w, intra = gdn_chunk(q, k, vb, kbg, gcum, beta)
    glast = gcum.reshape(t // DN_CHUNK, DN_CHUNK, LANES)[:, DN_CHUNK - 1, :DN_V_HEADS]
    og = gdn_scan(glast, qg, kd, u, w, intra, proj, o_norm_w)
    zero_bias = jnp.zeros((1, D_MODEL), F32)
    return matmul(og, w_out.astype(BF16), mode="residual", out_dtype=F32, tm=512, tn=512,
                  bias=zero_bias, res=x, gate=gate)


def _sgu_spatial_kernel(u_ref, v_ref, lnw_ref, lnb_ref, wsp_ref, bsp_ref, o_ref):
    cs = SGU_CHUNK
    gd = SGU_WIDTH // SGU_GROUPS
    v = v_ref[...].astype(F32)
    mu = jnp.mean(v, axis=-1, keepdims=True)
    var = jnp.mean(jnp.square(v - mu), axis=-1, keepdims=True)
    vn = ((v - mu) * lax.rsqrt(var + NORM_EPS) * lnw_ref[...] + lnb_ref[...]).astype(BF16)
    r = _iota((cs, cs), 0)
    c = _iota((cs, cs), 1)
    bsp = bsp_ref[...]
    for g in range(SGU_GROUPS):
        cols = slice(g * gd, (g + 1) * gd)
        wg = jnp.where(c <= r, wsp_ref[g], 0.0).astype(BF16)
        sp = jnp.dot(wg, vn[:, cols], preferred_element_type=F32) + bsp[:, g:g + 1]
        o_ref[:, cols] = (u_ref[:, cols].astype(F32) * sp).astype(BF16)


def sgu_spatial(zz, ln_w, ln_b, w_sp, b_sp):
    t = zz.shape[0]
    cs = SGU_CHUNK
    wd = SGU_WIDTH
    bsp_t = jnp.pad(b_sp.T, ((0, 0), (0, LANES - SGU_GROUPS)))
    row = pl.BlockSpec((1, wd), lambda i: (0, 0))
    return pl.pallas_call(
        _sgu_spatial_kernel,
        grid=(t // cs,),
        in_specs=[
            pl.BlockSpec((cs, wd), lambda i: (i, 0)),
            pl.BlockSpec((cs, wd), lambda i: (i, 1)),
            row, row,
            pl.BlockSpec((SGU_GROUPS, cs, cs), lambda i: (0, 0, 0)),
            pl.BlockSpec((cs, LANES), lambda i: (0, 0)),
        ],
        out_specs=pl.BlockSpec((cs, wd), lambda i: (i, 0)),
        out_shape=jax.ShapeDtypeStruct((t, wd), BF16),
        compiler_params=_params("parallel"),
        name="sgu_spatial",
    )(zz, zz, ln_w.reshape(1, -1), ln_b.reshape(1, -1), w_sp, bsp_t)


def chunked_gmlp_block(x, h, gate, w_in, b_in, ln_w, ln_b, w_sp, b_sp, w_out, b_out):
    zz = matmul(h, w_in.astype(BF16), mode="bias_gelu", out_dtype=BF16, tm=1024, tn=512, bias=b_in.reshape(1, -1))
    su = sgu_spatial(zz, ln_w, ln_b, w_sp, b_sp)
    return matmul(su, w_out.astype(BF16), mode="residual", out_dtype=F32, tm=512, tn=512,
                  bias=b_out.reshape(1, -1), res=x, gate=gate)


def _router_kernel(x_ref, w_ref, sc_ref, sh_ref, wr_ref, br_ref, h_ref, idx_ref, gate_ref, pos_ref, cnt_ref, carry):
    i = pl.program_id(0)
    tm = x_ref.shape[0]

    @pl.when(i == 0)
    def _():
        carry[...] = jnp.zeros_like(carry)

    h = _norm_mod(x_ref[...], w_ref[...], sc_ref[...], sh_ref[...])
    h_ref[...] = h
    lane = _iota((tm, LANES), 1).astype(F32)
    neg = jnp.float32(-jnp.inf)
    logits = jnp.dot(h, wr_ref[...], precision=HIGHEST, preferred_element_type=F32) + br_ref[...]
    logits = jnp.where(lane < N_EXPERTS, logits, neg)
    vals, idxs = [], []
    for _ in range(TOP_K):
        m = jnp.max(logits, axis=-1, keepdims=True)
        ix = jnp.min(jnp.where(logits == m, lane, float(LANES)), axis=-1, keepdims=True)
        vals.append(m)
        idxs.append(ix)
        logits = jnp.where(lane == ix, neg, logits)
    es = [jnp.exp(v - vals[0]) for v in vals]
    denom = es[0] + es[1] + es[2] + es[3]
    multi = jnp.zeros((tm, LANES), F32)
    for ix in idxs:
        multi = jnp.where(lane == ix, 1.0, multi)
    r = _iota((tm, tm), 0)
    c = _iota((tm, tm), 1)
    before = jnp.where(c < r, 1.0, 0.0).astype(BF16)
    rank = jnp.dot(before, multi.astype(BF16), preferred_element_type=F32) + carry[0:1, :]
    idx_t = jnp.zeros((tm, LANES), F32)
    gate_t = jnp.zeros((tm, LANES), F32)
    pos_t = jnp.zeros((tm, LANES), F32)
    for kk in range(TOP_K):
        pk = jnp.sum(jnp.where(lane == idxs[kk], rank, 0.0), axis=-1, keepdims=True)
        idx_t = jnp.where(lane == kk, idxs[kk], idx_t)
        gate_t = jnp.where(lane == kk, es[kk] / denom, gate_t)
        pos_t = jnp.where(lane == kk, pk, pos_t)
    idx_ref[...] = idx_t.astype(I32)
    gate_ref[...] = gate_t
    pos_ref[...] = pos_t.astype(I32)
    carry[...] = carry[...] + jnp.sum(multi, axis=0, keepdims=True)
    cnt_ref[...] = carry[...]


def moe_router(x, w, sc, sh, w_router, b_router):
    t, d = x.shape
    tm = 256
    wr = jnp.pad(w_router, ((0, 0), (0, LANES - N_EXPERTS)))
    br = jnp.pad(b_router.reshape(1, -1), ((0, 0), (0, LANES - N_EXPERTS)))
    vec = pl.BlockSpec((1, d), lambda i: (0, 0))
    tile = pl.BlockSpec((tm, LANES), lambda i: (i, 0))
    return pl.pallas_call(
        _router_kernel,
        grid=(t // tm,),
        in_specs=[pl.BlockSpec((tm, d), lambda i: (i, 0)), vec, vec, vec,
                  pl.BlockSpec((d, LANES), lambda i: (0, 0)), pl.BlockSpec((1, LANES), lambda i: (0, 0))],
        out_specs=[pl.BlockSpec((tm, d), lambda i: (i, 0)), tile, tile, tile, pl.BlockSpec((8, LANES), lambda i: (0, 0))],
        out_shape=[jax.ShapeDtypeStruct((t, d), F32), jax.ShapeDtypeStruct((t, LANES), I32),
                   jax.ShapeDtypeStruct((t, LANES), F32), jax.ShapeDtypeStruct((t, LANES), I32),
                   jax.ShapeDtypeStruct((8, LANES), F32)],
        scratch_shapes=[pltpu.VMEM((8, LANES), F32)],
        compiler_params=_params("arbitrary"),
        name="moe_router",
    )(x, w, sc, sh, wr, br)


def _dispatch_kernel(dest_ref, h_ref, xs_in_ref, xs_ref, sem):
    del xs_in_ref
    i = pl.program_id(0)
    tm = h_ref.shape[0]

    def row_copy(r, d):
        return pltpu.make_async_copy(h_ref.at[pl.ds(r, 1), :], xs_ref.at[pl.ds(d, 1), :], sem)

    def issue(r, carry):
        base = (i * tm + r) * TOP_K
        for kk in range(TOP_K):
            row_copy(r, dest_ref[base + kk]).start()
        return carry

    lax.fori_loop(0, tm, issue, 0)

    def drain(r, carry):
        for kk in range(TOP_K):
            row_copy(r, 0).wait()
        return carry

    lax.fori_loop(0, tm, drain, 0)


def moe_dispatch(dest, h, n_rows):
    t, d = h.shape
    tm = 256
    return pl.pallas_call(
        _dispatch_kernel,
        grid_spec=pltpu.PrefetchScalarGridSpec(
            num_scalar_prefetch=1,
            grid=(t // tm,),
            in_specs=[pl.BlockSpec((tm, d), lambda i, dst: (i, 0)), pl.BlockSpec(memory_space=pl.ANY)],
            out_specs=pl.BlockSpec(memory_space=pl.ANY),
            scratch_shapes=[pltpu.SemaphoreType.DMA(())],
        ),
        out_shape=jax.ShapeDtypeStruct((n_rows, d), F32),
        input_output_aliases={2: 0},
        compiler_params=_params("arbitrary"),
        name="moe_dispatch",
    )(dest.reshape(-1), h, jnp.zeros((n_rows, d), F32))


def _ffn_up_kernel(te_ref, nu_ref, x_ref, w_ref, b_ref, o_ref):
    i = pl.program_id(0)
    f = EXPERT_DIM
    fc = 512

    @pl.when(i < nu_ref[0])
    def _():
        xb = x_ref[...].astype(BF16)
        for c in range(f // fc):
            g = jnp.dot(xb, w_ref[:, c * fc:(c + 1) * fc], preferred_element_type=F32) + b_ref[:, c * fc:(c + 1) * fc]
            u = jnp.dot(xb, w_ref[:, f + c * fc:f + (c + 1) * fc], preferred_element_type=F32) + b_ref[:, f + c * fc:f + (c + 1) * fc]
            x_glu = jnp.minimum(g, SWIGLU_LIMIT)
            x_lin = jnp.clip(u, -SWIGLU_LIMIT, SWIGLU_LIMIT)
            act = x_glu * _sigmoid(SWIGLU_ALPHA * x_glu) * (x_lin + 1.0)
            o_ref[:, c * fc:(c + 1) * fc] = act.astype(BF16)

    @pl.when(i >= nu_ref[0])
    def _():
        o_ref[...] = jnp.zeros_like(o_ref)


def _ffn_down_kernel(te_ref, nu_ref, a_ref, w_ref, b_ref, o_ref):
    i = pl.program_id(0)

    @pl.when(i < nu_ref[0])
    def _():
        o_ref[...] = jnp.dot(a_ref[...], w_ref[...], preferred_element_type=F32) + b_ref[...]

    @pl.when(i >= nu_ref[0])
    def _():
        o_ref[...] = jnp.zeros_like(o_ref)


def moe_experts(tile_expert, n_used, xs, w_gate_up, b_gate_up, w_down, b_down):
    n_rows, d = xs.shape
    tm = MOE_TILE
    n_tiles = n_rows // tm
    f = EXPERT_DIM
    act = pl.pallas_call(
        _ffn_up_kernel,
        grid_spec=pltpu.PrefetchScalarGridSpec(
            num_scalar_prefetch=2,
            grid=(n_tiles,),
            in_specs=[pl.BlockSpec((tm, d), lambda i, te, nu: (i, 0)),
                      pl.BlockSpec((None, d, 2 * f), lambda i, te, nu: (te[i], 0, 0)),
                      pl.BlockSpec((None, 1, 2 * f), lambda i, te, nu: (te[i], 0, 0))],
            out_specs=pl.BlockSpec((tm, f), lambda i, te, nu: (i, 0)),
        ),
        out_shape=jax.ShapeDtypeStruct((n_rows, f), BF16),
        compiler_params=_params("arbitrary"),
        name="moe_ffn_up",
    )(tile_expert, n_used, xs, w_gate_up, b_gate_up.reshape(N_EXPERTS, 1, 2 * f))
    return pl.pallas_call(
        _ffn_down_kernel,
        grid_spec=pltpu.PrefetchScalarGridSpec(
            num_scalar_prefetch=2,
            grid=(n_tiles,),
            in_specs=[pl.BlockSpec((tm, f), lambda i, te, nu: (i, 0)),
                      pl.BlockSpec((None, f, d), lambda i, te, nu: (te[i], 0, 0)),
                      pl.BlockSpec((None, 1, d), lambda i, te, nu: (te[i], 0, 0))],
            out_specs=pl.BlockSpec((tm, d), lambda i, te, nu: (i, 0)),
        ),
        out_shape=jax.ShapeDtypeStruct((n_rows, d), F32),
        compiler_params=_params("arbitrary"),
        name="moe_ffn_down",
    )(tile_expert, n_used, act, w_down, b_down.reshape(N_EXPERTS, 1, d))


def _combine_kernel(dest_ref, x_ref, gate_ref, g2_ref, y_ref, o_ref, ybuf, sem):
    i = pl.program_id(0)
    n = pl.num_programs(0)
    tm = x_ref.shape[0]

    def row_copy(tile, slot, r, kk):
        d = dest_ref[(tile * tm + r) * TOP_K + kk]
        return pltpu.make_async_copy(y_ref.at[pl.ds(d, 1), :], ybuf.at[slot, kk, pl.ds(r, 1), :], sem.at[slot])

    def issue_tile(tile, slot):
        def body(r, carry):
            for kk in range(TOP_K):
                row_copy(tile, slot, r, kk).start()
            return carry
        lax.fori_loop(0, tm, body, 0)

    def drain_tile(tile, slot):
        def body(r, carry):
            for kk in range(TOP_K):
                row_copy(tile, slot, r, kk).wait()
            return carry
        lax.fori_loop(0, tm, body, 0)

    slot = i % 2

    @pl.when(i == 0)
    def _():
        issue_tile(0, 0)

    @pl.when(i + 1 < n)
    def _():
        issue_tile(i + 1, 1 - slot)

    drain_tile(i, slot)
    gates = gate_ref[...]
    acc = ybuf[slot, 0] * gates[:, 0:1]
    for kk in range(1, TOP_K):
        acc = acc + ybuf[slot, kk] * gates[:, kk:kk + 1]
    o_ref[...] = x_ref[...] + g2_ref[...] * acc


def moe_combine(dest, x, gates, gate2, y):
    t, d = x.shape
    tm = 128
    return pl.pallas_call(
        _combine_kernel,
        grid_spec=pltpu.PrefetchScalarGridSpec(
            num_scalar_prefetch=1,
            grid=(t // tm,),
            in_specs=[pl.BlockSpec((tm, d), lambda i, dst: (i, 0)),
                      pl.BlockSpec((tm, LANES), lambda i, dst: (i, 0)),
                      pl.BlockSpec((1, d), lambda i, dst: (0, 0)),
                      pl.BlockSpec(memory_space=pl.ANY)],
            out_specs=pl.BlockSpec((tm, d), lambda i, dst: (i, 0)),
            scratch_shapes=[pltpu.VMEM((2, TOP_K, tm, d), F32), pltpu.SemaphoreType.DMA((2,))],
        ),
        out_shape=jax.ShapeDtypeStruct((t, d), F32),
        compiler_params=_params("arbitrary"),
        name="moe_combine",
    )(dest.reshape(-1), x, gates, gate2, y)


def moe_block(x, norm_w, sc, sh, gate2, w_router, b_router, w_gate_up, b_gate_up, w_down, b_down):
    t, d = x.shape
    tm = MOE_TILE
    h, idx, gates, pos, counts = moe_router(x, norm_w, sc, sh, w_router, b_router)
    cnt = counts[0, :N_EXPERTS].astype(I32)
    padded = (cnt + tm - 1) // tm * tm
    pend = jnp.cumsum(padded)
    pstart = pend - padded
    n_tiles = (t * TOP_K + N_EXPERTS * (tm - 1) + tm - 1) // tm
    dest = pstart[idx[:, :TOP_K]] + pos[:, :TOP_K]
    tile_expert = jnp.clip(jnp.searchsorted(pend, jnp.arange(n_tiles, dtype=I32) * tm, side="right"),
                           0, N_EXPERTS - 1).astype(I32)
    n_used = (pend[-1:] // tm).astype(I32)
    xs = moe_dispatch(dest, h, n_tiles * tm)
    y = moe_experts(tile_expert, n_used, xs, w_gate_up.astype(BF16), b_gate_up, w_down.astype(BF16), b_down)
    return moe_combine(dest, x, gates, gate2, y)


def kernel(x, c, ada_w, ada_b, norm_w, dn_w_in, dn_conv_w, dn_a_log, dn_dt_bias, dn_o_norm_w, dn_w_out, sgu_w_in, sgu_b_in, sgu_ln_w, sgu_ln_b, sgu_w_sp, sgu_b_sp, sgu_w_out, sgu_b_out, moe_w_router, moe_b_router, moe_w_gate_up, moe_b_gate_up, moe_w_down, moe_b_down, final_norm_w):
    bsz, seq, d = x.shape
    assert bsz == 1 and d == D_MODEL
    xt = x.reshape(seq, d)
    mod = ada_mod(c, ada_w, ada_b)
    for i in range(DEPTH):
        sh1, sc1, gt1, sh2, sc2, gt2 = [mod[i:i + 1, s * d:(s + 1) * d] for s in range(6)]
        h = norm_mod(xt, norm_w[i, 0:1], sc1, sh1, BF16)
        j = i // 2
        if i % 2 == 0:
            xt = gated_deltanet_block(xt, h, gt1, dn_w_in[j], dn_conv_w[j], dn_a_log[j], dn_dt_bias[j],
                                      dn_o_norm_w[j], dn_w_out[j])
        else:
            xt = chunked_gmlp_block(xt, h, gt1, sgu_w_in[j], sgu_b_in[j], sgu_ln_w[j], sgu_ln_b[j], sgu_w_sp[j],
                                    sgu_b_sp[j], sgu_w_out[j], sgu_b_out[j])
        xt = moe_block(xt, norm_w[i, 1:2], sc2, sh2, gt2, moe_w_router[i], moe_b_router[i], moe_w_gate_up[i],
                       moe_b_gate_up[i], moe_w_down[i], moe_b_down[i])
    zero = jnp.zeros((1, d), F32)
    out = norm_mod(xt, final_norm_w.reshape(1, d), zero, zero, F32)
    return out.reshape(bsz, seq, d)
```

```python
import functools

import jax
import jax.numpy as jnp
from jax import lax
from jax.experimental import pallas as pl
from jax.experimental.pallas import tpu as pltpu

F32 = jnp.float32
BF16 = jnp.bfloat16
I32 = jnp.int32
HIGHEST = lax.Precision.HIGHEST

D_MODEL = 2048
DEPTH = 2
NORM_EPS = 1e-6
DN_HEAD_DIM = 128
DN_QK_HEADS = 16
DN_V_HEADS = 32
DN_QK_DIM = 2048
DN_V_DIM = 4096
DN_CONV_CH = 8192
DN_CONV = 4
DN_CHUNK = 64
SGU_WIDTH = 4096
SGU_CHUNK = 128
SGU_GROUPS = 32
N_EXPERTS = 32
TOP_K = 4
EXPERT_DIM = 2048
SWIGLU_LIMIT = 7.0
SWIGLU_ALPHA = 1.702

LANES = 128
VMEM_LIMIT = 56 * 1024 * 1024
MOE_TILE = 256
GDN_TILE = 256


def _params(*sem):
    return pltpu.CompilerParams(dimension_semantics=sem, vmem_limit_bytes=VMEM_LIMIT)


def _iota(shape, dim):
    return lax.broadcasted_iota(I32, shape, dim)


def _sigmoid(x):
    return 1.0 / (1.0 + jnp.exp(-x))


def _silu(x):
    return x * _sigmoid(x)


def _ada_kernel(c_ref, w_ref, b_ref, o_ref):
    c = c_ref[...]
    o_ref[...] = jnp.dot(_silu(c), w_ref[...], precision=HIGHEST, preferred_element_type=F32) + b_ref[...]


def ada_mod(c, ada_w, ada_b):
    depth, d, n = ada_w.shape
    tn = 1024
    c8 = jnp.broadcast_to(c, (8, d))
    out = pl.pallas_call(
        _ada_kernel,
        grid=(depth, n // tn),
        in_specs=[
            pl.BlockSpec((8, d), lambda l, j: (0, 0)),
            pl.BlockSpec((None, d, tn), lambda l, j: (l, 0, j)),
            pl.BlockSpec((None, 1, tn), lambda l, j: (l, 0, j)),
        ],
        out_specs=pl.BlockSpec((None, 8, tn), lambda l, j: (l, 0, j)),
        out_shape=jax.ShapeDtypeStruct((depth, 8, n), F32),
        compiler_params=_params("parallel", "parallel"),
        name="ada_mod",
    )(c8, ada_w, ada_b.reshape(depth, 1, n))
    return out[:, 0, :]


def _norm_mod(x, w, sc, sh):
    y = x * lax.rsqrt(jnp.mean(x * x, axis=-1, keepdims=True) + NORM_EPS)
    return (y * w) * (1.0 + sc) + sh


def _norm_mod_kernel(x_ref, w_ref, sc_ref, sh_ref, o_ref):
    o_ref[...] = _norm_mod(x_ref[...], w_ref[...], sc_ref[...], sh_ref[...]).astype(o_ref.dtype)


def norm_mod(x, w, sc, sh, out_dtype):
    t, d = x.shape
    tm = 512
    vec = pl.BlockSpec((1, d), lambda i: (0, 0))
    return pl.pallas_call(
        _norm_mod_kernel,
        grid=(t // tm,),
        in_specs=[pl.BlockSpec((tm, d), lambda i: (i, 0)), vec, vec, vec],
        out_specs=pl.BlockSpec((tm, d), lambda i: (i, 0)),
        out_shape=jax.ShapeDtypeStruct((t, d), out_dtype),
        compiler_params=_params("parallel"),
        name="norm_mod",
    )(x, w, sc, sh)


def _gelu_exact(x):
    return 0.5 * x * (1.0 + lax.erf(x * (2.0 ** -0.5)))


def _mm_kernel(a_ref, w_ref, *refs, mode):
    acc = jnp.dot(a_ref[...], w_ref[...], preferred_element_type=F32)
    if mode == "plain":
        (o_ref,) = refs
        o_ref[...] = acc.astype(o_ref.dtype)
    elif mode == "bias_gelu":
        b_ref, o_ref = refs
        o_ref[...] = _gelu_exact(acc + b_ref[...]).astype(o_ref.dtype)
    else:
        b_ref, res_ref, gate_ref, o_ref = refs
        o_ref[...] = res_ref[...] + gate_ref[...] * (acc + b_ref[...])


def matmul(a, w, *, mode, out_dtype, tm, tn, bias=None, res=None, gate=None):
    m, k = a.shape
    n = w.shape[1]
    assert m % tm == 0 and n % tn == 0
    row = pl.BlockSpec((1, tn), lambda i, j: (0, j))
    in_specs = [pl.BlockSpec((tm, k), lambda i, j: (i, 0)), pl.BlockSpec((k, tn), lambda i, j: (0, j))]
    args = [a, w]
    if mode != "plain":
        in_specs.append(row)
        args.append(bias)
    if mode == "residual":
        in_specs += [pl.BlockSpec((tm, tn), lambda i, j: (i, j)), row]
        args += [res, gate]
    return pl.pallas_call(
        functools.partial(_mm_kernel, mode=mode),
        grid=(m // tm, n // tn),
        in_specs=in_specs,
        out_specs=pl.BlockSpec((tm, tn), lambda i, j: (i, j)),
        out_shape=jax.ShapeDtypeStruct((m, n), out_dtype),
        compiler_params=_params("parallel", "parallel"),
        name="matmul_" + mode,
    )(*args)


def _softplus(x):
    return jnp.maximum(x, 0.0) + jnp.log1p(jnp.exp(-jnp.abs(x)))


def _head_select(n_rows, n_cols, first_row, cols_per_row):
    j = _iota((n_rows, n_cols), 0)
    l = _iota((n_rows, n_cols), 1)
    return jnp.where(j == first_row + l // cols_per_row, 1.0, 0.0).astype(F32)


def _unit_lower_inverses(lows, n_block):
    n = lows[0].shape[0]
    r = _iota((n, n), 0)
    c = _iota((n, n), 1)
    eye = jnp.where(r == c, 1.0, 0.0).astype(F32)

    def mm(a, b):
        return jnp.dot(a.astype(BF16), b.astype(BF16), preferred_element_type=F32)

    base = (r // 8) == (c // 8)
    ds = [jnp.where(base, low, 0.0) for low in lows]
    xs = [eye - d for d in ds]
    ps = [mm(d, d) for d in ds]
    xs = [x + mm(x, p) for x, p in zip(xs, ps)]
    ps = [mm(p, p) for p in ps]
    xs = [x + mm(x, p) for x, p in zip(xs, ps)]
    s = 8
    while s < n_block:
        pair = ((r // (2 * s)) == (c // (2 * s))) & ((r // s) != (c // s))
        ts = [mm(jnp.where(pair, low, 0.0), x) for low, x in zip(lows, xs)]
        xs = [x - mm(x, t) for x, t in zip(xs, ts)]
        s *= 2
    return xs


def _gdn_prep_kernel(q_ref, k_ref, v_ref, qh_ref, kh_ref, vh_ref, wq_ref, wk_ref, wv_ref, ab_ref, alog_ref, dtb_ref,
                     qg_ref, kd_ref, u_ref, w_ref, a_ref, gcum_ref,
                     win_q, win_k, win_v, gcum_s, gtot_s, beta_s):
    i = pl.program_id(0)
    h = pl.program_id(1)
    tm = q_ref.shape[0]
    halo = qh_ref.shape[0]
    cs = DN_CHUNK
    hd = DN_HEAD_DIM

    def conv_silu(x_ref, halo_ref, w_ref, win_ref):
        hal = halo_ref[...].astype(F32)
        win_ref[0:halo, :] = jnp.where(i == 0, jnp.zeros_like(hal), hal)
        win_ref[halo:halo + tm, :] = x_ref[...].astype(F32)
        w = w_ref[...]
        acc = None
        for j in range(DN_CONV):
            start = halo - (DN_CONV - 1) + j
            term = win_ref[start:start + tm, :] * w[j:j + 1, :]
            acc = term if acc is None else acc + term
        return _silu(acc)

    def l2norm(x):
        return x * lax.rsqrt(jnp.sum(x * x, axis=-1, keepdims=True) + NORM_EPS)

    @pl.when(h == 0)
    def _():
        ab = ab_ref[...]
        g = -jnp.exp(alog_ref[...]) * _softplus(ab + dtb_ref[...])
        r = _iota((tm, tm), 0)
        c = _iota((tm, tm), 1)
        same = (r // DN_CHUNK) == (c // DN_CHUNK)
        tri = jnp.where(same & (c <= r), 1.0, 0.0).astype(F32)
        blk = jnp.where(same, 1.0, 0.0).astype(F32)
        gcum = jnp.dot(tri, g, precision=HIGHEST, preferred_element_type=F32)
        gcum_s[...] = gcum
        gtot_s[...] = jnp.dot(blk, g, precision=HIGHEST, preferred_element_type=F32)
        beta_s[...] = _sigmoid(ab)
        gcum_ref[...] = gcum

    q = l2norm(conv_silu(q_ref, qh_ref, wq_ref, win_q)) * (DN_HEAD_DIM ** -0.5)
    k = l2norm(conv_silu(k_ref, kh_ref, wk_ref, win_k))
    v = conv_silu(v_ref, vh_ref, wv_ref, win_v)

    e_g = _head_select(LANES, 2 * hd, 2 * h, hd)
    e_b = _head_select(LANES, 2 * hd, DN_V_HEADS + 2 * h, hd)
    gc = jnp.dot(gcum_s[...], e_g, precision=HIGHEST, preferred_element_type=F32)
    gt = jnp.dot(gtot_s[...], e_g, precision=HIGHEST, preferred_element_type=F32)
    be = jnp.dot(beta_s[...], e_b, precision=HIGHEST, preferred_element_type=F32)
    q2 = jnp.concatenate([q, q], axis=-1)
    k2 = jnp.concatenate([k, k], axis=-1)
    eg = jnp.exp(gc)
    qg_ref[...] = (q2 * eg).astype(BF16)
    kd_ref[...] = (k2 * jnp.exp(gt - gc)).astype(BF16)
    vb = (v * be).astype(BF16)
    kbg = (k2 * be * eg).astype(BF16)
    qb = q.astype(BF16)
    kb = k.astype(BF16)

    n2 = 2 * cs
    r = _iota((n2, n2), 0)
    c = _iota((n2, n2), 1)
    same_head = (r // cs) == (c // cs)
    causal = same_head & (c <= r)
    strict = same_head & (c < r)
    top = _iota((cs, n2), 1) < cs
    nt = (((1,), (1,)), ((), ()))
    chunks = [slice(ci * cs, (ci + 1) * cs) for ci in range(tm // cs)]

    def stack_heads(x, rows):
        return jnp.concatenate([x[rows, :hd], x[rows, hd:]], axis=0)

    k_st = [jnp.concatenate([kb[rows], kb[rows]], axis=0) for rows in chunks]
    q_st = [jnp.concatenate([qb[rows], qb[rows]], axis=0) for rows in chunks]
    kk = [lax.dot_general(ks, ks, nt, preferred_element_type=F32) for ks in k_st]
    qk = [lax.dot_general(qs, ks, nt, preferred_element_type=F32) for qs, ks in zip(q_st, k_st)]
    g_col = [stack_heads(gc, rows) for rows in chunks]
    b_col = [stack_heads(be, rows) for rows in chunks]
    decay = [jnp.where(causal, jnp.exp(jnp.where(causal, g - g.T, 0.0)), 0.0) for g in g_col]
    lows = [jnp.where(strict, b * kk_i * d, 0.0) for b, kk_i, d in zip(b_col, kk, decay)]
    tinvs = [t.astype(BF16) for t in _unit_lower_inverses(lows, cs)]
    us = [jnp.dot(t, stack_heads(vb, rows), preferred_element_type=F32).astype(BF16) for t, rows in zip(tinvs, chunks)]
    ws = [jnp.dot(t, stack_heads(kbg, rows), preferred_element_type=F32).astype(BF16) for t, rows in zip(tinvs, chunks)]
    for rows, u2, w2, qk_i, d in zip(chunks, us, ws, qk, decay):
        intra = jnp.where(causal, qk_i * d, 0.0)
        a_ref[rows, :] = jnp.where(top, intra[:cs], intra[cs:]).astype(BF16)
        u_ref[rows, :hd] = u2[:cs]
        u_ref[rows, hd:] = u2[cs:]
        w_ref[rows, :hd] = w2[:cs]
        w_ref[rows, hd:] = w2[cs:]


def gdn_prep(proj, ab, conv_w, a_log, dt_bias):
    t = proj.shape[0]
    tm = GDN_TILE
    halo = 16
    hb = tm // halo
    hd = DN_HEAD_DIM
    nq = DN_QK_HEADS

    def halo_map(off):
        return lambda i, h: (jnp.maximum(i * hb - 1, 0), off + h)

    pad = jnp.zeros((1, LANES - DN_V_HEADS), F32)
    alog = jnp.concatenate([a_log.reshape(1, -1), pad], axis=1)
    dtb = jnp.concatenate([dt_bias.reshape(1, -1), pad], axis=1)
    big = jax.ShapeDtypeStruct((t, DN_V_DIM), BF16)
    small = jax.ShapeDtypeStruct((t, DN_QK_DIM), BF16)
    gshape = jax.ShapeDtypeStruct((t, LANES), F32)
    big_spec = pl.BlockSpec((tm, 2 * hd), lambda i, h: (i, h))
    small_spec = pl.BlockSpec((tm, hd), lambda i, h: (i, h))
    g_spec = pl.BlockSpec((tm, LANES), lambda i, h: (i, 0))
    row = pl.BlockSpec((1, LANES), lambda i, h: (0, 0))
    return pl.pallas_call(
        _gdn_prep_kernel,
        grid=(t // tm, nq),
        in_specs=[
            pl.BlockSpec((tm, hd), lambda i, h: (i, h)),
            pl.BlockSpec((tm, hd), lambda i, h: (i, nq + h)),
            pl.BlockSpec((tm, 2 * hd), lambda i, h: (i, nq + h)),
            pl.BlockSpec((halo, hd), halo_map(0)),
            pl.BlockSpec((halo, hd), halo_map(nq)),
            pl.BlockSpec((halo, 2 * hd), lambda i, h: (jnp.maximum(i * hb - 1, 0), nq + h)),
            pl.BlockSpec((DN_CONV, hd), lambda i, h: (0, h)),
            pl.BlockSpec((DN_CONV, hd), lambda i, h: (0, nq + h)),
            pl.BlockSpec((DN_CONV, 2 * hd), lambda i, h: (0, nq + h)),
            g_spec, row, row,
        ],
        out_specs=[big_spec, big_spec, big_spec, big_spec, small_spec, g_spec],
        out_shape=[big, big, big, big, small, gshape],
        scratch_shapes=[
            pltpu.VMEM((tm + halo, hd), F32), pltpu.VMEM((tm + halo, hd), F32), pltpu.VMEM((tm + halo, 2 * hd), F32),
            pltpu.VMEM((tm, LANES), F32), pltpu.VMEM((tm, LANES), F32), pltpu.VMEM((tm, LANES), F32),
        ],
        compiler_params=_params("parallel", "arbitrary"),
        name="gdn_prep",
    )(proj, proj, proj, proj, proj, proj, conv_w, conv_w, conv_w, ab, alog, dtb)


def _gdn_scan_kernel(glast_ref, qg_ref, kd_ref, u_ref, w_ref, a_ref, z_ref, onw_ref, og_ref, s_ref):
    ci = pl.program_id(0)
    cs = DN_CHUNK
    hd = DN_HEAD_DIM
    nh = DN_V_HEADS

    @pl.when(ci == 0)
    def _():
        s_ref[...] = jnp.zeros_like(s_ref)

    onw = onw_ref[...]
    tn = (((0,), (0,)), ((), ()))
    lane = _iota((cs, 2 * cs), 1)
    cols = [slice(hv * hd, (hv + 1) * hd) for hv in range(nh)]
    states = [s_ref[hv] for hv in range(nh)]
    ws = [jnp.dot(jnp.concatenate([w_ref[:, cols[hv]], qg_ref[:, cols[hv]]], axis=0), states[hv].astype(BF16),
                  preferred_element_type=F32) for hv in range(nh)]
    v_new = [(u_ref[:, cols[hv]].astype(F32) - ws[hv][:cs]).astype(BF16) for hv in range(nh)]
    for hv in range(nh):
        decay = jnp.exp(jnp.full((1, hd), glast_ref[ci, hv], F32))
        s_ref[hv] = states[hv] * decay + lax.dot_general(kd_ref[:, cols[hv]], v_new[hv], tn, preferred_element_type=F32)
    for pair in range(nh // 2):
        a2 = a_ref[:, pair * 2 * cs:(pair + 1) * 2 * cs]
        v2 = jnp.concatenate([v_new[2 * pair], v_new[2 * pair + 1]], axis=0)
        for s in range(2):
            hv = 2 * pair + s
            a_s = jnp.where((lane // cs) == s, a2, jnp.zeros_like(a2))
            o = ws[hv][cs:] + jnp.dot(a_s, v2, preferred_element_type=F32)
            on = o * lax.rsqrt(jnp.mean(o * o, axis=-1, keepdims=True) + NORM_EPS) * onw
            og_ref[:, cols[hv]] = (on * _silu(z_ref[:, cols[hv]].astype(F32))).astype(BF16)


def gdn_scan(glast, qg, kd, u, w, intra, proj, o_norm_w):
    t = qg.shape[0]
    cs = DN_CHUNK
    big_spec = pl.BlockSpec((cs, DN_V_DIM), lambda c, g: (c, 0))
    return pl.pallas_call(
        _gdn_scan_kernel,
        grid_spec=pltpu.PrefetchScalarGridSpec(
            num_scalar_prefetch=1,
            grid=(t // cs,),
            in_specs=[big_spec, big_spec, big_spec, big_spec,
                      pl.BlockSpec((cs, DN_V_HEADS * cs), lambda c, g: (c, 0)),
                      pl.BlockSpec((cs, DN_V_DIM), lambda c, g: (c, DN_CONV_CH // DN_V_DIM)),
                      pl.BlockSpec((1, DN_HEAD_DIM), lambda c, g: (0, 0))],
            out_specs=big_spec,
            scratch_shapes=[pltpu.VMEM((DN_V_HEADS, DN_HEAD_DIM, DN_HEAD_DIM), F32)],
        ),
        out_shape=jax.ShapeDtypeStruct((t, DN_V_DIM), BF16),
        compiler_params=_params("arbitrary"),
        name="gdn_scan",
    )(glast, qg, kd, u, w, intra, proj, o_norm_w.reshape(1, -1))


def gated_deltanet_block(x, h, gate, w_in, conv_w, a_log, dt_bias, o_norm_w, w_out):
    t = x.shape[0]
    n_main = DN_CONV_CH + DN_V_DIM
    w_main = w_in[:, :n_main].astype(BF16)
    w_ab = jnp.pad(w_in[:, n_main:], ((0, 0), (0, LANES - 2 * DN_V_HEADS))).astype(BF16)
    proj = matmul(h, w_main, mode="plain", out_dtype=BF16, tm=1024, tn=512)
    ab = matmul(h, w_ab, mode="plain", out_dtype=F32, tm=1024, tn=LANES)
    qg, kd, u, w, intra, gcum = gdn_prep(proj, ab, conv_w, a_log, dt_bias)
    glast = gcum.reshape(t // DN_CHUNK, DN_CHUNK, LANES)[:, DN_CHUNK - 1, :DN_V_HEADS]
    og = gdn_scan(glast, qg, kd, u, w, intra, proj, o_norm_w)
    zero_bias = jnp.zeros((1, D_MODEL), F32)
    return matmul(og, w_out.astype(BF16), mode="residual", out_dtype=F32, tm=512, tn=512,
                  bias=zero_bias, res=x, gate=gate)


def _sgu_spatial_kernel(u_ref, v_ref, lnw_ref, lnb_ref, wsp_ref, bsp_ref, o_ref):
    cs = SGU_CHUNK
    gd = SGU_WIDTH // SGU_GROUPS
    v = v_ref[...].astype(F32)
    mu = jnp.mean(v, axis=-1, keepdims=True)
    var = jnp.mean(jnp.square(v - mu), axis=-1, keepdims=True)
    vn = ((v - mu) * lax.rsqrt(var + NORM_EPS) * lnw_ref[...] + lnb_ref[...]).astype(BF16)
    r = _iota((cs, cs), 0)
    c = _iota((cs, cs), 1)
    bsp = bsp_ref[...]
    for g in range(SGU_GROUPS):
        cols = slice(g * gd, (g + 1) * gd)
        wg = jnp.where(c <= r, wsp_ref[g], 0.0).astype(BF16)
        sp = jnp.dot(wg, vn[:, cols], preferred_element_type=F32) + bsp[:, g:g + 1]
        o_ref[:, cols] = (u_ref[:, cols].astype(F32) * sp).astype(BF16)


def sgu_spatial(zz, ln_w, ln_b, w_sp, b_sp):
    t = zz.shape[0]
    cs = SGU_CHUNK
    wd = SGU_WIDTH
    bsp_t = jnp.pad(b_sp.T, ((0, 0), (0, LANES - SGU_GROUPS)))
    row = pl.BlockSpec((1, wd), lambda i: (0, 0))
    return pl.pallas_call(
        _sgu_spatial_kernel,
        grid=(t // cs,),
        in_specs=[
            pl.BlockSpec((cs, wd), lambda i: (i, 0)),
            pl.BlockSpec((cs, wd), lambda i: (i, 1)),
            row, row,
            pl.BlockSpec((SGU_GROUPS, cs, cs), lambda i: (0, 0, 0)),
            pl.BlockSpec((cs, LANES), lambda i: (0, 0)),
        ],
        out_specs=pl.BlockSpec((cs, wd), lambda i: (i, 0)),
        out_shape=jax.ShapeDtypeStruct((t, wd), BF16),
        compiler_params=_params("parallel"),
        name="sgu_spatial",
    )(zz, zz, ln_w.reshape(1, -1), ln_b.reshape(1, -1), w_sp, bsp_t)


def chunked_gmlp_block(x, h, gate, w_in, b_in, ln_w, ln_b, w_sp, b_sp, w_out, b_out):
    zz = matmul(h, w_in.astype(BF16), mode="bias_gelu", out_dtype=BF16, tm=1024, tn=512, bias=b_in.reshape(1, -1))
    su = sgu_spatial(zz, ln_w, ln_b, w_sp, b_sp)
    return matmul(su, w_out.astype(BF16), mode="residual", out_dtype=F32, tm=512, tn=512,
                  bias=b_out.reshape(1, -1), res=x, gate=gate)


def _router_kernel(x_ref, w_ref, sc_ref, sh_ref, wr_ref, br_ref, h_ref, idx_ref, gate_ref, pos_ref, cnt_ref, carry):
    i = pl.program_id(0)
    tm = x_ref.shape[0]

    @pl.when(i == 0)
    def _():
        carry[...] = jnp.zeros_like(carry)

    h = _norm_mod(x_ref[...], w_ref[...], sc_ref[...], sh_ref[...])
    h_ref[...] = h
    lane = _iota((tm, LANES), 1).astype(F32)
    neg = jnp.float32(-jnp.inf)
    logits = jnp.dot(h, wr_ref[...], precision=HIGHEST, preferred_element_type=F32) + br_ref[...]
    logits = jnp.where(lane < N_EXPERTS, logits, neg)
    vals, idxs = [], []
    for _ in range(TOP_K):
        m = jnp.max(logits, axis=-1, keepdims=True)
        ix = jnp.min(jnp.where(logits == m, lane, float(LANES)), axis=-1, keepdims=True)
        vals.append(m)
        idxs.append(ix)
        logits = jnp.where(lane == ix, neg, logits)
    es = [jnp.exp(v - vals[0]) for v in vals]
    denom = es[0] + es[1] + es[2] + es[3]
    multi = jnp.zeros((tm, LANES), F32)
    for ix in idxs:
        multi = jnp.where(lane == ix, 1.0, multi)
    r = _iota((tm, tm), 0)
    c = _iota((tm, tm), 1)
    before = jnp.where(c < r, 1.0, 0.0).astype(BF16)
    rank = jnp.dot(before, multi.astype(BF16), preferred_element_type=F32) + carry[0:1, :]
    idx_t = jnp.zeros((tm, LANES), F32)
    gate_t = jnp.zeros((tm, LANES), F32)
    pos_t = jnp.zeros((tm, LANES), F32)
    for kk in range(TOP_K):
        pk = jnp.sum(jnp.where(lane == idxs[kk], rank, 0.0), axis=-1, keepdims=True)
        idx_t = jnp.where(lane == kk, idxs[kk], idx_t)
        gate_t = jnp.where(lane == kk, es[kk] / denom, gate_t)
        pos_t = jnp.where(lane == kk, pk, pos_t)
    idx_ref[...] = idx_t.astype(I32)
    gate_ref[...] = gate_t
    pos_ref[...] = pos_t.astype(I32)
    carry[...] = carry[...] + jnp.sum(multi, axis=0, keepdims=True)
    cnt_ref[...] = carry[...]


def moe_router(x, w, sc, sh, w_router, b_router):
    t, d = x.shape
    tm = 256
    wr = jnp.pad(w_router, ((0, 0), (0, LANES - N_EXPERTS)))
    br = jnp.pad(b_router.reshape(1, -1), ((0, 0), (0, LANES - N_EXPERTS)))
    vec = pl.BlockSpec((1, d), lambda i: (0, 0))
    tile = pl.BlockSpec((tm, LANES), lambda i: (i, 0))
    return pl.pallas_call(
        _router_kernel,
        grid=(t // tm,),
        in_specs=[pl.BlockSpec((tm, d), lambda i: (i, 0)), vec, vec, vec,
                  pl.BlockSpec((d, LANES), lambda i: (0, 0)), pl.BlockSpec((1, LANES), lambda i: (0, 0))],
        out_specs=[pl.BlockSpec((tm, d), lambda i: (i, 0)), tile, tile, tile, pl.BlockSpec((8, LANES), lambda i: (0, 0))],
        out_shape=[jax.ShapeDtypeStruct((t, d), F32), jax.ShapeDtypeStruct((t, LANES), I32),
                   jax.ShapeDtypeStruct((t, LANES), F32), jax.ShapeDtypeStruct((t, LANES), I32),
                   jax.ShapeDtypeStruct((8, LANES), F32)],
        scratch_shapes=[pltpu.VMEM((8, LANES), F32)],
        compiler_params=_params("arbitrary"),
        name="moe_router",
    )(x, w, sc, sh, wr, br)


def _cast_kernel(x_ref, o_ref):
    o_ref[...] = x_ref[...].astype(o_ref.dtype)


def cast_layer_bf16(w, layer):
    _, e, k, n = w.shape
    tn = 1024
    return pl.pallas_call(
        _cast_kernel,
        grid=(e, n // tn),
        in_specs=[pl.BlockSpec((None, None, k, tn), lambda ei, j: (layer, ei, 0, j))],
        out_specs=pl.BlockSpec((None, k, tn), lambda ei, j: (ei, 0, j)),
        out_shape=jax.ShapeDtypeStruct((e, k, n), BF16),
        compiler_params=_params("parallel", "parallel"),
        name="cast_bf16",
    )(w)


def _dispatch_kernel(dest_ref, h_ref, xs_in_ref, xs_ref, sem):
    del xs_in_ref
    i = pl.program_id(0)
    tm = h_ref.shape[0]

    def row_copy(r, d):
        return pltpu.make_async_copy(h_ref.at[pl.ds(r, 1), :], xs_ref.at[pl.ds(d, 1), :], sem)

    def issue(r, carry):
        base = (i * tm + r) * TOP_K
        for kk in range(TOP_K):
            row_copy(r, dest_ref[base + kk]).start()
        return carry

    lax.fori_loop(0, tm, issue, 0)

    for kk in range(TOP_K):
        pltpu.make_async_copy(h_ref, xs_ref.at[pl.ds(0, tm), :], sem).wait()


def moe_dispatch(dest, h, n_rows):
    t, d = h.shape
    tm = 256
    return pl.pallas_call(
        _dispatch_kernel,
        grid_spec=pltpu.PrefetchScalarGridSpec(
            num_scalar_prefetch=1,
            grid=(t // tm,),
            in_specs=[pl.BlockSpec((tm, d), lambda i, dst: (i, 0)), pl.BlockSpec(memory_space=pl.ANY)],
            out_specs=pl.BlockSpec(memory_space=pl.ANY),
            scratch_shapes=[pltpu.SemaphoreType.DMA(())],
        ),
        out_shape=jax.ShapeDtypeStruct((n_rows, d), F32),
        input_output_aliases={2: 0},
        compiler_params=_params("arbitrary"),
        name="moe_dispatch",
    )(dest.reshape(-1), h, jnp.zeros((n_rows, d), F32))


def _ffn_up_kernel(te_ref, nu_ref, x_ref, w_ref, b_ref, o_ref):
    i = pl.program_id(0)
    f = EXPERT_DIM
    fc = 512

    @pl.when(i < nu_ref[0])
    def _():
        xb = x_ref[...].astype(BF16)
        for c in range(f // fc):
            g = jnp.dot(xb, w_ref[:, c * fc:(c + 1) * fc], preferred_element_type=F32) + b_ref[:, c * fc:(c + 1) * fc]
            u = jnp.dot(xb, w_ref[:, f + c * fc:f + (c + 1) * fc], preferred_element_type=F32) + b_ref[:, f + c * fc:f + (c + 1) * fc]
            x_glu = jnp.minimum(g, SWIGLU_LIMIT)
            x_lin = jnp.clip(u, -SWIGLU_LIMIT, SWIGLU_LIMIT)
            act = x_glu * _sigmoid(SWIGLU_ALPHA * x_glu) * (x_lin + 1.0)
            o_ref[:, c * fc:(c + 1) * fc] = act.astype(BF16)

    @pl.when(i >= nu_ref[0])
    def _():
        o_ref[...] = jnp.zeros_like(o_ref)


def _ffn_down_kernel(te_ref, nu_ref, a_ref, w_ref, b_ref, o_ref):
    i = pl.program_id(0)

    @pl.when(i < nu_ref[0])
    def _():
        o_ref[...] = jnp.dot(a_ref[...], w_ref[...], preferred_element_type=F32) + b_ref[...]

    @pl.when(i >= nu_ref[0])
    def _():
        o_ref[...] = jnp.zeros_like(o_ref)


def moe_experts(tile_expert, n_used, xs, w_gate_up, b_gate_up, w_down, b_down):
    n_rows, d = xs.shape
    tm = MOE_TILE
    n_tiles = n_rows // tm
    f = EXPERT_DIM
    act = pl.pallas_call(
        _ffn_up_kernel,
        grid_spec=pltpu.PrefetchScalarGridSpec(
            num_scalar_prefetch=2,
            grid=(n_tiles,),
            in_specs=[pl.BlockSpec((tm, d), lambda i, te, nu: (i, 0)),
                      pl.BlockSpec((None, d, 2 * f), lambda i, te, nu: (te[i], 0, 0)),
                      pl.BlockSpec((None, 1, 2 * f), lambda i, te, nu: (te[i], 0, 0))],
            out_specs=pl.BlockSpec((tm, f), lambda i, te, nu: (i, 0)),
        ),
        out_shape=jax.ShapeDtypeStruct((n_rows, f), BF16),
        compiler_params=_params("arbitrary"),
        name="moe_ffn_up",
    )(tile_expert, n_used, xs, w_gate_up, b_gate_up.reshape(N_EXPERTS, 1, 2 * f))
    return pl.pallas_call(
        _ffn_down_kernel,
        grid_spec=pltpu.PrefetchScalarGridSpec(
            num_scalar_prefetch=2,
            grid=(n_tiles,),
            in_specs=[pl.BlockSpec((tm, f), lambda i, te, nu: (i, 0)),
                      pl.BlockSpec((None, f, d), lambda i, te, nu: (te[i], 0, 0)),
                      pl.BlockSpec((None, 1, d), lambda i, te, nu: (te[i], 0, 0))],
            out_specs=pl.BlockSpec((tm, d), lambda i, te, nu: (i, 0)),
        ),
        out_shape=jax.ShapeDtypeStruct((n_rows, d), F32),
        compiler_params=_params("arbitrary"),
        name="moe_ffn_down",
    )(tile_expert, n_used, act, w_down, b_down.reshape(N_EXPERTS, 1, d))


def _combine_kernel(dest_ref, x_ref, gate_ref, g2_ref, y_ref, o_ref, ybuf, sem):
    i = pl.program_id(0)
    n = pl.num_programs(0)
    tm = x_ref.shape[0]

    def row_copy(tile, slot, r, kk):
        d = dest_ref[(tile * tm + r) * TOP_K + kk]
        return pltpu.make_async_copy(y_ref.at[pl.ds(d, 1), :], ybuf.at[slot, kk, pl.ds(r, 1), :], sem.at[slot])

    def issue_tile(tile, slot):
        def body(r, carry):
            for kk in range(TOP_K):
                row_copy(tile, slot, r, kk).start()
            return carry
        lax.fori_loop(0, tm, body, 0)

    def drain_tile(slot):
        for kk in range(TOP_K):
            pltpu.make_async_copy(y_ref.at[pl.ds(0, tm), :], ybuf.at[slot, kk], sem.at[slot]).wait()

    slot = i % 2

    @pl.when(i == 0)
    def _():
        issue_tile(0, 0)

    @pl.when(i + 1 < n)
    def _():
        issue_tile(i + 1, 1 - slot)

    drain_tile(slot)
    gates = gate_ref[...]
    acc = ybuf[slot, 0] * gates[:, 0:1]
    for kk in range(1, TOP_K):
        acc = acc + ybuf[slot, kk] * gates[:, kk:kk + 1]
    o_ref[...] = x_ref[...] + g2_ref[...] * acc


def moe_combine(dest, x, gates, gate2, y):
    t, d = x.shape
    tm = 128
    return pl.pallas_call(
        _combine_kernel,
        grid_spec=pltpu.PrefetchScalarGridSpec(
            num_scalar_prefetch=1,
            grid=(t // tm,),
            in_specs=[pl.BlockSpec((tm, d), lambda i, dst: (i, 0)),
                      pl.BlockSpec((tm, LANES), lambda i, dst: (i, 0)),
                      pl.BlockSpec((1, d), lambda i, dst: (0, 0)),
                      pl.BlockSpec(memory_space=pl.ANY)],
            out_specs=pl.BlockSpec((tm, d), lambda i, dst: (i, 0)),
            scratch_shapes=[pltpu.VMEM((2, TOP_K, tm, d), F32), pltpu.SemaphoreType.DMA((2,))],
        ),
        out_shape=jax.ShapeDtypeStruct((t, d), F32),
        compiler_params=_params("arbitrary"),
        name="moe_combine",
    )(dest.reshape(-1), x, gates, gate2, y)


def moe_block(x, norm_w, sc, sh, gate2, w_router, b_router, w_gate_up, b_gate_up, w_down, b_down):
    t, d = x.shape
    tm = MOE_TILE
    h, idx, gates, pos, counts = moe_router(x, norm_w, sc, sh, w_router, b_router)
    cnt = counts[0, :N_EXPERTS].astype(I32)
    padded = (cnt + tm - 1) // tm * tm
    pend = jnp.cumsum(padded)
    pstart = pend - padded
    n_tiles = (t * TOP_K + N_EXPERTS * (tm - 1) + tm - 1) // tm
    dest = pstart[idx[:, :TOP_K]] + pos[:, :TOP_K]
    tile_expert = jnp.clip(jnp.searchsorted(pend, jnp.arange(n_tiles, dtype=I32) * tm, side="right"),
                           0, N_EXPERTS - 1).astype(I32)
    n_used = (pend[-1:] // tm).astype(I32)
    xs = moe_dispatch(dest, h, n_tiles * tm)
    y = moe_experts(tile_expert, n_used, xs, w_gate_up, b_gate_up, w_down, b_down)
    return moe_combine(dest, x, gates, gate2, y)


def kernel(x, c, ada_w, ada_b, norm_w, dn_w_in, dn_conv_w, dn_a_log, dn_dt_bias, dn_o_norm_w, dn_w_out, sgu_w_in, sgu_b_in, sgu_ln_w, sgu_ln_b, sgu_w_sp, sgu_b_sp, sgu_w_out, sgu_b_out, moe_w_router, moe_b_router, moe_w_gate_up, moe_b_gate_up, moe_w_down, moe_b_down, final_norm_w):
    bsz, seq, d = x.shape
    assert bsz == 1 and d == D_MODEL
    xt = x.reshape(seq, d)
    mod = ada_mod(c, ada_w, ada_b)
    for i in range(DEPTH):
        sh1, sc1, gt1, sh2, sc2, gt2 = [mod[i:i + 1, s * d:(s + 1) * d] for s in range(6)]
        h = norm_mod(xt, norm_w[i, 0:1], sc1, sh1, BF16)
        j = i // 2
        if i % 2 == 0:
            xt = gated_deltanet_block(xt, h, gt1, dn_w_in[j], dn_conv_w[j], dn_a_log[j], dn_dt_bias[j],
                                      dn_o_norm_w[j], dn_w_out[j])
        else:
            xt = chunked_gmlp_block(xt, h, gt1, sgu_w_in[j], sgu_b_in[j], sgu_ln_w[j], sgu_ln_b[j], sgu_w_sp[j],
                                    sgu_b_sp[j], sgu_w_out[j], sgu_b_out[j])
        xt = moe_block(xt, norm_w[i, 1:2], sc2, sh2, gt2, moe_w_router[i], moe_b_router[i],
                       cast_layer_bf16(moe_w_gate_up, i), moe_b_gate_up[i], cast_layer_bf16(moe_w_down, i), moe_b_down[i])
    zero = jnp.zeros((1, d), F32)
    out = norm_mod(xt, final_norm_w.reshape(1, d), zero, zero, F32)
    return out.reshape(bsz, seq, d)
```

```python
import functools

import jax
import jax.numpy as jnp
from jax import lax
from jax.experimental import pallas as pl
from jax.experimental.pallas import tpu as pltpu

F32 = jnp.float32
BF16 = jnp.bfloat16
I32 = jnp.int32
HIGHEST = lax.Precision.HIGHEST

D_MODEL = 2048
DEPTH = 2
NORM_EPS = 1e-6
DN_HEAD_DIM = 128
DN_QK_HEADS = 16
DN_V_HEADS = 32
DN_QK_DIM = 2048
DN_V_DIM = 4096
DN_CONV_CH = 8192
DN_CONV = 4
DN_CHUNK = 64
SGU_WIDTH = 4096
SGU_CHUNK = 128
SGU_GROUPS = 32
N_EXPERTS = 32
TOP_K = 4
EXPERT_DIM = 2048
SWIGLU_LIMIT = 7.0
SWIGLU_ALPHA = 1.702

LANES = 128
VMEM_LIMIT = 56 * 1024 * 1024
MOE_TILE = 256
GDN_TILE = 512


def _params(*sem):
    return pltpu.CompilerParams(dimension_semantics=sem, vmem_limit_bytes=VMEM_LIMIT)


def _iota(shape, dim):
    return lax.broadcasted_iota(I32, shape, dim)


def _sigmoid(x):
    return 1.0 / (1.0 + jnp.exp(-x))


def _silu(x):
    return x * _sigmoid(x)


def _ada_kernel(c_ref, w_ref, b_ref, o_ref):
    c = c_ref[...]
    o_ref[...] = jnp.dot(_silu(c), w_ref[...], precision=HIGHEST, preferred_element_type=F32) + b_ref[...]


def ada_mod(c, ada_w, ada_b):
    depth, d, n = ada_w.shape
    tn = 1024
    c8 = jnp.broadcast_to(c, (8, d))
    out = pl.pallas_call(
        _ada_kernel,
        grid=(depth, n // tn),
        in_specs=[
            pl.BlockSpec((8, d), lambda l, j: (0, 0)),
            pl.BlockSpec((None, d, tn), lambda l, j: (l, 0, j)),
            pl.BlockSpec((None, 1, tn), lambda l, j: (l, 0, j)),
        ],
        out_specs=pl.BlockSpec((None, 8, tn), lambda l, j: (l, 0, j)),
        out_shape=jax.ShapeDtypeStruct((depth, 8, n), F32),
        compiler_params=_params("parallel", "parallel"),
        name="ada_mod",
    )(c8, ada_w, ada_b.reshape(depth, 1, n))
    return out[:, 0, :]


def _norm_mod(x, w, sc, sh):
    y = x * lax.rsqrt(jnp.mean(x * x, axis=-1, keepdims=True) + NORM_EPS)
    return (y * w) * (1.0 + sc) + sh


def _norm_mod_kernel(x_ref, w_ref, sc_ref, sh_ref, o_ref):
    o_ref[...] = _norm_mod(x_ref[...], w_ref[...], sc_ref[...], sh_ref[...]).astype(o_ref.dtype)


def norm_mod(x, w, sc, sh, out_dtype):
    t, d = x.shape
    tm = 512
    vec = pl.BlockSpec((1, d), lambda i: (0, 0))
    return pl.pallas_call(
        _norm_mod_kernel,
        grid=(t // tm,),
        in_specs=[pl.BlockSpec((tm, d), lambda i: (i, 0)), vec, vec, vec],
        out_specs=pl.BlockSpec((tm, d), lambda i: (i, 0)),
        out_shape=jax.ShapeDtypeStruct((t, d), out_dtype),
        compiler_params=_params("parallel"),
        name="norm_mod",
    )(x, w, sc, sh)


def _gelu_exact(x):
    return 0.5 * x * (1.0 + lax.erf(x * (2.0 ** -0.5)))


def _mm_kernel(a_ref, w_ref, *refs, mode):
    acc = jnp.dot(a_ref[...], w_ref[...], preferred_element_type=F32)
    if mode == "plain":
        (o_ref,) = refs
        o_ref[...] = acc.astype(o_ref.dtype)
    elif mode == "bias_gelu":
        b_ref, o_ref = refs
        o_ref[...] = _gelu_exact(acc + b_ref[...]).astype(o_ref.dtype)
    else:
        b_ref, res_ref, gate_ref, o_ref = refs
        o_ref[...] = res_ref[...] + gate_ref[...] * (acc + b_ref[...])


def matmul(a, w, *, mode, out_dtype, tm, tn, bias=None, res=None, gate=None):
    m, k = a.shape
    n = w.shape[1]
    assert m % tm == 0 and n % tn == 0
    row = pl.BlockSpec((1, tn), lambda i, j: (0, j))
    in_specs = [pl.BlockSpec((tm, k), lambda i, j: (i, 0)), pl.BlockSpec((k, tn), lambda i, j: (0, j))]
    args = [a, w]
    if mode != "plain":
        in_specs.append(row)
        args.append(bias)
    if mode == "residual":
        in_specs += [pl.BlockSpec((tm, tn), lambda i, j: (i, j)), row]
        args += [res, gate]
    return pl.pallas_call(
        functools.partial(_mm_kernel, mode=mode),
        grid=(m // tm, n // tn),
        in_specs=in_specs,
        out_specs=pl.BlockSpec((tm, tn), lambda i, j: (i, j)),
        out_shape=jax.ShapeDtypeStruct((m, n), out_dtype),
        compiler_params=_params("parallel", "parallel"),
        name="matmul_" + mode,
    )(*args)


def _softplus(x):
    return jnp.maximum(x, 0.0) + jnp.log1p(jnp.exp(-jnp.abs(x)))


def _unit_lower_inverses(lows, n_block):
    n = lows[0].shape[0]
    r = _iota((n, n), 0)
    c = _iota((n, n), 1)
    eye = jnp.where(r == c, 1.0, 0.0).astype(F32)

    def mm(a, b):
        return jnp.dot(a.astype(BF16), b.astype(BF16), preferred_element_type=F32)

    base = (r // 8) == (c // 8)
    ds = [jnp.where(base, low, 0.0) for low in lows]
    xs = [eye - d for d in ds]
    ps = [mm(d, d) for d in ds]
    xs = [x + mm(x, p) for x, p in zip(xs, ps)]
    ps = [mm(p, p) for p in ps]
    xs = [x + mm(x, p) for x, p in zip(xs, ps)]
    s = 8
    while s < n_block:
        pair = ((r // (2 * s)) == (c // (2 * s))) & ((r // s) != (c // s))
        ts = [mm(jnp.where(pair, low, 0.0), x) for low, x in zip(lows, xs)]
        xs = [x - mm(x, t) for x, t in zip(xs, ts)]
        s *= 2
    return xs


def _gdn_prep_kernel(q_ref, k_ref, v_ref, qh_ref, kh_ref, vh_ref, wq_ref, wk_ref, wv_ref, ab_ref, alog_ref, dtb_ref,
                     qg_ref, kd_ref, u_ref, w_ref, a_ref, gcum_ref,
                     win_q, win_k, win_v, gcum_s, gtot_s, beta_s):
    i = pl.program_id(0)
    h = pl.program_id(1)
    tm = q_ref.shape[0]
    halo = qh_ref.shape[0]
    cs = DN_CHUNK
    hd = DN_HEAD_DIM

    def conv_silu(x_ref, halo_ref, w_ref, win_ref):
        hal = halo_ref[...].astype(F32)
        win_ref[0:halo, :] = jnp.where(i == 0, jnp.zeros_like(hal), hal)
        win_ref[halo:halo + tm, :] = x_ref[...].astype(F32)
        w = w_ref[...]
        acc = None
        for j in range(DN_CONV):
            start = halo - (DN_CONV - 1) + j
            term = win_ref[start:start + tm, :] * w[j:j + 1, :]
            acc = term if acc is None else acc + term
        return _silu(acc)

    def l2norm(x):
        return x * lax.rsqrt(jnp.sum(x * x, axis=-1, keepdims=True) + NORM_EPS)

    @pl.when(h == 0)
    def _():
        ab = ab_ref[...]
        g = -jnp.exp(alog_ref[...]) * _softplus(ab + dtb_ref[...])
        r = _iota((tm, tm), 0)
        c = _iota((tm, tm), 1)
        same = (r // DN_CHUNK) == (c // DN_CHUNK)
        tri = jnp.where(same & (c <= r), 1.0, 0.0).astype(F32)
        blk = jnp.where(same, 1.0, 0.0).astype(F32)
        gcum = jnp.dot(tri, g, precision=HIGHEST, preferred_element_type=F32)
        gcum_s[...] = gcum
        gtot_s[...] = jnp.dot(blk, g, precision=HIGHEST, preferred_element_type=F32)
        beta_s[...] = _sigmoid(ab)
        gcum_ref[...] = gcum

    q = l2norm(conv_silu(q_ref, qh_ref, wq_ref, win_q)) * (DN_HEAD_DIM ** -0.5)
    k = l2norm(conv_silu(k_ref, kh_ref, wk_ref, win_k))
    v = conv_silu(v_ref, vh_ref, wv_ref, win_v)

    def head_gates(ref, first_lane):
        rolled = pltpu.roll(ref[...], (LANES - first_lane) % LANES, 1)
        return [jnp.broadcast_to(rolled[:, s:s + 1], (tm, hd)) for s in range(2)]

    gc = head_gates(gcum_s, 2 * h)
    gt = head_gates(gtot_s, 2 * h)
    be = head_gates(beta_s, DN_V_HEADS + 2 * h)
    vb, kbg = [], []
    for s in range(2):
        cols = slice(s * hd, (s + 1) * hd)
        eg = jnp.exp(gc[s])
        qg_ref[:, cols] = (q * eg).astype(BF16)
        kd_ref[:, cols] = (k * jnp.exp(gt[s] - gc[s])).astype(BF16)
        vb.append((v[:, cols] * be[s]).astype(BF16))
        kbg.append((k * be[s] * eg).astype(BF16))
    qb = q.astype(BF16)
    kb = k.astype(BF16)

    n2 = 2 * cs
    r = _iota((n2, n2), 0)
    c = _iota((n2, n2), 1)
    same_head = (r // cs) == (c // cs)
    causal = same_head & (c <= r)
    strict = same_head & (c < r)
    top = _iota((cs, n2), 1) < cs
    nt = (((1,), (1,)), ((), ()))
    chunks = [slice(ci * cs, (ci + 1) * cs) for ci in range(tm // cs)]

    def stack_heads(x, rows):
        return jnp.concatenate([x[0][rows], x[1][rows]], axis=0)

    k_st = [jnp.concatenate([kb[rows], kb[rows]], axis=0) for rows in chunks]
    q_st = [jnp.concatenate([qb[rows], qb[rows]], axis=0) for rows in chunks]
    kk = [lax.dot_general(ks, ks, nt, preferred_element_type=F32) for ks in k_st]
    qk = [lax.dot_general(qs, ks, nt, preferred_element_type=F32) for qs, ks in zip(q_st, k_st)]
    g_col = [stack_heads(gc, rows) for rows in chunks]
    b_col = [stack_heads(be, rows) for rows in chunks]
    decay = [jnp.where(causal, jnp.exp(jnp.where(causal, g - g.T, 0.0)), 0.0) for g in g_col]
    lows = [jnp.where(strict, b * kk_i * d, 0.0) for b, kk_i, d in zip(b_col, kk, decay)]
    tinvs = [t.astype(BF16) for t in _unit_lower_inverses(lows, cs)]
    us = [jnp.dot(t, stack_heads(vb, rows), preferred_element_type=F32).astype(BF16) for t, rows in zip(tinvs, chunks)]
    ws = [jnp.dot(t, stack_heads(kbg, rows), preferred_element_type=F32).astype(BF16) for t, rows in zip(tinvs, chunks)]
    for rows, u2, w2, qk_i, d in zip(chunks, us, ws, qk, decay):
        intra = jnp.where(causal, qk_i * d, 0.0)
        a_ref[rows, :] = jnp.where(top, intra[:cs], intra[cs:]).astype(BF16)
        u_ref[rows, :hd] = u2[:cs]
        u_ref[rows, hd:] = u2[cs:]
        w_ref[rows, :hd] = w2[:cs]
        w_ref[rows, hd:] = w2[cs:]


def gdn_prep(proj, ab, conv_w, a_log, dt_bias):
    t = proj.shape[0]
    tm = GDN_TILE
    halo = 16
    hb = tm // halo
    hd = DN_HEAD_DIM
    nq = DN_QK_HEADS

    def halo_map(off):
        return lambda i, h: (jnp.maximum(i * hb - 1, 0), off + h)

    pad = jnp.zeros((1, LANES - DN_V_HEADS), F32)
    alog = jnp.concatenate([a_log.reshape(1, -1), pad], axis=1)
    dtb = jnp.concatenate([dt_bias.reshape(1, -1), pad], axis=1)
    big = jax.ShapeDtypeStruct((t, DN_V_DIM), BF16)
    small = jax.ShapeDtypeStruct((t, DN_QK_DIM), BF16)
    gshape = jax.ShapeDtypeStruct((t, LANES), F32)
    big_spec = pl.BlockSpec((tm, 2 * hd), lambda i, h: (i, h))
    small_spec = pl.BlockSpec((tm, hd), lambda i, h: (i, h))
    g_spec = pl.BlockSpec((tm, LANES), lambda i, h: (i, 0))
    row = pl.BlockSpec((1, LANES), lambda i, h: (0, 0))
    return pl.pallas_call(
        _gdn_prep_kernel,
        grid=(t // tm, nq),
        in_specs=[
            pl.BlockSpec((tm, hd), lambda i, h: (i, h)),
            pl.BlockSpec((tm, hd), lambda i, h: (i, nq + h)),
            pl.BlockSpec((tm, 2 * hd), lambda i, h: (i, nq + h)),
            pl.BlockSpec((halo, hd), halo_map(0)),
            pl.BlockSpec((halo, hd), halo_map(nq)),
            pl.BlockSpec((halo, 2 * hd), lambda i, h: (jnp.maximum(i * hb - 1, 0), nq + h)),
            pl.BlockSpec((DN_CONV, hd), lambda i, h: (0, h)),
            pl.BlockSpec((DN_CONV, hd), lambda i, h: (0, nq + h)),
            pl.BlockSpec((DN_CONV, 2 * hd), lambda i, h: (0, nq + h)),
            g_spec, row, row,
        ],
        out_specs=[big_spec, big_spec, big_spec, big_spec, small_spec, g_spec],
        out_shape=[big, big, big, big, small, gshape],
        scratch_shapes=[
            pltpu.VMEM((tm + halo, hd), F32), pltpu.VMEM((tm + halo, hd), F32), pltpu.VMEM((tm + halo, 2 * hd), F32),
            pltpu.VMEM((tm, LANES), F32), pltpu.VMEM((tm, LANES), F32), pltpu.VMEM((tm, LANES), F32),
        ],
        compiler_params=_params("parallel", "arbitrary"),
        name="gdn_prep",
    )(proj, proj, proj, proj, proj, proj, conv_w, conv_w, conv_w, ab, alog, dtb)


def _gdn_scan_kernel(glast_ref, qg_ref, kd_ref, u_ref, w_ref, a_ref, z_ref, onw_ref, og_ref, s_ref):
    ci = pl.program_id(0)
    cs = DN_CHUNK
    hd = DN_HEAD_DIM
    nh = DN_V_HEADS

    @pl.when(ci == 0)
    def _():
        s_ref[...] = jnp.zeros_like(s_ref)

    onw = onw_ref[...]
    tn = (((0,), (0,)), ((), ()))
    lane = _iota((cs, 2 * cs), 1)
    cols = [slice(hv * hd, (hv + 1) * hd) for hv in range(nh)]
    states = [s_ref[hv] for hv in range(nh)]
    ws = [jnp.dot(jnp.concatenate([w_ref[:, cols[hv]], qg_ref[:, cols[hv]]], axis=0), states[hv].astype(BF16),
                  preferred_element_type=F32) for hv in range(nh)]
    v_new = [(u_ref[:, cols[hv]].astype(F32) - ws[hv][:cs]).astype(BF16) for hv in range(nh)]
    for hv in range(nh):
        decay = jnp.exp(jnp.full((1, hd), glast_ref[ci, hv], F32))
        s_ref[hv] = states[hv] * decay + lax.dot_general(kd_ref[:, cols[hv]], v_new[hv], tn, preferred_element_type=F32)
    for pair in range(nh // 2):
        a2 = a_ref[:, pair * 2 * cs:(pair + 1) * 2 * cs]
        v2 = jnp.concatenate([v_new[2 * pair], v_new[2 * pair + 1]], axis=0)
        for s in range(2):
            hv = 2 * pair + s
            a_s = jnp.where((lane // cs) == s, a2, jnp.zeros_like(a2))
            o = ws[hv][cs:] + jnp.dot(a_s, v2, preferred_element_type=F32)
            on = o * lax.rsqrt(jnp.mean(o * o, axis=-1, keepdims=True) + NORM_EPS) * onw
            og_ref[:, cols[hv]] = (on * _silu(z_ref[:, cols[hv]].astype(F32))).astype(BF16)


def gdn_scan(glast, qg, kd, u, w, intra, proj, o_norm_w):
    t = qg.shape[0]
    cs = DN_CHUNK
    big_spec = pl.BlockSpec((cs, DN_V_DIM), lambda c, g: (c, 0))
    return pl.pallas_call(
        _gdn_scan_kernel,
        grid_spec=pltpu.PrefetchScalarGridSpec(
            num_scalar_prefetch=1,
            grid=(t // cs,),
            in_specs=[big_spec, big_spec, big_spec, big_spec,
                      pl.BlockSpec((cs, DN_V_HEADS * cs), lambda c, g: (c, 0)),
                      pl.BlockSpec((cs, DN_V_DIM), lambda c, g: (c, DN_CONV_CH // DN_V_DIM)),
                      pl.BlockSpec((1, DN_HEAD_DIM), lambda c, g: (0, 0))],
            out_specs=big_spec,
            scratch_shapes=[pltpu.VMEM((DN_V_HEADS, DN_HEAD_DIM, DN_HEAD_DIM), F32)],
        ),
        out_shape=jax.ShapeDtypeStruct((t, DN_V_DIM), BF16),
        compiler_params=_params("arbitrary"),
        name="gdn_scan",
    )(glast, qg, kd, u, w, intra, proj, o_norm_w.reshape(1, -1))


def gated_deltanet_block(x, h, gate, w_in, conv_w, a_log, dt_bias, o_norm_w, w_out):
    t = x.shape[0]
    n_main = DN_CONV_CH + DN_V_DIM
    w_main = w_in[:, :n_main].astype(BF16)
    w_ab = jnp.pad(w_in[:, n_main:], ((0, 0), (0, LANES - 2 * DN_V_HEADS))).astype(BF16)
    proj = matmul(h, w_main, mode="plain", out_dtype=BF16, tm=1024, tn=512)
    ab = matmul(h, w_ab, mode="plain", out_dtype=F32, tm=1024, tn=LANES)
    qg, kd, u, w, intra, gcum = gdn_prep(proj, ab, conv_w, a_log, dt_bias)
    glast = gcum.reshape(t // DN_CHUNK, DN_CHUNK, LANES)[:, DN_CHUNK - 1, :DN_V_HEADS]
    og = gdn_scan(glast, qg, kd, u, w, intra, proj, o_norm_w)
    zero_bias = jnp.zeros((1, D_MODEL), F32)
    return matmul(og, w_out.astype(BF16), mode="residual", out_dtype=F32, tm=512, tn=512,
                  bias=zero_bias, res=x, gate=gate)


def _sgu_spatial_kernel(u_ref, v_ref, lnw_ref, lnb_ref, wsp_ref, bsp_ref, o_ref):
    cs = SGU_CHUNK
    gd = SGU_WIDTH // SGU_GROUPS
    v = v_ref[...].astype(F32)
    mu = jnp.mean(v, axis=-1, keepdims=True)
    var = jnp.mean(jnp.square(v - mu), axis=-1, keepdims=True)
    vn = ((v - mu) * lax.rsqrt(var + NORM_EPS) * lnw_ref[...] + lnb_ref[...]).astype(BF16)
    r = _iota((cs, cs), 0)
    c = _iota((cs, cs), 1)
    bsp = bsp_ref[...]
    for g in range(SGU_GROUPS):
        cols = slice(g * gd, (g + 1) * gd)
        wg = jnp.where(c <= r, wsp_ref[g], 0.0).astype(BF16)
        sp = jnp.dot(wg, vn[:, cols], preferred_element_type=F32) + bsp[:, g:g + 1]
        o_ref[:, cols] = (u_ref[:, cols].astype(F32) * sp).astype(BF16)


def sgu_spatial(zz, ln_w, ln_b, w_sp, b_sp):
    t = zz.shape[0]
    cs = SGU_CHUNK
    wd = SGU_WIDTH
    bsp_t = jnp.pad(b_sp.T, ((0, 0), (0, LANES - SGU_GROUPS)))
    row = pl.BlockSpec((1, wd), lambda i: (0, 0))
    return pl.pallas_call(
        _sgu_spatial_kernel,
        grid=(t // cs,),
        in_specs=[
            pl.BlockSpec((cs, wd), lambda i: (i, 0)),
            pl.BlockSpec((cs, wd), lambda i: (i, 1)),
            row, row,
            pl.BlockSpec((SGU_GROUPS, cs, cs), lambda i: (0, 0, 0)),
            pl.BlockSpec((cs, LANES), lambda i: (0, 0)),
        ],
        out_specs=pl.BlockSpec((cs, wd), lambda i: (i, 0)),
        out_shape=jax.ShapeDtypeStruct((t, wd), BF16),
        compiler_params=_params("parallel"),
        name="sgu_spatial",
    )(zz, zz, ln_w.reshape(1, -1), ln_b.reshape(1, -1), w_sp, bsp_t)


def chunked_gmlp_block(x, h, gate, w_in, b_in, ln_w, ln_b, w_sp, b_sp, w_out, b_out):
    zz = matmul(h, w_in.astype(BF16), mode="bias_gelu", out_dtype=BF16, tm=1024, tn=512, bias=b_in.reshape(1, -1))
    su = sgu_spatial(zz, ln_w, ln_b, w_sp, b_sp)
    return matmul(su, w_out.astype(BF16), mode="residual", out_dtype=F32, tm=512, tn=512,
                  bias=b_out.reshape(1, -1), res=x, gate=gate)


def _router_kernel(x_ref, w_ref, sc_ref, sh_ref, wr_ref, br_ref, h_ref, idx_ref, gate_ref, pos_ref, cnt_ref, carry):
    i = pl.program_id(0)
    tm = x_ref.shape[0]

    @pl.when(i == 0)
    def _():
        carry[...] = jnp.zeros_like(carry)

    h = _norm_mod(x_ref[...], w_ref[...], sc_ref[...], sh_ref[...])
    h_ref[...] = h
    lane = _iota((tm, LANES), 1).astype(F32)
    neg = jnp.float32(-jnp.inf)
    logits = jnp.dot(h, wr_ref[...], precision=HIGHEST, preferred_element_type=F32) + br_ref[...]
    logits = jnp.where(lane < N_EXPERTS, logits, neg)
    vals, idxs = [], []
    for _ in range(TOP_K):
        m = jnp.max(logits, axis=-1, keepdims=True)
        ix = jnp.min(jnp.where(logits == m, lane, float(LANES)), axis=-1, keepdims=True)
        vals.append(m)
        idxs.append(ix)
        logits = jnp.where(lane == ix, neg, logits)
    es = [jnp.exp(v - vals[0]) for v in vals]
    denom = es[0] + es[1] + es[2] + es[3]
    multi = jnp.zeros((tm, LANES), F32)
    for ix in idxs:
        multi = jnp.where(lane == ix, 1.0, multi)
    r = _iota((tm, tm), 0)
    c = _iota((tm, tm), 1)
    before = jnp.where(c < r, 1.0, 0.0).astype(BF16)
    rank = jnp.dot(before, multi.astype(BF16), preferred_element_type=F32) + carry[0:1, :]
    idx_t = jnp.zeros((tm, LANES), F32)
    gate_t = jnp.zeros((tm, LANES), F32)
    pos_t = jnp.zeros((tm, LANES), F32)
    for kk in range(TOP_K):
        pk = jnp.sum(jnp.where(lane == idxs[kk], rank, 0.0), axis=-1, keepdims=True)
        idx_t = jnp.where(lane == kk, idxs[kk], idx_t)
        gate_t = jnp.where(lane == kk, es[kk] / denom, gate_t)
        pos_t = jnp.where(lane == kk, pk, pos_t)
    idx_ref[...] = idx_t.astype(I32)
    gate_ref[...] = gate_t
    pos_ref[...] = pos_t.astype(I32)
    carry[...] = carry[...] + jnp.sum(multi, axis=0, keepdims=True)
    cnt_ref[...] = carry[...]


def moe_router(x, w, sc, sh, w_router, b_router):
    t, d = x.shape
    tm = 256
    wr = jnp.pad(w_router, ((0, 0), (0, LANES - N_EXPERTS)))
    br = jnp.pad(b_router.reshape(1, -1), ((0, 0), (0, LANES - N_EXPERTS)))
    vec = pl.BlockSpec((1, d), lambda i: (0, 0))
    tile = pl.BlockSpec((tm, LANES), lambda i: (i, 0))
    return pl.pallas_call(
        _router_kernel,
        grid=(t // tm,),
        in_specs=[pl.BlockSpec((tm, d), lambda i: (i, 0)), vec, vec, vec,
                  pl.BlockSpec((d, LANES), lambda i: (0, 0)), pl.BlockSpec((1, LANES), lambda i: (0, 0))],
        out_specs=[pl.BlockSpec((tm, d), lambda i: (i, 0)), tile, tile, tile, pl.BlockSpec((8, LANES), lambda i: (0, 0))],
        out_shape=[jax.ShapeDtypeStruct((t, d), F32), jax.ShapeDtypeStruct((t, LANES), I32),
                   jax.ShapeDtypeStruct((t, LANES), F32), jax.ShapeDtypeStruct((t, LANES), I32),
                   jax.ShapeDtypeStruct((8, LANES), F32)],
        scratch_shapes=[pltpu.VMEM((8, LANES), F32)],
        compiler_params=_params("arbitrary"),
        name="moe_router",
    )(x, w, sc, sh, wr, br)


def _cast_kernel(x_ref, o_ref):
    o_ref[...] = x_ref[...].astype(o_ref.dtype)


def cast_layer_bf16(w, layer):
    _, e, k, n = w.shape
    tn = 1024
    return pl.pallas_call(
        _cast_kernel,
        grid=(e, n // tn),
        in_specs=[pl.BlockSpec((None, None, k, tn), lambda ei, j: (layer, ei, 0, j))],
        out_specs=pl.BlockSpec((None, k, tn), lambda ei, j: (ei, 0, j)),
        out_shape=jax.ShapeDtypeStruct((e, k, n), BF16),
        compiler_params=_params("parallel", "parallel"),
        name="cast_bf16",
    )(w)


def _dispatch_kernel(dest_ref, ztile_ref, h_ref, xs_ref, zbuf, sem, zsem):
    i = pl.program_id(0)
    tm = h_ref.shape[0]
    zt = zbuf.shape[0]

    @pl.when(i == 0)
    def _():
        zbuf[...] = jnp.zeros_like(zbuf)

        def zero_copy(e):
            row = pl.multiple_of(ztile_ref[e], zt)
            return pltpu.make_async_copy(zbuf, xs_ref.at[pl.ds(row, zt), :], zsem)

        for e in range(N_EXPERTS):
            @pl.when(ztile_ref[e] >= 0)
            def _():
                zero_copy(e).start()
        for e in range(N_EXPERTS):
            @pl.when(ztile_ref[e] >= 0)
            def _():
                zero_copy(e).wait()

        def zero_tail(j, carry):
            row = pl.multiple_of(j * zt, zt)
            cp = pltpu.make_async_copy(zbuf, xs_ref.at[pl.ds(row, zt), :], zsem)
            cp.start()
            cp.wait()
            return carry

        lax.fori_loop(ztile_ref[N_EXPERTS], xs_ref.shape[0] // zt, zero_tail, 0)

    def row_copy(r, d):
        return pltpu.make_async_copy(h_ref.at[pl.ds(r, 1), :], xs_ref.at[pl.ds(d, 1), :], sem)

    def issue(r, carry):
        base = (i * tm + r) * TOP_K
        for kk in range(TOP_K):
            row_copy(r, dest_ref[base + kk]).start(priority=kk % 2)
        return carry

    lax.fori_loop(0, tm, issue, 0)

    for kk in range(TOP_K):
        pltpu.make_async_copy(h_ref, xs_ref.at[pl.ds(0, tm), :], sem).wait()


def moe_dispatch(dest, zero_plan, h, n_rows):
    t, d = h.shape
    tm = 256
    return pl.pallas_call(
        _dispatch_kernel,
        grid_spec=pltpu.PrefetchScalarGridSpec(
            num_scalar_prefetch=2,
            grid=(t // tm,),
            in_specs=[pl.BlockSpec((tm, d), lambda i, dst, zt: (i, 0))],
            out_specs=pl.BlockSpec(memory_space=pl.ANY),
            scratch_shapes=[pltpu.VMEM((MOE_TILE, d), F32), pltpu.SemaphoreType.DMA(()), pltpu.SemaphoreType.DMA(())],
        ),
        out_shape=jax.ShapeDtypeStruct((n_rows, d), F32),
        compiler_params=_params("arbitrary"),
        name="moe_dispatch",
    )(dest.reshape(-1), zero_plan, h)


def _ffn_up_kernel(te_ref, nu_ref, x_ref, w_ref, b_ref, o_ref):
    i = pl.program_id(0)
    f = EXPERT_DIM
    fc = 512

    @pl.when(i < nu_ref[0])
    def _():
        xb = x_ref[...].astype(BF16)
        for c in range(f // fc):
            g = jnp.dot(xb, w_ref[:, c * fc:(c + 1) * fc], preferred_element_type=F32) + b_ref[:, c * fc:(c + 1) * fc]
            u = jnp.dot(xb, w_ref[:, f + c * fc:f + (c + 1) * fc], preferred_element_type=F32) + b_ref[:, f + c * fc:f + (c + 1) * fc]
            x_glu = jnp.minimum(g, SWIGLU_LIMIT)
            x_lin = jnp.clip(u, -SWIGLU_LIMIT, SWIGLU_LIMIT)
            act = x_glu * _sigmoid(SWIGLU_ALPHA * x_glu) * (x_lin + 1.0)
            o_ref[:, c * fc:(c + 1) * fc] = act.astype(BF16)

    @pl.when(i >= nu_ref[0])
    def _():
        o_ref[...] = jnp.zeros_like(o_ref)


def _ffn_down_kernel(te_ref, nu_ref, a_ref, w_ref, b_ref, o_ref):
    i = pl.program_id(0)

    @pl.when(i < nu_ref[0])
    def _():
        o_ref[...] = jnp.dot(a_ref[...], w_ref[...], preferred_element_type=F32) + b_ref[...]

    @pl.when(i >= nu_ref[0])
    def _():
        o_ref[...] = jnp.zeros_like(o_ref)


def moe_experts(tile_expert, n_used, xs, w_gate_up, b_gate_up, w_down, b_down):
    n_rows, d = xs.shape
    tm = MOE_TILE
    n_tiles = n_rows // tm
    f = EXPERT_DIM
    act = pl.pallas_call(
        _ffn_up_kernel,
        grid_spec=pltpu.PrefetchScalarGridSpec(
            num_scalar_prefetch=2,
            grid=(n_tiles,),
            in_specs=[pl.BlockSpec((tm, d), lambda i, te, nu: (jnp.minimum(i, nu[0] - 1), 0)),
                      pl.BlockSpec((None, d, 2 * f), lambda i, te, nu: (te[i], 0, 0)),
                      pl.BlockSpec((None, 1, 2 * f), lambda i, te, nu: (te[i], 0, 0))],
            out_specs=pl.BlockSpec((tm, f), lambda i, te, nu: (i, 0)),
        ),
        out_shape=jax.ShapeDtypeStruct((n_rows, f), BF16),
        compiler_params=_params("arbitrary"),
        name="moe_ffn_up",
    )(tile_expert, n_used, xs, w_gate_up, b_gate_up.reshape(N_EXPERTS, 1, 2 * f))
    return pl.pallas_call(
        _ffn_down_kernel,
        grid_spec=pltpu.PrefetchScalarGridSpec(
            num_scalar_prefetch=2,
            grid=(n_tiles,),
            in_specs=[pl.BlockSpec((tm, f), lambda i, te, nu: (i, 0)),
                      pl.BlockSpec((None, f, d), lambda i, te, nu: (te[i], 0, 0)),
                      pl.BlockSpec((None, 1, d), lambda i, te, nu: (te[i], 0, 0))],
            out_specs=pl.BlockSpec((tm, d), lambda i, te, nu: (i, 0)),
        ),
        out_shape=jax.ShapeDtypeStruct((n_rows, d), F32),
        compiler_params=_params("arbitrary"),
        name="moe_ffn_down",
    )(tile_expert, n_used, act, w_down, b_down.reshape(N_EXPERTS, 1, d))


def _combine_kernel(dest_ref, x_ref, gate_ref, g2_ref, y_ref, o_ref, ybuf, sem):
    i = pl.program_id(0)
    n = pl.num_programs(0)
    tm = x_ref.shape[0]

    def row_copy(tile, slot, r, kk):
        d = dest_ref[(tile * tm + r) * TOP_K + kk]
        return pltpu.make_async_copy(y_ref.at[pl.ds(d, 1), :], ybuf.at[slot, kk, pl.ds(r, 1), :], sem.at[slot])

    def issue_tile(tile, slot):
        def body(r, carry):
            for kk in range(TOP_K):
                row_copy(tile, slot, r, kk).start(priority=kk % 2)
            return carry
        lax.fori_loop(0, tm, body, 0)

    def drain_tile(slot):
        for kk in range(TOP_K):
            pltpu.make_async_copy(y_ref.at[pl.ds(0, tm), :], ybuf.at[slot, kk], sem.at[slot]).wait()

    slot = i % 2

    @pl.when(i == 0)
    def _():
        issue_tile(0, 0)

    @pl.when(i + 1 < n)
    def _():
        issue_tile(i + 1, 1 - slot)

    drain_tile(slot)
    gates = gate_ref[...]
    acc = ybuf[slot, 0] * gates[:, 0:1]
    for kk in range(1, TOP_K):
        acc = acc + ybuf[slot, kk] * gates[:, kk:kk + 1]
    o_ref[...] = x_ref[...] + g2_ref[...] * acc


def moe_combine(dest, x, gates, gate2, y):
    t, d = x.shape
    tm = 128
    return pl.pallas_call(
        _combine_kernel,
        grid_spec=pltpu.PrefetchScalarGridSpec(
            num_scalar_prefetch=1,
            grid=(t // tm,),
            in_specs=[pl.BlockSpec((tm, d), lambda i, dst: (i, 0)),
                      pl.BlockSpec((tm, LANES), lambda i, dst: (i, 0)),
                      pl.BlockSpec((1, d), lambda i, dst: (0, 0)),
                      pl.BlockSpec(memory_space=pl.ANY)],
            out_specs=pl.BlockSpec((tm, d), lambda i, dst: (i, 0)),
            scratch_shapes=[pltpu.VMEM((2, TOP_K, tm, d), F32), pltpu.SemaphoreType.DMA((2,))],
        ),
        out_shape=jax.ShapeDtypeStruct((t, d), F32),
        compiler_params=_params("arbitrary"),
        name="moe_combine",
    )(dest.reshape(-1), x, gates, gate2, y)


def moe_block(x, norm_w, sc, sh, gate2, w_router, b_router, w_gate_up, b_gate_up, w_down, b_down):
    t, d = x.shape
    tm = MOE_TILE
    h, idx, gates, pos, counts = moe_router(x, norm_w, sc, sh, w_router, b_router)
    cnt = counts[0, :N_EXPERTS].astype(I32)
    padded = (cnt + tm - 1) // tm * tm
    pend = jnp.cumsum(padded)
    pstart = pend - padded
    n_tiles = (t * TOP_K + N_EXPERTS * (tm - 1) + tm - 1) // tm
    is_expert = idx[:, :TOP_K, None] == jnp.arange(N_EXPERTS, dtype=I32)
    dest = pos[:, :TOP_K] + jnp.sum(jnp.where(is_expert, pstart, 0), axis=-1)
    tile_start = jnp.arange(n_tiles, dtype=I32) * tm
    tile_expert = jnp.minimum(jnp.sum((pend[None, :] <= tile_start[:, None]).astype(I32), axis=1), N_EXPERTS - 1)
    n_used = (pend[-1:] // tm).astype(I32)
    zero_plan = jnp.concatenate([jnp.where(padded > 0, pend - tm, -1), n_used]).astype(I32)
    xs = moe_dispatch(dest, zero_plan, h, n_tiles * tm)
    y = moe_experts(tile_expert, n_used, xs, w_gate_up, b_gate_up, w_down, b_down)
    return moe_combine(dest, x, gates, gate2, y)


def kernel(x, c, ada_w, ada_b, norm_w, dn_w_in, dn_conv_w, dn_a_log, dn_dt_bias, dn_o_norm_w, dn_w_out, sgu_w_in, sgu_b_in, sgu_ln_w, sgu_ln_b, sgu_w_sp, sgu_b_sp, sgu_w_out, sgu_b_out, moe_w_router, moe_b_router, moe_w_gate_up, moe_b_gate_up, moe_w_down, moe_b_down, final_norm_w):
    bsz, seq, d = x.shape
    assert bsz == 1 and d == D_MODEL
    xt = x.reshape(seq, d)
    mod = ada_mod(c, ada_w, ada_b)
    for i in range(DEPTH):
        sh1, sc1, gt1, sh2, sc2, gt2 = [mod[i:i + 1, s * d:(s + 1) * d] for s in range(6)]
        h = norm_mod(xt, norm_w[i, 0:1], sc1, sh1, BF16)
        j = i // 2
        if i % 2 == 0:
            xt = gated_deltanet_block(xt, h, gt1, dn_w_in[j], dn_conv_w[j], dn_a_log[j], dn_dt_bias[j],
                                      dn_o_norm_w[j], dn_w_out[j])
        else:
            xt = chunked_gmlp_block(xt, h, gt1, sgu_w_in[j], sgu_b_in[j], sgu_ln_w[j], sgu_ln_b[j], sgu_w_sp[j],
                                    sgu_b_sp[j], sgu_w_out[j], sgu_b_out[j])
        xt = moe_block(xt, norm_w[i, 1:2], sc2, sh2, gt2, moe_w_router[i], moe_b_router[i],
                       cast_layer_bf16(moe_w_gate_up, i), moe_b_gate_up[i], cast_layer_bf16(moe_w_down, i), moe_b_down[i])
    zero = jnp.zeros((1, d), F32)
    out = norm_mod(xt, final_norm_w.reshape(1, d), zero, zero, F32)
    return out.reshape(bsz, seq, d)
```

```python
import functools

import jax
import jax.numpy as jnp
from jax import lax
from jax.experimental import pallas as pl
from jax.experimental.pallas import tpu as pltpu

F32 = jnp.float32
BF16 = jnp.bfloat16
I32 = jnp.int32
HIGHEST = lax.Precision.HIGHEST

D_MODEL = 2048
DEPTH = 2
NORM_EPS = 1e-6
DN_HEAD_DIM = 128
DN_QK_HEADS = 16
DN_V_HEADS = 32
DN_QK_DIM = 2048
DN_V_DIM = 4096
DN_CONV_CH = 8192
DN_CONV = 4
DN_CHUNK = 64
SGU_WIDTH = 4096
SGU_CHUNK = 128
SGU_GROUPS = 32
N_EXPERTS = 32
TOP_K = 4
EXPERT_DIM = 2048
SWIGLU_LIMIT = 7.0
SWIGLU_ALPHA = 1.702

LANES = 128
VMEM_LIMIT = 56 * 1024 * 1024
MOE_TILE = 256
GDN_TILE = 512


def _params(*sem):
    return pltpu.CompilerParams(dimension_semantics=sem, vmem_limit_bytes=VMEM_LIMIT)


def _iota(shape, dim):
    return lax.broadcasted_iota(I32, shape, dim)


def _sigmoid(x):
    return 1.0 / (1.0 + jnp.exp(-x))


def _silu(x):
    return x * _sigmoid(x)


def _ada_kernel(c_ref, w_ref, b_ref, o_ref):
    c = c_ref[...]
    o_ref[...] = jnp.dot(_silu(c), w_ref[...], precision=HIGHEST, preferred_element_type=F32) + b_ref[...]


def ada_mod(c, ada_w, ada_b):
    depth, d, n = ada_w.shape
    tn = 1024
    c8 = jnp.broadcast_to(c, (8, d))
    out = pl.pallas_call(
        _ada_kernel,
        grid=(depth, n // tn),
        in_specs=[
            pl.BlockSpec((8, d), lambda l, j: (0, 0)),
            pl.BlockSpec((None, d, tn), lambda l, j: (l, 0, j)),
            pl.BlockSpec((None, 1, tn), lambda l, j: (l, 0, j)),
        ],
        out_specs=pl.BlockSpec((None, 8, tn), lambda l, j: (l, 0, j)),
        out_shape=jax.ShapeDtypeStruct((depth, 8, n), F32),
        compiler_params=_params("parallel", "parallel"),
        name="ada_mod",
    )(c8, ada_w, ada_b.reshape(depth, 1, n))
    return out[:, 0, :]


def _norm_mod(x, w, sc, sh):
    y = x * lax.rsqrt(jnp.mean(x * x, axis=-1, keepdims=True) + NORM_EPS)
    return (y * w) * (1.0 + sc) + sh


def _norm_mod_kernel(x_ref, w_ref, sc_ref, sh_ref, o_ref):
    o_ref[...] = _norm_mod(x_ref[...], w_ref[...], sc_ref[...], sh_ref[...]).astype(o_ref.dtype)


def norm_mod(x, w, sc, sh, out_dtype):
    t, d = x.shape
    tm = 512
    vec = pl.BlockSpec((1, d), lambda i: (0, 0))
    return pl.pallas_call(
        _norm_mod_kernel,
        grid=(t // tm,),
        in_specs=[pl.BlockSpec((tm, d), lambda i: (i, 0)), vec, vec, vec],
        out_specs=pl.BlockSpec((tm, d), lambda i: (i, 0)),
        out_shape=jax.ShapeDtypeStruct((t, d), out_dtype),
        compiler_params=_params("parallel"),
        name="norm_mod",
    )(x, w, sc, sh)


def _gelu_exact(x):
    return 0.5 * x * (1.0 + lax.erf(x * (2.0 ** -0.5)))


def _mm_kernel(a_ref, w_ref, *refs, mode):
    acc = jnp.dot(a_ref[...], w_ref[...], preferred_element_type=F32)
    if mode == "plain":
        (o_ref,) = refs
        o_ref[...] = acc.astype(o_ref.dtype)
    elif mode == "bias_gelu":
        b_ref, o_ref = refs
        o_ref[...] = _gelu_exact(acc + b_ref[...]).astype(o_ref.dtype)
    else:
        b_ref, res_ref, gate_ref, o_ref = refs
        o_ref[...] = res_ref[...] + gate_ref[...] * (acc + b_ref[...])


def matmul(a, w, *, mode, out_dtype, tm, tn, bias=None, res=None, gate=None):
    m, k = a.shape
    n = w.shape[1]
    assert m % tm == 0 and n % tn == 0
    row = pl.BlockSpec((1, tn), lambda i, j: (0, j))
    in_specs = [pl.BlockSpec((tm, k), lambda i, j: (i, 0)), pl.BlockSpec((k, tn), lambda i, j: (0, j))]
    args = [a, w]
    if mode != "plain":
        in_specs.append(row)
        args.append(bias)
    if mode == "residual":
        in_specs += [pl.BlockSpec((tm, tn), lambda i, j: (i, j)), row]
        args += [res, gate]
    return pl.pallas_call(
        functools.partial(_mm_kernel, mode=mode),
        grid=(m // tm, n // tn),
        in_specs=in_specs,
        out_specs=pl.BlockSpec((tm, tn), lambda i, j: (i, j)),
        out_shape=jax.ShapeDtypeStruct((m, n), out_dtype),
        compiler_params=_params("parallel", "parallel"),
        name="matmul_" + mode,
    )(*args)


def _softplus(x):
    return jnp.maximum(x, 0.0) + jnp.log1p(jnp.exp(-jnp.abs(x)))


def _unit_lower_inverses(lows, n_block):
    n = lows[0].shape[0]
    r = _iota((n, n), 0)
    c = _iota((n, n), 1)
    eye = jnp.where(r == c, 1.0, 0.0).astype(F32)

    def mm(a, b):
        return jnp.dot(a.astype(BF16), b.astype(BF16), preferred_element_type=F32)

    base = (r // 8) == (c // 8)
    ds = [jnp.where(base, low, 0.0) for low in lows]
    xs = [eye - d for d in ds]
    ps = [mm(d, d) for d in ds]
    xs = [x + mm(x, p) for x, p in zip(xs, ps)]
    ps = [mm(p, p) for p in ps]
    xs = [x + mm(x, p) for x, p in zip(xs, ps)]
    s = 8
    while s < n_block:
        pair = ((r // (2 * s)) == (c // (2 * s))) & ((r // s) != (c // s))
        ts = [mm(jnp.where(pair, low, 0.0), x) for low, x in zip(lows, xs)]
        xs = [x - mm(x, t) for x, t in zip(xs, ts)]
        s *= 2
    return xs


def _gdn_prep_kernel(q_ref, k_ref, v_ref, qh_ref, kh_ref, vh_ref, wq_ref, wk_ref, wv_ref, ab_ref, alog_ref, dtb_ref,
                     qg_ref, kd_ref, u_ref, w_ref, a_ref, gcum_ref,
                     win_q, win_k, win_v, gcum_s, gtot_s, beta_s):
    i = pl.program_id(0)
    h = pl.program_id(1)
    tm = q_ref.shape[0]
    halo = qh_ref.shape[0]
    cs = DN_CHUNK
    hd = DN_HEAD_DIM

    def conv_silu(x_ref, halo_ref, w_ref, win_ref):
        hal = halo_ref[...].astype(F32)
        win_ref[0:halo, :] = jnp.where(i == 0, jnp.zeros_like(hal), hal)
        win_ref[halo:halo + tm, :] = x_ref[...].astype(F32)
        w = w_ref[...]
        acc = None
        for j in range(DN_CONV):
            start = halo - (DN_CONV - 1) + j
            term = win_ref[start:start + tm, :] * w[j:j + 1, :]
            acc = term if acc is None else acc + term
        return _silu(acc)

    def l2norm(x):
        return x * lax.rsqrt(jnp.sum(x * x, axis=-1, keepdims=True) + NORM_EPS)

    @pl.when(h == 0)
    def _():
        ab = ab_ref[...]
        g = -jnp.exp(alog_ref[...]) * _softplus(ab + dtb_ref[...])
        r = _iota((tm, tm), 0)
        c = _iota((tm, tm), 1)
        same = (r // DN_CHUNK) == (c // DN_CHUNK)
        tri = jnp.where(same & (c <= r), 1.0, 0.0).astype(F32)
        blk = jnp.where(same, 1.0, 0.0).astype(F32)
        gcum = jnp.dot(tri, g, precision=HIGHEST, preferred_element_type=F32)
        gcum_s[...] = gcum
        gtot_s[...] = jnp.dot(blk, g, precision=HIGHEST, preferred_element_type=F32)
        beta_s[...] = _sigmoid(ab)
        gcum_ref[...] = gcum

    q = l2norm(conv_silu(q_ref, qh_ref, wq_ref, win_q)) * (DN_HEAD_DIM ** -0.5)
    k = l2norm(conv_silu(k_ref, kh_ref, wk_ref, win_k))
    v = conv_silu(v_ref, vh_ref, wv_ref, win_v)

    def head_gates(ref, first_lane):
        rolled = pltpu.roll(ref[...], (LANES - first_lane) % LANES, 1)
        return [jnp.broadcast_to(rolled[:, s:s + 1], (tm, hd)) for s in range(2)]

    gc = head_gates(gcum_s, 2 * h)
    gt = head_gates(gtot_s, 2 * h)
    be = head_gates(beta_s, DN_V_HEADS + 2 * h)
    vb, kbg = [], []
    for s in range(2):
        cols = slice(s * hd, (s + 1) * hd)
        eg = jnp.exp(gc[s])
        qg_ref[:, cols] = (q * eg).astype(BF16)
        kd_ref[:, cols] = (k * jnp.exp(gt[s] - gc[s])).astype(BF16)
        vb.append((v[:, cols] * be[s]).astype(BF16))
        kbg.append((k * be[s] * eg).astype(BF16))
    qb = q.astype(BF16)
    kb = k.astype(BF16)

    n2 = 2 * cs
    r = _iota((n2, n2), 0)
    c = _iota((n2, n2), 1)
    same_head = (r // cs) == (c // cs)
    causal = same_head & (c <= r)
    strict = same_head & (c < r)
    top = _iota((cs, n2), 1) < cs
    nt = (((1,), (1,)), ((), ()))
    chunks = [slice(ci * cs, (ci + 1) * cs) for ci in range(tm // cs)]

    def stack_heads(x, rows):
        return jnp.concatenate([x[0][rows], x[1][rows]], axis=0)

    k_st = [jnp.concatenate([kb[rows], kb[rows]], axis=0) for rows in chunks]
    q_st = [jnp.concatenate([qb[rows], qb[rows]], axis=0) for rows in chunks]
    kk = [lax.dot_general(ks, ks, nt, preferred_element_type=F32) for ks in k_st]
    qk = [lax.dot_general(qs, ks, nt, preferred_element_type=F32) for qs, ks in zip(q_st, k_st)]
    g_col = [stack_heads(gc, rows) for rows in chunks]
    b_col = [stack_heads(be, rows) for rows in chunks]
    decay = [jnp.where(causal, jnp.exp(jnp.where(causal, g - g.T, 0.0)), 0.0) for g in g_col]
    lows = [jnp.where(strict, b * kk_i * d, 0.0) for b, kk_i, d in zip(b_col, kk, decay)]
    tinvs = [t.astype(BF16) for t in _unit_lower_inverses(lows, cs)]
    us = [jnp.dot(t, stack_heads(vb, rows), preferred_element_type=F32).astype(BF16) for t, rows in zip(tinvs, chunks)]
    ws = [jnp.dot(t, stack_heads(kbg, rows), preferred_element_type=F32).astype(BF16) for t, rows in zip(tinvs, chunks)]
    for rows, u2, w2, qk_i, d in zip(chunks, us, ws, qk, decay):
        intra = jnp.where(causal, qk_i * d, 0.0)
        a_ref[rows, :] = jnp.where(top, intra[:cs], intra[cs:]).astype(BF16)
        u_ref[rows, :hd] = u2[:cs]
        u_ref[rows, hd:] = u2[cs:]
        w_ref[rows, :hd] = w2[:cs]
        w_ref[rows, hd:] = w2[cs:]


def gdn_prep(proj, ab, conv_w, a_log, dt_bias):
    t = proj.shape[0]
    tm = GDN_TILE
    halo = 16
    hb = tm // halo
    hd = DN_HEAD_DIM
    nq = DN_QK_HEADS

    def halo_map(off):
        return lambda i, h: (jnp.maximum(i * hb - 1, 0), off + h)

    pad = jnp.zeros((1, LANES - DN_V_HEADS), F32)
    alog = jnp.concatenate([a_log.reshape(1, -1), pad], axis=1)
    dtb = jnp.concatenate([dt_bias.reshape(1, -1), pad], axis=1)
    big = jax.ShapeDtypeStruct((t, DN_V_DIM), BF16)
    small = jax.ShapeDtypeStruct((t, DN_QK_DIM), BF16)
    gshape = jax.ShapeDtypeStruct((t, LANES), F32)
    big_spec = pl.BlockSpec((tm, 2 * hd), lambda i, h: (i, h))
    small_spec = pl.BlockSpec((tm, hd), lambda i, h: (i, h))
    g_spec = pl.BlockSpec((tm, LANES), lambda i, h: (i, 0))
    row = pl.BlockSpec((1, LANES), lambda i, h: (0, 0))
    return pl.pallas_call(
        _gdn_prep_kernel,
        grid=(t // tm, nq),
        in_specs=[
            pl.BlockSpec((tm, hd), lambda i, h: (i, h)),
            pl.BlockSpec((tm, hd), lambda i, h: (i, nq + h)),
            pl.BlockSpec((tm, 2 * hd), lambda i, h: (i, nq + h)),
            pl.BlockSpec((halo, hd), halo_map(0)),
            pl.BlockSpec((halo, hd), halo_map(nq)),
            pl.BlockSpec((halo, 2 * hd), lambda i, h: (jnp.maximum(i * hb - 1, 0), nq + h)),
            pl.BlockSpec((DN_CONV, hd), lambda i, h: (0, h)),
            pl.BlockSpec((DN_CONV, hd), lambda i, h: (0, nq + h)),
            pl.BlockSpec((DN_CONV, 2 * hd), lambda i, h: (0, nq + h)),
            g_spec, row, row,
        ],
        out_specs=[big_spec, big_spec, big_spec, big_spec, small_spec, g_spec],
        out_shape=[big, big, big, big, small, gshape],
        scratch_shapes=[
            pltpu.VMEM((tm + halo, hd), F32), pltpu.VMEM((tm + halo, hd), F32), pltpu.VMEM((tm + halo, 2 * hd), F32),
            pltpu.VMEM((tm, LANES), F32), pltpu.VMEM((tm, LANES), F32), pltpu.VMEM((tm, LANES), F32),
        ],
        compiler_params=_params("parallel", "arbitrary"),
        name="gdn_prep",
    )(proj, proj, proj, proj, proj, proj, conv_w, conv_w, conv_w, ab, alog, dtb)


def _gdn_scan_kernel(glast_ref, qg_ref, kd_ref, u_ref, w_ref, a_ref, z_ref, onw_ref, og_ref, s_ref):
    ci = pl.program_id(0)
    cs = DN_CHUNK
    hd = DN_HEAD_DIM
    nh = DN_V_HEADS

    @pl.when(ci == 0)
    def _():
        s_ref[...] = jnp.zeros_like(s_ref)

    onw = onw_ref[...]
    tn = (((0,), (0,)), ((), ()))
    lane = _iota((cs, 2 * cs), 1)
    cols = [slice(hv * hd, (hv + 1) * hd) for hv in range(nh)]
    states = [s_ref[hv] for hv in range(nh)]
    ws = [jnp.dot(jnp.concatenate([w_ref[:, cols[hv]], qg_ref[:, cols[hv]]], axis=0), states[hv].astype(BF16),
                  preferred_element_type=F32) for hv in range(nh)]
    v_new = [(u_ref[:, cols[hv]].astype(F32) - ws[hv][:cs]).astype(BF16) for hv in range(nh)]
    for hv in range(nh):
        decay = jnp.exp(jnp.full((1, hd), glast_ref[ci, hv], F32))
        s_ref[hv] = states[hv] * decay + lax.dot_general(kd_ref[:, cols[hv]], v_new[hv], tn, preferred_element_type=F32)
    for pair in range(nh // 2):
        a2 = a_ref[:, pair * 2 * cs:(pair + 1) * 2 * cs]
        v2 = jnp.concatenate([v_new[2 * pair], v_new[2 * pair + 1]], axis=0)
        for s in range(2):
            hv = 2 * pair + s
            a_s = jnp.where((lane // cs) == s, a2, jnp.zeros_like(a2))
            o = ws[hv][cs:] + jnp.dot(a_s, v2, preferred_element_type=F32)
            on = o * lax.rsqrt(jnp.mean(o * o, axis=-1, keepdims=True) + NORM_EPS) * onw
            og_ref[:, cols[hv]] = (on * _silu(z_ref[:, cols[hv]].astype(F32))).astype(BF16)


def gdn_scan(glast, qg, kd, u, w, intra, proj, o_norm_w):
    t = qg.shape[0]
    cs = DN_CHUNK
    big_spec = pl.BlockSpec((cs, DN_V_DIM), lambda c, g: (c, 0))
    return pl.pallas_call(
        _gdn_scan_kernel,
        grid_spec=pltpu.PrefetchScalarGridSpec(
            num_scalar_prefetch=1,
            grid=(t // cs,),
            in_specs=[big_spec, big_spec, big_spec, big_spec,
                      pl.BlockSpec((cs, DN_V_HEADS * cs), lambda c, g: (c, 0)),
                      pl.BlockSpec((cs, DN_V_DIM), lambda c, g: (c, DN_CONV_CH // DN_V_DIM)),
                      pl.BlockSpec((1, DN_HEAD_DIM), lambda c, g: (0, 0))],
            out_specs=big_spec,
            scratch_shapes=[pltpu.VMEM((DN_V_HEADS, DN_HEAD_DIM, DN_HEAD_DIM), F32)],
        ),
        out_shape=jax.ShapeDtypeStruct((t, DN_V_DIM), BF16),
        compiler_params=_params("arbitrary"),
        name="gdn_scan",
    )(glast, qg, kd, u, w, intra, proj, o_norm_w.reshape(1, -1))


def gated_deltanet_block(x, h, gate, w_in, conv_w, a_log, dt_bias, o_norm_w, w_out):
    t = x.shape[0]
    n_main = DN_CONV_CH + DN_V_DIM
    w_main = w_in[:, :n_main].astype(BF16)
    w_ab = jnp.pad(w_in[:, n_main:], ((0, 0), (0, LANES - 2 * DN_V_HEADS))).astype(BF16)
    proj = matmul(h, w_main, mode="plain", out_dtype=BF16, tm=1024, tn=512)
    ab = matmul(h, w_ab, mode="plain", out_dtype=F32, tm=1024, tn=LANES)
    qg, kd, u, w, intra, gcum = gdn_prep(proj, ab, conv_w, a_log, dt_bias)
    glast = gcum.reshape(t // DN_CHUNK, DN_CHUNK, LANES)[:, DN_CHUNK - 1, :DN_V_HEADS]
    og = gdn_scan(glast, qg, kd, u, w, intra, proj, o_norm_w)
    zero_bias = jnp.zeros((1, D_MODEL), F32)
    return matmul(og, w_out.astype(BF16), mode="residual", out_dtype=F32, tm=512, tn=512,
                  bias=zero_bias, res=x, gate=gate)


def _sgu_spatial_kernel(u_ref, v_ref, lnw_ref, lnb_ref, wsp_ref, bsp_ref, o_ref):
    cs = SGU_CHUNK
    gd = SGU_WIDTH // SGU_GROUPS
    v = v_ref[...].astype(F32)
    mu = jnp.mean(v, axis=-1, keepdims=True)
    var = jnp.mean(jnp.square(v - mu), axis=-1, keepdims=True)
    vn = ((v - mu) * lax.rsqrt(var + NORM_EPS) * lnw_ref[...] + lnb_ref[...]).astype(BF16)
    r = _iota((cs, cs), 0)
    c = _iota((cs, cs), 1)
    bsp = bsp_ref[...]
    for g in range(SGU_GROUPS):
        cols = slice(g * gd, (g + 1) * gd)
        wg = jnp.where(c <= r, wsp_ref[g], 0.0).astype(BF16)
        sp = jnp.dot(wg, vn[:, cols], preferred_element_type=F32) + bsp[:, g:g + 1]
        o_ref[:, cols] = (u_ref[:, cols].astype(F32) * sp).astype(BF16)


def sgu_spatial(zz, ln_w, ln_b, w_sp, b_sp):
    t = zz.shape[0]
    cs = SGU_CHUNK
    wd = SGU_WIDTH
    bsp_t = jnp.pad(b_sp.T, ((0, 0), (0, LANES - SGU_GROUPS)))
    row = pl.BlockSpec((1, wd), lambda i: (0, 0))
    return pl.pallas_call(
        _sgu_spatial_kernel,
        grid=(t // cs,),
        in_specs=[
            pl.BlockSpec((cs, wd), lambda i: (i, 0)),
            pl.BlockSpec((cs, wd), lambda i: (i, 1)),
            row, row,
            pl.BlockSpec((SGU_GROUPS, cs, cs), lambda i: (0, 0, 0)),
            pl.BlockSpec((cs, LANES), lambda i: (0, 0)),
        ],
        out_specs=pl.BlockSpec((cs, wd), lambda i: (i, 0)),
        out_shape=jax.ShapeDtypeStruct((t, wd), BF16),
        compiler_params=_params("parallel"),
        name="sgu_spatial",
    )(zz, zz, ln_w.reshape(1, -1), ln_b.reshape(1, -1), w_sp, bsp_t)


def chunked_gmlp_block(x, h, gate, w_in, b_in, ln_w, ln_b, w_sp, b_sp, w_out, b_out):
    zz = matmul(h, w_in.astype(BF16), mode="bias_gelu", out_dtype=BF16, tm=1024, tn=512, bias=b_in.reshape(1, -1))
    su = sgu_spatial(zz, ln_w, ln_b, w_sp, b_sp)
    return matmul(su, w_out.astype(BF16), mode="residual", out_dtype=F32, tm=512, tn=512,
                  bias=b_out.reshape(1, -1), res=x, gate=gate)


def _router_kernel(x_ref, w_ref, sc_ref, sh_ref, wr_ref, br_ref, h_ref, idx_ref, gate_ref, pos_ref, cnt_ref, carry):
    i = pl.program_id(0)
    tm = x_ref.shape[0]

    @pl.when(i == 0)
    def _():
        carry[...] = jnp.zeros_like(carry)

    h = _norm_mod(x_ref[...], w_ref[...], sc_ref[...], sh_ref[...])
    h_ref[...] = h
    lane = _iota((tm, LANES), 1).astype(F32)
    neg = jnp.float32(-jnp.inf)
    logits = jnp.dot(h, wr_ref[...], precision=HIGHEST, preferred_element_type=F32) + br_ref[...]
    logits = jnp.where(lane < N_EXPERTS, logits, neg)
    vals, idxs = [], []
    for _ in range(TOP_K):
        m = jnp.max(logits, axis=-1, keepdims=True)
        ix = jnp.min(jnp.where(logits == m, lane, float(LANES)), axis=-1, keepdims=True)
        vals.append(m)
        idxs.append(ix)
        logits = jnp.where(lane == ix, neg, logits)
    es = [jnp.exp(v - vals[0]) for v in vals]
    denom = es[0] + es[1] + es[2] + es[3]
    multi = jnp.zeros((tm, LANES), F32)
    for ix in idxs:
        multi = jnp.where(lane == ix, 1.0, multi)
    r = _iota((tm, tm), 0)
    c = _iota((tm, tm), 1)
    before = jnp.where(c < r, 1.0, 0.0).astype(BF16)
    rank = jnp.dot(before, multi.astype(BF16), preferred_element_type=F32) + carry[0:1, :]
    idx_t = jnp.zeros((tm, LANES), F32)
    gate_t = jnp.zeros((tm, LANES), F32)
    pos_t = jnp.zeros((tm, LANES), F32)
    for kk in range(TOP_K):
        pk = jnp.sum(jnp.where(lane == idxs[kk], rank, 0.0), axis=-1, keepdims=True)
        idx_t = jnp.where(lane == kk, idxs[kk], idx_t)
        gate_t = jnp.where(lane == kk, es[kk] / denom, gate_t)
        pos_t = jnp.where(lane == kk, pk, pos_t)
    idx_ref[...] = idx_t.astype(I32)
    gate_ref[...] = gate_t
    pos_ref[...] = pos_t.astype(I32)
    carry[...] = carry[...] + jnp.sum(multi, axis=0, keepdims=True)
    cnt_ref[...] = carry[...]


def moe_router(x, w, sc, sh, w_router, b_router):
    t, d = x.shape
    tm = 256
    wr = jnp.pad(w_router, ((0, 0), (0, LANES - N_EXPERTS)))
    br = jnp.pad(b_router.reshape(1, -1), ((0, 0), (0, LANES - N_EXPERTS)))
    vec = pl.BlockSpec((1, d), lambda i: (0, 0))
    tile = pl.BlockSpec((tm, LANES), lambda i: (i, 0))
    return pl.pallas_call(
        _router_kernel,
        grid=(t // tm,),
        in_specs=[pl.BlockSpec((tm, d), lambda i: (i, 0)), vec, vec, vec,
                  pl.BlockSpec((d, LANES), lambda i: (0, 0)), pl.BlockSpec((1, LANES), lambda i: (0, 0))],
        out_specs=[pl.BlockSpec((tm, d), lambda i: (i, 0)), tile, tile, tile, pl.BlockSpec((8, LANES), lambda i: (0, 0))],
        out_shape=[jax.ShapeDtypeStruct((t, d), F32), jax.ShapeDtypeStruct((t, LANES), I32),
                   jax.ShapeDtypeStruct((t, LANES), F32), jax.ShapeDtypeStruct((t, LANES), I32),
                   jax.ShapeDtypeStruct((8, LANES), F32)],
        scratch_shapes=[pltpu.VMEM((8, LANES), F32)],
        compiler_params=_params("arbitrary"),
        name="moe_router",
    )(x, w, sc, sh, wr, br)


def _dispatch_kernel(dest_ref, ztile_ref, h_ref, xs_ref, zbuf, sem, zsem):
    i = pl.program_id(0)
    tm = h_ref.shape[0]
    zt = zbuf.shape[0]

    @pl.when(i == 0)
    def _():
        zbuf[...] = jnp.zeros_like(zbuf)

        def zero_copy(e):
            row = pl.multiple_of(ztile_ref[e], zt)
            return pltpu.make_async_copy(zbuf, xs_ref.at[pl.ds(row, zt), :], zsem)

        for e in range(N_EXPERTS):
            @pl.when(ztile_ref[e] >= 0)
            def _():
                zero_copy(e).start()
        for e in range(N_EXPERTS):
            @pl.when(ztile_ref[e] >= 0)
            def _():
                zero_copy(e).wait()

        def zero_tail(j, carry):
            row = pl.multiple_of(j * zt, zt)
            cp = pltpu.make_async_copy(zbuf, xs_ref.at[pl.ds(row, zt), :], zsem)
            cp.start()
            cp.wait()
            return carry

        lax.fori_loop(ztile_ref[N_EXPERTS], xs_ref.shape[0] // zt, zero_tail, 0)

    def row_copy(r, d):
        return pltpu.make_async_copy(h_ref.at[pl.ds(r, 1), :], xs_ref.at[pl.ds(d, 1), :], sem)

    def issue(r, carry):
        base = (i * tm + r) * TOP_K
        for kk in range(TOP_K):
            row_copy(r, dest_ref[base + kk]).start(priority=kk % 2)
        return carry

    lax.fori_loop(0, tm, issue, 0)

    for kk in range(TOP_K):
        pltpu.make_async_copy(h_ref, xs_ref.at[pl.ds(0, tm), :], sem).wait()


def moe_dispatch(dest, zero_plan, h, n_rows):
    t, d = h.shape
    tm = 256
    return pl.pallas_call(
        _dispatch_kernel,
        grid_spec=pltpu.PrefetchScalarGridSpec(
            num_scalar_prefetch=2,
            grid=(t // tm,),
            in_specs=[pl.BlockSpec((tm, d), lambda i, dst, zt: (i, 0))],
            out_specs=pl.BlockSpec(memory_space=pl.ANY),
            scratch_shapes=[pltpu.VMEM((MOE_TILE, d), F32), pltpu.SemaphoreType.DMA(()), pltpu.SemaphoreType.DMA(())],
        ),
        out_shape=jax.ShapeDtypeStruct((n_rows, d), F32),
        compiler_params=_params("arbitrary"),
        name="moe_dispatch",
    )(dest.reshape(-1), zero_plan, h)


def _stream_expert_weights(w_hbm, wbuf, stage, sem, layer, e_cur, e_next, tig, gprev, cur):
    n_pairs = wbuf.shape[1] // 2
    cw = stage.shape[-1]

    def chunk_copy(e, c):
        return pltpu.make_async_copy(w_hbm.at[layer, e, :, pl.ds(c * cw, cw)], stage.at[c % 2], sem.at[c % 2])

    def start_pair(e, p):
        chunk_copy(e, 2 * p).start()
        chunk_copy(e, 2 * p + 1).start()

    def retire_pair(e, p, slot):
        for c in (2 * p, 2 * p + 1):
            chunk_copy(e, c).wait()
            wbuf[slot, c] = stage[c % 2].astype(BF16)

    @pl.when(tig == 0)
    def _():
        for p in range(n_pairs):
            @pl.when(gprev <= p)
            def _():
                start_pair(e_cur, p)

            @pl.when(gprev <= p + 1)
            def _():
                retire_pair(e_cur, p, cur)

    @pl.when(e_next >= 0)
    def _():
        for j in range(n_pairs + 1):
            @pl.when(tig == j)
            def _():
                if j >= 1:
                    retire_pair(e_next, j - 1, 1 - cur)
                if j < n_pairs:
                    start_pair(e_next, j)


def _ffn_up_kernel(te_ref, tig_ref, gprev_ref, nxt_ref, gidx_ref, nu_ref, x_ref, w_hbm, b_ref, o_ref,
                   wbuf, stage, sem, *, layer):
    i = pl.program_id(0)
    f = o_ref.shape[1]
    nc = wbuf.shape[1] // 2
    fc = wbuf.shape[-1]

    @pl.when(i < nu_ref[0])
    def _():
        cur = gidx_ref[i] % 2
        _stream_expert_weights(w_hbm, wbuf, stage, sem, layer, te_ref[i], nxt_ref[i], tig_ref[i], gprev_ref[i], cur)
        xb = x_ref[...].astype(BF16)
        for c in range(nc):
            g = jnp.dot(xb, wbuf[cur, c], preferred_element_type=F32) + b_ref[:, c * fc:(c + 1) * fc]
            u = jnp.dot(xb, wbuf[cur, nc + c], preferred_element_type=F32) + b_ref[:, f + c * fc:f + (c + 1) * fc]
            x_glu = jnp.minimum(g, SWIGLU_LIMIT)
            x_lin = jnp.clip(u, -SWIGLU_LIMIT, SWIGLU_LIMIT)
            act = x_glu * _sigmoid(SWIGLU_ALPHA * x_glu) * (x_lin + 1.0)
            o_ref[:, c * fc:(c + 1) * fc] = act.astype(BF16)

    @pl.when(i >= nu_ref[0])
    def _():
        o_ref[...] = jnp.zeros_like(o_ref)


def _ffn_down_kernel(te_ref, tig_ref, gprev_ref, nxt_ref, gidx_ref, nu_ref, a_ref, w_hbm, b_ref, o_ref,
                     wbuf, stage, sem, *, layer):
    i = pl.program_id(0)
    fc = wbuf.shape[-1]

    @pl.when(i < nu_ref[0])
    def _():
        cur = gidx_ref[i] % 2
        _stream_expert_weights(w_hbm, wbuf, stage, sem, layer, te_ref[i], nxt_ref[i], tig_ref[i], gprev_ref[i], cur)
        a = a_ref[...]
        for c in range(wbuf.shape[1]):
            cols = slice(c * fc, (c + 1) * fc)
            o_ref[:, cols] = jnp.dot(a, wbuf[cur, c], preferred_element_type=F32) + b_ref[:, cols]

    @pl.when(i >= nu_ref[0])
    def _():
        o_ref[...] = jnp.zeros_like(o_ref)


def moe_experts(plan, xs, layer, w_gate_up, b_gate_up, w_down, b_down):
    n_rows, d = xs.shape
    tm = MOE_TILE
    n_tiles = n_rows // tm
    f = w_down.shape[2]
    cw = f // 4
    n_plan = len(plan)
    expert_row = lambda i, te, *_: (te[i], 0, 0)
    any_spec = pl.BlockSpec(memory_space=pl.ANY)
    act = pl.pallas_call(
        functools.partial(_ffn_up_kernel, layer=layer),
        grid_spec=pltpu.PrefetchScalarGridSpec(
            num_scalar_prefetch=n_plan,
            grid=(n_tiles,),
            in_specs=[pl.BlockSpec((tm, d), lambda i, *p: (jnp.minimum(i, p[-1][0] - 1), 0)),
                      any_spec,
                      pl.BlockSpec((None, 1, 2 * f), expert_row)],
            out_specs=pl.BlockSpec((tm, f), lambda i, *p: (i, 0)),
            scratch_shapes=[pltpu.VMEM((2, 2 * f // cw, d, cw), BF16), pltpu.VMEM((2, d, cw), F32),
                            pltpu.SemaphoreType.DMA((2,))],
        ),
        out_shape=jax.ShapeDtypeStruct((n_rows, f), BF16),
        compiler_params=_params("arbitrary"),
        name="moe_ffn_up",
    )(*plan, xs, w_gate_up, b_gate_up.reshape(N_EXPERTS, 1, 2 * f))
    return pl.pallas_call(
        functools.partial(_ffn_down_kernel, layer=layer),
        grid_spec=pltpu.PrefetchScalarGridSpec(
            num_scalar_prefetch=n_plan,
            grid=(n_tiles,),
            in_specs=[pl.BlockSpec((tm, f), lambda i, *p: (i, 0)),
                      any_spec,
                      pl.BlockSpec((None, 1, d), expert_row)],
            out_specs=pl.BlockSpec((tm, d), lambda i, *p: (i, 0)),
            scratch_shapes=[pltpu.VMEM((2, d // cw, f, cw), BF16), pltpu.VMEM((2, f, cw), F32),
                            pltpu.SemaphoreType.DMA((2,))],
        ),
        out_shape=jax.ShapeDtypeStruct((n_rows, d), F32),
        compiler_params=_params("arbitrary"),
        name="moe_ffn_down",
    )(*plan, act, w_down, b_down.reshape(N_EXPERTS, 1, d))


def _combine_kernel(dest_ref, x_ref, gate_ref, g2_ref, y_ref, o_ref, ybuf, sem):
    i = pl.program_id(0)
    n = pl.num_programs(0)
    tm = x_ref.shape[0]

    def row_copy(tile, slot, r, kk):
        d = dest_ref[(tile * tm + r) * TOP_K + kk]
        return pltpu.make_async_copy(y_ref.at[pl.ds(d, 1), :], ybuf.at[slot, kk, pl.ds(r, 1), :], sem.at[slot])

    def issue_tile(tile, slot):
        def body(r, carry):
            for kk in range(TOP_K):
                row_copy(tile, slot, r, kk).start(priority=kk % 2)
            return carry
        lax.fori_loop(0, tm, body, 0)

    def drain_tile(slot):
        for kk in range(TOP_K):
            pltpu.make_async_copy(y_ref.at[pl.ds(0, tm), :], ybuf.at[slot, kk], sem.at[slot]).wait()

    slot = i % 2

    @pl.when(i == 0)
    def _():
        issue_tile(0, 0)

    @pl.when(i + 1 < n)
    def _():
        issue_tile(i + 1, 1 - slot)

    drain_tile(slot)
    gates = gate_ref[...]
    acc = ybuf[slot, 0] * gates[:, 0:1]
    for kk in range(1, TOP_K):
        acc = acc + ybuf[slot, kk] * gates[:, kk:kk + 1]
    o_ref[...] = x_ref[...] + g2_ref[...] * acc


def moe_combine(dest, x, gates, gate2, y):
    t, d = x.shape
    tm = 128
    return pl.pallas_call(
        _combine_kernel,
        grid_spec=pltpu.PrefetchScalarGridSpec(
            num_scalar_prefetch=1,
            grid=(t // tm,),
            in_specs=[pl.BlockSpec((tm, d), lambda i, dst: (i, 0)),
                      pl.BlockSpec((tm, LANES), lambda i, dst: (i, 0)),
                      pl.BlockSpec((1, d), lambda i, dst: (0, 0)),
                      pl.BlockSpec(memory_space=pl.ANY)],
            out_specs=pl.BlockSpec((tm, d), lambda i, dst: (i, 0)),
            scratch_shapes=[pltpu.VMEM((2, TOP_K, tm, d), F32), pltpu.SemaphoreType.DMA((2,))],
        ),
        out_shape=jax.ShapeDtypeStruct((t, d), F32),
        compiler_params=_params("arbitrary"),
        name="moe_combine",
    )(dest.reshape(-1), x, gates, gate2, y)


def moe_block(x, norm_w, sc, sh, gate2, layer, w_router, b_router, w_gate_up, b_gate_up, w_down, b_down):
    t, d = x.shape
    tm = MOE_TILE
    h, idx, gates, pos, counts = moe_router(x, norm_w, sc, sh, w_router, b_router)
    cnt = counts[0, :N_EXPERTS].astype(I32)
    padded = (cnt + tm - 1) // tm * tm
    pend = jnp.cumsum(padded)
    pstart = pend - padded
    n_tiles = (t * TOP_K + N_EXPERTS * (tm - 1) + tm - 1) // tm
    is_expert = idx[:, :TOP_K, None] == jnp.arange(N_EXPERTS, dtype=I32)
    dest = pos[:, :TOP_K] + jnp.sum(jnp.where(is_expert, pstart, 0), axis=-1)
    tile_start = jnp.arange(n_tiles, dtype=I32) * tm
    tile_expert = jnp.minimum(jnp.sum((pend[None, :] <= tile_start[:, None]).astype(I32), axis=1), N_EXPERTS - 1)
    n_used = (pend[-1:] // tm).astype(I32)
    zero_plan = jnp.concatenate([jnp.where(padded > 0, pend - tm, -1), n_used]).astype(I32)
    xs = moe_dispatch(dest, zero_plan, h, n_tiles * tm)
    experts = jnp.arange(N_EXPERTS, dtype=I32)
    group_tiles = padded // tm
    nonempty = group_tiles > 0
    later = (experts[None, :] > experts[:, None]) & nonempty[None, :]
    earlier = (experts[None, :] < experts[:, None]) & nonempty[None, :]
    next_e = jnp.min(jnp.where(later, experts[None, :], N_EXPERTS), axis=1)
    prev_e = jnp.max(jnp.where(earlier, experts[None, :], -1), axis=1)
    next_e = jnp.where(next_e < N_EXPERTS, next_e, -1)
    prev_tiles = jnp.where(prev_e >= 0, group_tiles[jnp.maximum(prev_e, 0)], 0)
    ordinal = jnp.cumsum(nonempty.astype(I32)) - nonempty.astype(I32)
    tile_in_group = jnp.arange(n_tiles, dtype=I32) - (pstart // tm)[tile_expert]
    plan = tuple(a.astype(I32) for a in (tile_expert, tile_in_group, prev_tiles[tile_expert], next_e[tile_expert],
                                         ordinal[tile_expert], n_used))
    y = moe_experts(plan, xs, layer, w_gate_up, b_gate_up, w_down, b_down)
    return moe_combine(dest, x, gates, gate2, y)


def kernel(x, c, ada_w, ada_b, norm_w, dn_w_in, dn_conv_w, dn_a_log, dn_dt_bias, dn_o_norm_w, dn_w_out, sgu_w_in, sgu_b_in, sgu_ln_w, sgu_ln_b, sgu_w_sp, sgu_b_sp, sgu_w_out, sgu_b_out, moe_w_router, moe_b_router, moe_w_gate_up, moe_b_gate_up, moe_w_down, moe_b_down, final_norm_w):
    bsz, seq, d = x.shape
    assert bsz == 1 and d == D_MODEL
    xt = x.reshape(seq, d)
    mod = ada_mod(c, ada_w, ada_b)
    for i in range(DEPTH):
        sh1, sc1, gt1, sh2, sc2, gt2 = [mod[i:i + 1, s * d:(s + 1) * d] for s in range(6)]
        h = norm_mod(xt, norm_w[i, 0:1], sc1, sh1, BF16)
        j = i // 2
        if i % 2 == 0:
            xt = gated_deltanet_block(xt, h, gt1, dn_w_in[j], dn_conv_w[j], dn_a_log[j], dn_dt_bias[j],
                                      dn_o_norm_w[j], dn_w_out[j])
        else:
            xt = chunked_gmlp_block(xt, h, gt1, sgu_w_in[j], sgu_b_in[j], sgu_ln_w[j], sgu_ln_b[j], sgu_w_sp[j],
                                    sgu_b_sp[j], sgu_w_out[j], sgu_b_out[j])
        xt = moe_block(xt, norm_w[i, 1:2], sc2, sh2, gt2, i, moe_w_router[i], moe_b_router[i],
                       moe_w_gate_up, moe_b_gate_up[i], moe_w_down, moe_b_down[i])
    zero = jnp.zeros((1, d), F32)
    out = norm_mod(xt, final_norm_w.reshape(1, d), zero, zero, F32)
    return out.reshape(bsz, seq, d)
```

```python
import functools

import jax
import jax.numpy as jnp
from jax import lax
from jax.experimental import pallas as pl
from jax.experimental.pallas import tpu as pltpu

F32 = jnp.float32
BF16 = jnp.bfloat16
I32 = jnp.int32
HIGHEST = lax.Precision.HIGHEST

D_MODEL = 2048
DEPTH = 2
NORM_EPS = 1e-6
DN_HEAD_DIM = 128
DN_QK_HEADS = 16
DN_V_HEADS = 32
DN_QK_DIM = 2048
DN_V_DIM = 4096
DN_CONV_CH = 8192
DN_CONV = 4
DN_CHUNK = 64
SGU_WIDTH = 4096
SGU_CHUNK = 128
SGU_GROUPS = 32
N_EXPERTS = 32
TOP_K = 4
EXPERT_DIM = 2048
SWIGLU_LIMIT = 7.0
SWIGLU_ALPHA = 1.702

LANES = 128
VMEM_LIMIT = 58 * 1024 * 1024
MOE_TILE = 512
GDN_TILE = 512


def _params(*sem):
    return pltpu.CompilerParams(dimension_semantics=sem, vmem_limit_bytes=VMEM_LIMIT)


def _iota(shape, dim):
    return lax.broadcasted_iota(I32, shape, dim)


def _sigmoid(x):
    return 1.0 / (1.0 + jnp.exp(-x))


def _silu(x):
    return x * _sigmoid(x)


def _ada_kernel(c_ref, w_ref, b_ref, o_ref):
    c = c_ref[...]
    o_ref[...] = jnp.dot(_silu(c), w_ref[...], precision=HIGHEST, preferred_element_type=F32) + b_ref[...]


def ada_mod(c, ada_w, ada_b):
    depth, d, n = ada_w.shape
    tn = 1024
    c8 = jnp.broadcast_to(c, (8, d))
    out = pl.pallas_call(
        _ada_kernel,
        grid=(depth, n // tn),
        in_specs=[
            pl.BlockSpec((8, d), lambda l, j: (0, 0)),
            pl.BlockSpec((None, d, tn), lambda l, j: (l, 0, j)),
            pl.BlockSpec((None, 1, tn), lambda l, j: (l, 0, j)),
        ],
        out_specs=pl.BlockSpec((None, 8, tn), lambda l, j: (l, 0, j)),
        out_shape=jax.ShapeDtypeStruct((depth, 8, n), F32),
        compiler_params=_params("parallel", "parallel"),
        name="ada_mod",
    )(c8, ada_w, ada_b.reshape(depth, 1, n))
    return out[:, 0, :]


def _norm_mod(x, w, sc, sh):
    y = x * lax.rsqrt(jnp.mean(x * x, axis=-1, keepdims=True) + NORM_EPS)
    return (y * w) * (1.0 + sc) + sh


def _norm_mod_kernel(x_ref, w_ref, sc_ref, sh_ref, o_ref):
    o_ref[...] = _norm_mod(x_ref[...], w_ref[...], sc_ref[...], sh_ref[...]).astype(o_ref.dtype)


def norm_mod(x, w, sc, sh, out_dtype):
    t, d = x.shape
    tm = 512
    vec = pl.BlockSpec((1, d), lambda i: (0, 0))
    return pl.pallas_call(
        _norm_mod_kernel,
        grid=(t // tm,),
        in_specs=[pl.BlockSpec((tm, d), lambda i: (i, 0)), vec, vec, vec],
        out_specs=pl.BlockSpec((tm, d), lambda i: (i, 0)),
        out_shape=jax.ShapeDtypeStruct((t, d), out_dtype),
        compiler_params=_params("parallel"),
        name="norm_mod",
    )(x, w, sc, sh)


def _gelu_exact(x):
    return 0.5 * x * (1.0 + lax.erf(x * (2.0 ** -0.5)))


def _mm_kernel(a_ref, w_ref, *refs, mode):
    *refs, wb = refs
    @pl.when(pl.program_id(1) == 0)
    def _():
        wb[...] = w_ref[...].astype(BF16)

    acc = jnp.dot(a_ref[...], wb[...], preferred_element_type=F32)
    if mode == "plain":
        (o_ref,) = refs
        o_ref[...] = acc.astype(o_ref.dtype)
    elif mode == "bias_gelu":
        b_ref, o_ref = refs
        o_ref[...] = _gelu_exact(acc + b_ref[...]).astype(o_ref.dtype)
    else:
        b_ref, res_ref, gate_ref, o_ref = refs
        o_ref[...] = res_ref[...] + gate_ref[...] * (acc + b_ref[...])


def matmul(a, w, n, *, mode, out_dtype, tm, tn, bias=None, res=None, gate=None):
    m, k = a.shape
    assert m % tm == 0 and n % tn == 0 and w.shape[1] >= n
    row = pl.BlockSpec((1, tn), lambda j, i: (0, j))
    in_specs = [pl.BlockSpec((tm, k), lambda j, i: (i, 0)), pl.BlockSpec((k, tn), lambda j, i: (0, j))]
    args = [a, w]
    if mode != "plain":
        in_specs.append(row)
        args.append(bias)
    if mode == "residual":
        in_specs += [pl.BlockSpec((tm, tn), lambda j, i: (i, j)), row]
        args += [res, gate]
    return pl.pallas_call(
        functools.partial(_mm_kernel, mode=mode),
        grid=(n // tn, m // tm),
        in_specs=in_specs,
        out_specs=pl.BlockSpec((tm, tn), lambda j, i: (i, j)),
        out_shape=jax.ShapeDtypeStruct((m, n), out_dtype),
        scratch_shapes=[pltpu.VMEM((k, tn), BF16)],
        compiler_params=_params("parallel", "arbitrary"),
        name="matmul_" + mode,
    )(*args)


def _softplus(x):
    return jnp.maximum(x, 0.0) + jnp.log1p(jnp.exp(-jnp.abs(x)))


def _unit_lower_inverses(lows, n_block):
    n = lows[0].shape[0]
    r = _iota((n, n), 0)
    c = _iota((n, n), 1)
    eye = jnp.where(r == c, 1.0, 0.0).astype(F32)

    def mm(a, b):
        return jnp.dot(a.astype(BF16), b.astype(BF16), preferred_element_type=F32)

    base = (r // 8) == (c // 8)
    ds = [jnp.where(base, low, 0.0) for low in lows]
    xs = [eye - d for d in ds]
    ps = [mm(d, d) for d in ds]
    xs = [x + mm(x, p) for x, p in zip(xs, ps)]
    ps = [mm(p, p) for p in ps]
    xs = [x + mm(x, p) for x, p in zip(xs, ps)]
    s = 8
    while s < n_block:
        pair = ((r // (2 * s)) == (c // (2 * s))) & ((r // s) != (c // s))
        ts = [mm(jnp.where(pair, low, 0.0), x) for low, x in zip(lows, xs)]
        xs = [x - mm(x, t) for x, t in zip(xs, ts)]
        s *= 2
    return xs


def _gdn_prep_kernel(q_ref, k_ref, v_ref, qh_ref, kh_ref, vh_ref, wq_ref, wk_ref, wv_ref, ab_ref, alog_ref, dtb_ref,
                     qg_ref, kd_ref, u_ref, w_ref, a_ref, gcum_ref,
                     win_q, win_k, win_v, gcum_s, gtot_s, beta_s):
    i = pl.program_id(0)
    h = pl.program_id(1)
    tm = q_ref.shape[0]
    halo = qh_ref.shape[0]
    cs = DN_CHUNK
    hd = DN_HEAD_DIM

    def conv_silu(x_ref, halo_ref, w_ref, win_ref):
        hal = halo_ref[...].astype(F32)
        win_ref[0:halo, :] = jnp.where(i == 0, jnp.zeros_like(hal), hal)
        win_ref[halo:halo + tm, :] = x_ref[...].astype(F32)
        w = w_ref[...]
        acc = None
        for j in range(DN_CONV):
            start = halo - (DN_CONV - 1) + j
            term = win_ref[start:start + tm, :] * w[j:j + 1, :]
            acc = term if acc is None else acc + term
        return _silu(acc)

    def l2norm(x):
        return x * lax.rsqrt(jnp.sum(x * x, axis=-1, keepdims=True) + NORM_EPS)

    @pl.when(h == 0)
    def _():
        ab = ab_ref[...]
        g = -jnp.exp(alog_ref[...]) * _softplus(ab + dtb_ref[...])
        r = _iota((tm, tm), 0)
        c = _iota((tm, tm), 1)
        same = (r // DN_CHUNK) == (c // DN_CHUNK)
        tri = jnp.where(same & (c <= r), 1.0, 0.0).astype(F32)
        blk = jnp.where(same, 1.0, 0.0).astype(F32)
        gcum = jnp.dot(tri, g, precision=HIGHEST, preferred_element_type=F32)
        gcum_s[...] = gcum
        gtot_s[...] = jnp.dot(blk, g, precision=HIGHEST, preferred_element_type=F32)
        beta_s[...] = _sigmoid(ab)
        gcum_ref[...] = gcum

    q = l2norm(conv_silu(q_ref, qh_ref, wq_ref, win_q)) * (DN_HEAD_DIM ** -0.5)
    k = l2norm(conv_silu(k_ref, kh_ref, wk_ref, win_k))
    v = conv_silu(v_ref, vh_ref, wv_ref, win_v)

    def head_gates(ref, first_lane):
        rolled = pltpu.roll(ref[...], (LANES - first_lane) % LANES, 1)
        return [jnp.broadcast_to(rolled[:, s:s + 1], (tm, hd)) for s in range(2)]

    gc = head_gates(gcum_s, 2 * h)
    gt = head_gates(gtot_s, 2 * h)
    be = head_gates(beta_s, DN_V_HEADS + 2 * h)
    vb, kbg = [], []
    for s in range(2):
        cols = slice(s * hd, (s + 1) * hd)
        eg = jnp.exp(gc[s])
        qg_ref[:, cols] = (q * eg).astype(BF16)
        kd_ref[:, cols] = (k * jnp.exp(gt[s] - gc[s])).astype(BF16)
        vb.append((v[:, cols] * be[s]).astype(BF16))
        kbg.append((k * be[s] * eg).astype(BF16))
    qb = q.astype(BF16)
    kb = k.astype(BF16)

    n2 = 2 * cs
    r = _iota((n2, n2), 0)
    c = _iota((n2, n2), 1)
    same_head = (r // cs) == (c // cs)
    causal = same_head & (c <= r)
    strict = same_head & (c < r)
    top = _iota((cs, n2), 1) < cs
    nt = (((1,), (1,)), ((), ()))
    chunks = [slice(ci * cs, (ci + 1) * cs) for ci in range(tm // cs)]

    def stack_heads(x, rows):
        return jnp.concatenate([x[0][rows], x[1][rows]], axis=0)

    k_st = [jnp.concatenate([kb[rows], kb[rows]], axis=0) for rows in chunks]
    q_st = [jnp.concatenate([qb[rows], qb[rows]], axis=0) for rows in chunks]
    kk = [lax.dot_general(ks, ks, nt, preferred_element_type=F32) for ks in k_st]
    qk = [lax.dot_general(qs, ks, nt, preferred_element_type=F32) for qs, ks in zip(q_st, k_st)]
    g_col = [stack_heads(gc, rows) for rows in chunks]
    b_col = [stack_heads(be, rows) for rows in chunks]
    decay = [jnp.where(causal, jnp.exp(jnp.where(causal, g - g.T, 0.0)), 0.0) for g in g_col]
    lows = [jnp.where(strict, b * kk_i * d, 0.0) for b, kk_i, d in zip(b_col, kk, decay)]
    tinvs = [t.astype(BF16) for t in _unit_lower_inverses(lows, cs)]
    us = [jnp.dot(t, stack_heads(vb, rows), preferred_element_type=F32).astype(BF16) for t, rows in zip(tinvs, chunks)]
    ws = [jnp.dot(t, stack_heads(kbg, rows), preferred_element_type=F32).astype(BF16) for t, rows in zip(tinvs, chunks)]
    for rows, u2, w2, qk_i, d in zip(chunks, us, ws, qk, decay):
        intra = jnp.where(causal, qk_i * d, 0.0)
        a_ref[rows, :] = jnp.where(top, intra[:cs], intra[cs:]).astype(BF16)
        u_ref[rows, :hd] = u2[:cs]
        u_ref[rows, hd:] = u2[cs:]
        w_ref[rows, :hd] = w2[:cs]
        w_ref[rows, hd:] = w2[cs:]


def gdn_prep(proj, ab, conv_w, a_log, dt_bias):
    t = proj.shape[0]
    tm = GDN_TILE
    halo = 16
    hb = tm // halo
    hd = DN_HEAD_DIM
    nq = DN_QK_HEADS

    def halo_map(off):
        return lambda i, h: (jnp.maximum(i * hb - 1, 0), off + h)

    pad = jnp.zeros((1, LANES - DN_V_HEADS), F32)
    alog = jnp.concatenate([a_log.reshape(1, -1), pad], axis=1)
    dtb = jnp.concatenate([dt_bias.reshape(1, -1), pad], axis=1)
    big = jax.ShapeDtypeStruct((t, DN_V_DIM), BF16)
    small = jax.ShapeDtypeStruct((t, DN_QK_DIM), BF16)
    gshape = jax.ShapeDtypeStruct((t, LANES), F32)
    big_spec = pl.BlockSpec((tm, 2 * hd), lambda i, h: (i, h))
    small_spec = pl.BlockSpec((tm, hd), lambda i, h: (i, h))
    g_spec = pl.BlockSpec((tm, LANES), lambda i, h: (i, 0))
    row = pl.BlockSpec((1, LANES), lambda i, h: (0, 0))
    return pl.pallas_call(
        _gdn_prep_kernel,
        grid=(t // tm, nq),
        in_specs=[
            pl.BlockSpec((tm, hd), lambda i, h: (i, h)),
            pl.BlockSpec((tm, hd), lambda i, h: (i, nq + h)),
            pl.BlockSpec((tm, 2 * hd), lambda i, h: (i, nq + h)),
            pl.BlockSpec((halo, hd), halo_map(0)),
            pl.BlockSpec((halo, hd), halo_map(nq)),
            pl.BlockSpec((halo, 2 * hd), lambda i, h: (jnp.maximum(i * hb - 1, 0), nq + h)),
            pl.BlockSpec((DN_CONV, hd), lambda i, h: (0, h)),
            pl.BlockSpec((DN_CONV, hd), lambda i, h: (0, nq + h)),
            pl.BlockSpec((DN_CONV, 2 * hd), lambda i, h: (0, nq + h)),
            g_spec, row, row,
        ],
        out_specs=[big_spec, big_spec, big_spec, big_spec, small_spec, g_spec],
        out_shape=[big, big, big, big, small, gshape],
        scratch_shapes=[
            pltpu.VMEM((tm + halo, hd), F32), pltpu.VMEM((tm + halo, hd), F32), pltpu.VMEM((tm + halo, 2 * hd), F32),
            pltpu.VMEM((tm, LANES), F32), pltpu.VMEM((tm, LANES), F32), pltpu.VMEM((tm, LANES), F32),
        ],
        compiler_params=_params("parallel", "arbitrary"),
        name="gdn_prep",
    )(proj, proj, proj, proj, proj, proj, conv_w, conv_w, conv_w, ab, alog, dtb)


def _gdn_scan_kernel(glast_ref, qg_ref, kd_ref, u_ref, w_ref, a_ref, z_ref, onw_ref, og_ref, s_ref):
    ci = pl.program_id(0)
    cs = DN_CHUNK
    hd = DN_HEAD_DIM
    nh = DN_V_HEADS

    @pl.when(ci == 0)
    def _():
        s_ref[...] = jnp.zeros_like(s_ref)

    onw = onw_ref[...]
    tn = (((0,), (0,)), ((), ()))
    lane = _iota((cs, 2 * cs), 1)
    cols = [slice(hv * hd, (hv + 1) * hd) for hv in range(nh)]
    states = [s_ref[hv] for hv in range(nh)]
    ws = [jnp.dot(jnp.concatenate([w_ref[:, cols[hv]], qg_ref[:, cols[hv]]], axis=0), states[hv].astype(BF16),
                  preferred_element_type=F32) for hv in range(nh)]
    v_new = [(u_ref[:, cols[hv]].astype(F32) - ws[hv][:cs]).astype(BF16) for hv in range(nh)]
    for hv in range(nh):
        decay = jnp.exp(jnp.full((1, hd), glast_ref[ci, hv], F32))
        s_ref[hv] = states[hv] * decay + lax.dot_general(kd_ref[:, cols[hv]], v_new[hv], tn, preferred_element_type=F32)
    for pair in range(nh // 2):
        a2 = a_ref[:, pair * 2 * cs:(pair + 1) * 2 * cs]
        v2 = jnp.concatenate([v_new[2 * pair], v_new[2 * pair + 1]], axis=0)
        for s in range(2):
            hv = 2 * pair + s
            a_s = jnp.where((lane // cs) == s, a2, jnp.zeros_like(a2))
            o = ws[hv][cs:] + jnp.dot(a_s, v2, preferred_element_type=F32)
            on = o * lax.rsqrt(jnp.mean(o * o, axis=-1, keepdims=True) + NORM_EPS) * onw
            og_ref[:, cols[hv]] = (on * _silu(z_ref[:, cols[hv]].astype(F32))).astype(BF16)


def gdn_scan(glast, qg, kd, u, w, intra, proj, o_norm_w):
    t = qg.shape[0]
    cs = DN_CHUNK
    big_spec = pl.BlockSpec((cs, DN_V_DIM), lambda c, g: (c, 0))
    return pl.pallas_call(
        _gdn_scan_kernel,
        grid_spec=pltpu.PrefetchScalarGridSpec(
            num_scalar_prefetch=1,
            grid=(t // cs,),
            in_specs=[big_spec, big_spec, big_spec, big_spec,
                      pl.BlockSpec((cs, DN_V_HEADS * cs), lambda c, g: (c, 0)),
                      pl.BlockSpec((cs, DN_V_DIM), lambda c, g: (c, DN_CONV_CH // DN_V_DIM)),
                      pl.BlockSpec((1, DN_HEAD_DIM), lambda c, g: (0, 0))],
            out_specs=big_spec,
            scratch_shapes=[pltpu.VMEM((DN_V_HEADS, DN_HEAD_DIM, DN_HEAD_DIM), F32)],
        ),
        out_shape=jax.ShapeDtypeStruct((t, DN_V_DIM), BF16),
        compiler_params=_params("arbitrary"),
        name="gdn_scan",
    )(glast, qg, kd, u, w, intra, proj, o_norm_w.reshape(1, -1))


def gated_deltanet_block(x, h, gate, w_in, conv_w, a_log, dt_bias, o_norm_w, w_out):
    t = x.shape[0]
    n_main = DN_CONV_CH + DN_V_DIM
    w_ab = jnp.pad(w_in[:, n_main:], ((0, 0), (0, LANES - 2 * DN_V_HEADS)))
    proj = matmul(h, w_in, n_main, mode="plain", out_dtype=BF16, tm=1024, tn=1024)
    ab = matmul(h, w_ab, LANES, mode="plain", out_dtype=F32, tm=1024, tn=LANES)
    qg, kd, u, w, intra, gcum = gdn_prep(proj, ab, conv_w, a_log, dt_bias)
    glast = gcum.reshape(t // DN_CHUNK, DN_CHUNK, LANES)[:, DN_CHUNK - 1, :DN_V_HEADS]
    og = gdn_scan(glast, qg, kd, u, w, intra, proj, o_norm_w)
    zero_bias = jnp.zeros((1, D_MODEL), F32)
    return matmul(og, w_out, D_MODEL, mode="residual", out_dtype=F32, tm=1024, tn=512,
                  bias=zero_bias, res=x, gate=gate)


def _sgu_spatial_kernel(u_ref, v_ref, lnw_ref, lnb_ref, wsp_ref, bsp_ref, o_ref):
    cs = SGU_CHUNK
    gd = SGU_WIDTH // SGU_GROUPS
    v = v_ref[...].astype(F32)
    mu = jnp.mean(v, axis=-1, keepdims=True)
    var = jnp.mean(jnp.square(v - mu), axis=-1, keepdims=True)
    vn = ((v - mu) * lax.rsqrt(var + NORM_EPS) * lnw_ref[...] + lnb_ref[...]).astype(BF16)
    r = _iota((cs, cs), 0)
    c = _iota((cs, cs), 1)
    bsp = bsp_ref[...]
    for g in range(SGU_GROUPS):
        cols = slice(g * gd, (g + 1) * gd)
        wg = jnp.where(c <= r, wsp_ref[g], 0.0).astype(BF16)
        sp = jnp.dot(wg, vn[:, cols], preferred_element_type=F32) + bsp[:, g:g + 1]
        o_ref[:, cols] = (u_ref[:, cols].astype(F32) * sp).astype(BF16)


def sgu_spatial(zz, ln_w, ln_b, w_sp, b_sp):
    t = zz.shape[0]
    cs = SGU_CHUNK
    wd = SGU_WIDTH
    bsp_t = jnp.pad(b_sp.T, ((0, 0), (0, LANES - SGU_GROUPS)))
    row = pl.BlockSpec((1, wd), lambda i: (0, 0))
    return pl.pallas_call(
        _sgu_spatial_kernel,
        grid=(t // cs,),
        in_specs=[
            pl.BlockSpec((cs, wd), lambda i: (i, 0)),
            pl.BlockSpec((cs, wd), lambda i: (i, 1)),
            row, row,
            pl.BlockSpec((SGU_GROUPS, cs, cs), lambda i: (0, 0, 0)),
            pl.BlockSpec((cs, LANES), lambda i: (0, 0)),
        ],
        out_specs=pl.BlockSpec((cs, wd), lambda i: (i, 0)),
        out_shape=jax.ShapeDtypeStruct((t, wd), BF16),
        compiler_params=_params("parallel"),
        name="sgu_spatial",
    )(zz, zz, ln_w.reshape(1, -1), ln_b.reshape(1, -1), w_sp, bsp_t)


def chunked_gmlp_block(x, h, gate, w_in, b_in, ln_w, ln_b, w_sp, b_sp, w_out, b_out):
    zz = matmul(h, w_in, 2 * SGU_WIDTH, mode="bias_gelu", out_dtype=BF16, tm=1024, tn=1024, bias=b_in.reshape(1, -1))
    su = sgu_spatial(zz, ln_w, ln_b, w_sp, b_sp)
    return matmul(su, w_out, D_MODEL, mode="residual", out_dtype=F32, tm=1024, tn=512,
                  bias=b_out.reshape(1, -1), res=x, gate=gate)


def _dot_split3(a, b):
    a_hi = a.astype(BF16)
    b_hi = b.astype(BF16)
    a_lo = (a - a_hi.astype(F32)).astype(BF16)
    b_lo = (b - b_hi.astype(F32)).astype(BF16)
    return (jnp.dot(a_hi, b_hi, preferred_element_type=F32)
            + (jnp.dot(a_lo, b_hi, preferred_element_type=F32) + jnp.dot(a_hi, b_lo, preferred_element_type=F32)))


def _router_kernel(x_ref, w_ref, sc_ref, sh_ref, wr_ref, br_ref, h_ref, idx_ref, gate_ref, pos_ref, cnt_ref, carry):
    i = pl.program_id(0)
    tm = x_ref.shape[0]

    @pl.when(i == 0)
    def _():
        carry[...] = jnp.zeros_like(carry)

    h = _norm_mod(x_ref[...], w_ref[...], sc_ref[...], sh_ref[...])
    h_ref[...] = h
    lane = _iota((tm, LANES), 1).astype(F32)
    neg = jnp.float32(-jnp.inf)
    logits = _dot_split3(h, wr_ref[...]) + br_ref[...]
    logits = jnp.where(lane < N_EXPERTS, logits, neg)
    vals, idxs = [], []
    for _ in range(TOP_K):
        m = jnp.max(logits, axis=-1, keepdims=True)
        ix = jnp.min(jnp.where(logits == m, lane, float(LANES)), axis=-1, keepdims=True)
        vals.append(m)
        idxs.append(ix)
        logits = jnp.where(lane == ix, neg, logits)
    es = [jnp.exp(v - vals[0]) for v in vals]
    denom = es[0] + es[1] + es[2] + es[3]
    multi = jnp.zeros((tm, LANES), F32)
    for ix in idxs:
        multi = jnp.where(lane == ix, 1.0, multi)
    r = _iota((tm, tm), 0)
    c = _iota((tm, tm), 1)
    before = jnp.where(c < r, 1.0, 0.0).astype(BF16)
    rank = jnp.dot(before, multi.astype(BF16), preferred_element_type=F32) + carry[0:1, :]
    idx_t = jnp.zeros((tm, LANES), F32)
    gate_t = jnp.zeros((tm, LANES), F32)
    pos_t = jnp.zeros((tm, LANES), F32)
    for kk in range(TOP_K):
        pk = jnp.sum(jnp.where(lane == idxs[kk], rank, 0.0), axis=-1, keepdims=True)
        idx_t = jnp.where(lane == kk, idxs[kk], idx_t)
        gate_t = jnp.where(lane == kk, es[kk] / denom, gate_t)
        pos_t = jnp.where(lane == kk, pk, pos_t)
    idx_ref[...] = idx_t.astype(I32)
    gate_ref[...] = gate_t
    pos_ref[...] = pos_t.astype(I32)
    carry[...] = carry[...] + jnp.sum(multi, axis=0, keepdims=True)
    cnt_ref[...] = carry[...]


def moe_router(x, w, sc, sh, w_router, b_router):
    t, d = x.shape
    tm = 256
    wr = jnp.pad(w_router, ((0, 0), (0, LANES - N_EXPERTS)))
    br = jnp.pad(b_router.reshape(1, -1), ((0, 0), (0, LANES - N_EXPERTS)))
    vec = pl.BlockSpec((1, d), lambda i: (0, 0))
    tile = pl.BlockSpec((tm, LANES), lambda i: (i, 0))
    return pl.pallas_call(
        _router_kernel,
        grid=(t // tm,),
        in_specs=[pl.BlockSpec((tm, d), lambda i: (i, 0)), vec, vec, vec,
                  pl.BlockSpec((d, LANES), lambda i: (0, 0)), pl.BlockSpec((1, LANES), lambda i: (0, 0))],
        out_specs=[pl.BlockSpec((tm, d), lambda i: (i, 0)), tile, tile, tile, pl.BlockSpec((8, LANES), lambda i: (0, 0))],
        out_shape=[jax.ShapeDtypeStruct((t, d), F32), jax.ShapeDtypeStruct((t, LANES), I32),
                   jax.ShapeDtypeStruct((t, LANES), F32), jax.ShapeDtypeStruct((t, LANES), I32),
                   jax.ShapeDtypeStruct((8, LANES), F32)],
        scratch_shapes=[pltpu.VMEM((8, LANES), F32)],
        compiler_params=_params("arbitrary"),
        name="moe_router",
    )(x, w, sc, sh, wr, br)


def _dispatch_kernel(dest_ref, ztile_ref, h_ref, xs_ref, zbuf, sem, zsem):
    i = pl.program_id(0)
    tm = h_ref.shape[0]
    zt = zbuf.shape[0]

    @pl.when(i == 0)
    def _():
        zbuf[...] = jnp.zeros_like(zbuf)

        def zero_copy(e):
            row = pl.multiple_of(ztile_ref[e], zt)
            return pltpu.make_async_copy(zbuf, xs_ref.at[pl.ds(row, zt), :], zsem)

        for e in range(N_EXPERTS):
            @pl.when(ztile_ref[e] >= 0)
            def _():
                zero_copy(e).start()
        for e in range(N_EXPERTS):
            @pl.when(ztile_ref[e] >= 0)
            def _():
                zero_copy(e).wait()

        def zero_tail(j, carry):
            row = pl.multiple_of(j * zt, zt)
            cp = pltpu.make_async_copy(zbuf, xs_ref.at[pl.ds(row, zt), :], zsem)
            cp.start()
            cp.wait()
            return carry

        lax.fori_loop(ztile_ref[N_EXPERTS], xs_ref.shape[0] // zt, zero_tail, 0)

    def row_copy(r, d):
        return pltpu.make_async_copy(h_ref.at[pl.ds(r, 1), :], xs_ref.at[pl.ds(d, 1), :], sem)

    def issue(r, carry):
        base = (i * tm + r) * TOP_K
        for kk in range(TOP_K):
            row_copy(r, dest_ref[base + kk]).start(priority=kk % 2)
        return carry

    lax.fori_loop(0, tm, issue, 0, unroll=8)

    for kk in range(TOP_K):
        pltpu.make_async_copy(h_ref, xs_ref.at[pl.ds(0, tm), :], sem).wait()


def moe_dispatch(dest, zero_plan, h, n_rows):
    t, d = h.shape
    tm = 256
    return pl.pallas_call(
        _dispatch_kernel,
        grid_spec=pltpu.PrefetchScalarGridSpec(
            num_scalar_prefetch=2,
            grid=(t // tm,),
            in_specs=[pl.BlockSpec((tm, d), lambda i, dst, zt: (i, 0))],
            out_specs=pl.BlockSpec(memory_space=pl.ANY),
            scratch_shapes=[pltpu.VMEM((MOE_TILE, d), F32), pltpu.SemaphoreType.DMA(()), pltpu.SemaphoreType.DMA(())],
        ),
        out_shape=jax.ShapeDtypeStruct((n_rows, d), F32),
        compiler_params=_params("arbitrary"),
        name="moe_dispatch",
    )(dest.reshape(-1), zero_plan, h)


def _stream_expert_weights(w_hbm, wbuf, stage, sem, layer, e_cur, e_next, tig, gprev, cur, phase=0, phases=1):
    n_pairs = wbuf.shape[1] // 2
    cw = stage.shape[-1]
    prev_calls = gprev * phases
    q = tig * phases + phase

    def chunk_copy(e, c):
        return pltpu.make_async_copy(w_hbm.at[layer, e, :, pl.ds(c * cw, cw)], stage.at[c % 2], sem.at[c % 2])

    def start_pair(e, p):
        chunk_copy(e, 2 * p).start()
        chunk_copy(e, 2 * p + 1).start()

    def retire_pair(e, p, slot):
        for c in (2 * p, 2 * p + 1):
            chunk_copy(e, c).wait()
            wbuf[slot, c] = stage[c % 2].astype(BF16)

    if phase == 0:
        @pl.when(tig == 0)
        def _():
            for p in range(n_pairs):
                @pl.when(prev_calls <= p)
                def _():
                    start_pair(e_cur, p)

                @pl.when(prev_calls <= p + 1)
                def _():
                    retire_pair(e_cur, p, cur)

    @pl.when(e_next >= 0)
    def _():
        for j in range(phase, n_pairs + 1, phases):
            @pl.when(q == j)
            def _():
                if j >= 1:
                    retire_pair(e_next, j - 1, 1 - cur)
                if j < n_pairs:
                    start_pair(e_next, j)


def _ffn_up_kernel(te_ref, tig_ref, gprev_ref, nxt_ref, gidx_ref, nu_ref, x_ref, w_hbm, b_ref, o_ref,
                   wbuf, stage, sem, *, layer):
    i = pl.program_id(0)
    f = o_ref.shape[1]
    nc = wbuf.shape[1] // 2
    fc = wbuf.shape[-1]

    @pl.when(i < nu_ref[0])
    def _():
        cur = gidx_ref[i] % 2
        phases = 2
        xb = x_ref[...].astype(BF16)
        for c in range(nc):
            if c % (nc // phases) == 0:
                _stream_expert_weights(w_hbm, wbuf, stage, sem, layer, te_ref[i], nxt_ref[i], tig_ref[i],
                                       gprev_ref[i], cur, phase=c // (nc // phases), phases=phases)
            g = jnp.dot(xb, wbuf[cur, c], preferred_element_type=F32) + b_ref[:, c * fc:(c + 1) * fc]
            u = jnp.dot(xb, wbuf[cur, nc + c], preferred_element_type=F32) + b_ref[:, f + c * fc:f + (c + 1) * fc]
            x_glu = jnp.minimum(g, SWIGLU_LIMIT)
            x_lin = jnp.clip(u, -SWIGLU_LIMIT, SWIGLU_LIMIT)
            act = x_glu * _sigmoid(SWIGLU_ALPHA * x_glu) * (x_lin + 1.0)
            o_ref[:, c * fc:(c + 1) * fc] = act.astype(BF16)

    @pl.when(i >= nu_ref[0])
    def _():
        o_ref[...] = jnp.zeros_like(o_ref)


def _ffn_down_kernel(te_ref, tig_ref, gprev_ref, nxt_ref, gidx_ref, nu_ref, a_ref, w_hbm, b_ref, o_ref,
                     wbuf, stage, sem, *, layer):
    i = pl.program_id(0)
    fc = wbuf.shape[-1]

    @pl.when(i < nu_ref[0])
    def _():
        cur = gidx_ref[i] % 2
        _stream_expert_weights(w_hbm, wbuf, stage, sem, layer, te_ref[i], nxt_ref[i], tig_ref[i], gprev_ref[i], cur)
        a = a_ref[...]
        for c in range(wbuf.shape[1]):
            cols = slice(c * fc, (c + 1) * fc)
            o_ref[:, cols] = jnp.dot(a, wbuf[cur, c], preferred_element_type=F32) + b_ref[:, cols]

    @pl.when(i >= nu_ref[0])
    def _():
        o_ref[...] = jnp.zeros_like(o_ref)


def moe_experts(plan, xs, layer, w_gate_up, b_gate_up, w_down, b_down):
    n_rows, d = xs.shape
    tm = MOE_TILE
    n_tiles = n_rows // tm
    f = w_down.shape[2]
    cw = f // 4
    n_plan = len(plan)
    expert_row = lambda i, te, *_: (te[i], 0, 0)
    any_spec = pl.BlockSpec(memory_space=pl.ANY)
    act = pl.pallas_call(
        functools.partial(_ffn_up_kernel, layer=layer),
        grid_spec=pltpu.PrefetchScalarGridSpec(
            num_scalar_prefetch=n_plan,
            grid=(n_tiles,),
            in_specs=[pl.BlockSpec((tm, d), lambda i, *p: (jnp.minimum(i, p[-1][0] - 1), 0)),
                      any_spec,
                      pl.BlockSpec((None, 1, 2 * f), expert_row)],
            out_specs=pl.BlockSpec((tm, f), lambda i, *p: (i, 0)),
            scratch_shapes=[pltpu.VMEM((2, 2 * f // cw, d, cw), BF16), pltpu.VMEM((2, d, cw), F32),
                            pltpu.SemaphoreType.DMA((2,))],
        ),
        out_shape=jax.ShapeDtypeStruct((n_rows, f), BF16),
        compiler_params=_params("arbitrary"),
        name="moe_ffn_up",
    )(*plan, xs, w_gate_up, b_gate_up.reshape(N_EXPERTS, 1, 2 * f))
    return pl.pallas_call(
        functools.partial(_ffn_down_kernel, layer=layer),
        grid_spec=pltpu.PrefetchScalarGridSpec(
            num_scalar_prefetch=n_plan,
            grid=(n_tiles,),
            in_specs=[pl.BlockSpec((tm, f), lambda i, *p: (i, 0)),
                      any_spec,
                      pl.BlockSpec((None, 1, d), expert_row)],
            out_specs=pl.BlockSpec((tm, d), lambda i, *p: (i, 0)),
            scratch_shapes=[pltpu.VMEM((2, d // cw, f, cw), BF16), pltpu.VMEM((2, f, cw), F32),
                            pltpu.SemaphoreType.DMA((2,))],
        ),
        out_shape=jax.ShapeDtypeStruct((n_rows, d), F32),
        compiler_params=_params("arbitrary"),
        name="moe_ffn_down",
    )(*plan, act, w_down, b_down.reshape(N_EXPERTS, 1, d))


def _combine_kernel(dest_ref, x_ref, gate_ref, g2_ref, nw_ref, nsc_ref, nsh_ref, y_ref, *refs, emit_x):
    if emit_x:
        o_ref, hn_ref, ybuf, sem = refs
    else:
        hn_ref, ybuf, sem = refs
    i = pl.program_id(0)
    n = pl.num_programs(0)
    tm = x_ref.shape[0]

    def row_copy(tile, slot, r, kk):
        d = dest_ref[(tile * tm + r) * TOP_K + kk]
        return pltpu.make_async_copy(y_ref.at[pl.ds(d, 1), :], ybuf.at[slot, kk, pl.ds(r, 1), :], sem.at[slot])

    def issue_tile(tile, slot):
        def body(r, carry):
            for kk in range(TOP_K):
                row_copy(tile, slot, r, kk).start(priority=kk % 2)
            return carry
        lax.fori_loop(0, tm, body, 0, unroll=8)

    def drain_tile(slot):
        for kk in range(TOP_K):
            pltpu.make_async_copy(y_ref.at[pl.ds(0, tm), :], ybuf.at[slot, kk], sem.at[slot]).wait()

    slot = i % 2

    @pl.when(i == 0)
    def _():
        issue_tile(0, 0)

    @pl.when(i + 1 < n)
    def _():
        issue_tile(i + 1, 1 - slot)

    drain_tile(slot)
    gates = gate_ref[...]
    acc = ybuf[slot, 0] * gates[:, 0:1]
    for kk in range(1, TOP_K):
        acc = acc + ybuf[slot, kk] * gates[:, kk:kk + 1]
    x_new = x_ref[...] + g2_ref[...] * acc
    if emit_x:
        o_ref[...] = x_new
    hn_ref[...] = _norm_mod(x_new, nw_ref[...], nsc_ref[...], nsh_ref[...]).astype(hn_ref.dtype)


def moe_combine(dest, x, gates, gate2, y, next_norm, next_dtype, emit_x):
    t, d = x.shape
    tm = 128
    tile = pl.BlockSpec((tm, d), lambda i, dst: (i, 0))
    vec = pl.BlockSpec((1, d), lambda i, dst: (0, 0))
    normed = jax.ShapeDtypeStruct((t, d), next_dtype)
    return pl.pallas_call(
        functools.partial(_combine_kernel, emit_x=emit_x),
        grid_spec=pltpu.PrefetchScalarGridSpec(
            num_scalar_prefetch=1,
            grid=(t // tm,),
            in_specs=[tile, pl.BlockSpec((tm, LANES), lambda i, dst: (i, 0)), vec, vec, vec, vec,
                      pl.BlockSpec(memory_space=pl.ANY)],
            out_specs=[tile, tile] if emit_x else tile,
            scratch_shapes=[pltpu.VMEM((2, TOP_K, tm, d), F32), pltpu.SemaphoreType.DMA((2,))],
        ),
        out_shape=[jax.ShapeDtypeStruct((t, d), F32), normed] if emit_x else normed,
        compiler_params=_params("arbitrary"),
        name="moe_combine",
    )(dest.reshape(-1), x, gates, gate2, *next_norm, y)


def moe_block(x, norm_w, sc, sh, gate2, layer, w_router, b_router, w_gate_up, b_gate_up, w_down, b_down,
              next_norm, next_dtype, emit_x):
    t, d = x.shape
    tm = MOE_TILE
    h, idx, gates, pos, counts = moe_router(x, norm_w, sc, sh, w_router, b_router)
    cnt = counts[0, :N_EXPERTS].astype(I32)
    padded = (cnt + tm - 1) // tm * tm
    pend = jnp.cumsum(padded)
    pstart = pend - padded
    n_tiles = (t * TOP_K + N_EXPERTS * (tm - 1) + tm - 1) // tm
    is_expert = idx[:, :TOP_K, None] == jnp.arange(N_EXPERTS, dtype=I32)
    dest = pos[:, :TOP_K] + jnp.sum(jnp.where(is_expert, pstart, 0), axis=-1)
    tile_start = jnp.arange(n_tiles, dtype=I32) * tm
    tile_expert = jnp.minimum(jnp.sum((pend[None, :] <= tile_start[:, None]).astype(I32), axis=1), N_EXPERTS - 1)
    n_used = (pend[-1:] // tm).astype(I32)
    zero_plan = jnp.concatenate([jnp.where(padded > 0, pend - tm, -1), n_used]).astype(I32)
    xs = moe_dispatch(dest, zero_plan, h, n_tiles * tm)
    experts = jnp.arange(N_EXPERTS, dtype=I32)
    group_tiles = padded // tm
    nonempty = group_tiles > 0
    later = (experts[None, :] > experts[:, None]) & nonempty[None, :]
    earlier = (experts[None, :] < experts[:, None]) & nonempty[None, :]
    next_e = jnp.min(jnp.where(later, experts[None, :], N_EXPERTS), axis=1)
    prev_e = jnp.max(jnp.where(earlier, experts[None, :], -1), axis=1)
    next_e = jnp.where(next_e < N_EXPERTS, next_e, -1)
    prev_tiles = jnp.where(prev_e >= 0, group_tiles[jnp.maximum(prev_e, 0)], 0)
    ordinal = jnp.cumsum(nonempty.astype(I32)) - nonempty.astype(I32)
    tile_in_group = jnp.arange(n_tiles, dtype=I32) - (pstart // tm)[tile_expert]
    plan = tuple(a.astype(I32) for a in (tile_expert, tile_in_group, prev_tiles[tile_expert], next_e[tile_expert],
                                         ordinal[tile_expert], n_used))
    y = moe_experts(plan, xs, layer, w_gate_up, b_gate_up, w_down, b_down)
    return moe_combine(dest, x, gates, gate2, y, next_norm, next_dtype, emit_x)


def kernel(x, c, ada_w, ada_b, norm_w, dn_w_in, dn_conv_w, dn_a_log, dn_dt_bias, dn_o_norm_w, dn_w_out, sgu_w_in, sgu_b_in, sgu_ln_w, sgu_ln_b, sgu_w_sp, sgu_b_sp, sgu_w_out, sgu_b_out, moe_w_router, moe_b_router, moe_w_gate_up, moe_b_gate_up, moe_w_down, moe_b_down, final_norm_w):
    bsz, seq, d = x.shape
    assert bsz == 1 and d == D_MODEL
    xt = x.reshape(seq, d)
    mod = ada_mod(c, ada_w, ada_b)
    mods = [[mod[i:i + 1, s * d:(s + 1) * d] for s in range(6)] for i in range(DEPTH)]
    zero = jnp.zeros((1, d), F32)
    h = norm_mod(xt, norm_w[0, 0:1], mods[0][1], mods[0][0], BF16)
    for i in range(DEPTH):
        sh1, sc1, gt1, sh2, sc2, gt2 = mods[i]
        last = i == DEPTH - 1
        next_norm = (final_norm_w.reshape(1, d), zero, zero) if last else (norm_w[i + 1, 0:1], mods[i + 1][1], mods[i + 1][0])
        j = i // 2
        if i % 2 == 0:
            xt = gated_deltanet_block(xt, h, gt1, dn_w_in[j], dn_conv_w[j], dn_a_log[j], dn_dt_bias[j],
                                      dn_o_norm_w[j], dn_w_out[j])
        else:
            xt = chunked_gmlp_block(xt, h, gt1, sgu_w_in[j], sgu_b_in[j], sgu_ln_w[j], sgu_ln_b[j], sgu_w_sp[j],
                                    sgu_b_sp[j], sgu_w_out[j], sgu_b_out[j])
        res = moe_block(xt, norm_w[i, 1:2], sc2, sh2, gt2, i, moe_w_router[i], moe_b_router[i],
                        moe_w_gate_up, moe_b_gate_up[i], moe_w_down, moe_b_down[i],
                        next_norm, F32 if last else BF16, not last)
        if last:
            return res.reshape(bsz, seq, d)
        xt, h = res
```

```python
import functools

import jax
import jax.numpy as jnp
from jax import lax
from jax.experimental import pallas as pl
from jax.experimental.pallas import tpu as pltpu

F32 = jnp.float32
BF16 = jnp.bfloat16
I32 = jnp.int32
HIGHEST = lax.Precision.HIGHEST

D_MODEL = 2048
DEPTH = 2
NORM_EPS = 1e-6
DN_HEAD_DIM = 128
DN_QK_HEADS = 16
DN_V_HEADS = 32
DN_QK_DIM = 2048
DN_V_DIM = 4096
DN_CONV_CH = 8192
DN_CONV = 4
DN_CHUNK = 64
SGU_WIDTH = 4096
SGU_CHUNK = 128
SGU_GROUPS = 32
N_EXPERTS = 32
TOP_K = 4
EXPERT_DIM = 2048
SWIGLU_LIMIT = 7.0
SWIGLU_ALPHA = 1.702

LANES = 128
VMEM_LIMIT = 58 * 1024 * 1024
MOE_TILE = 512
GDN_TILE = 2048


def _params(*sem):
    return pltpu.CompilerParams(dimension_semantics=sem, vmem_limit_bytes=VMEM_LIMIT)


def _iota(shape, dim):
    return lax.broadcasted_iota(I32, shape, dim)


def _sigmoid(x):
    return 1.0 / (1.0 + jnp.exp(-x))


def _silu(x):
    return x * _sigmoid(x)


def _ada_kernel(c_ref, w_ref, b_ref, o_ref):
    c = c_ref[...]
    o_ref[...] = jnp.dot(_silu(c), w_ref[...], precision=HIGHEST, preferred_element_type=F32) + b_ref[...]


def ada_mod(c, ada_w, ada_b):
    depth, d, n = ada_w.shape
    tn = 1024
    c8 = jnp.broadcast_to(c, (8, d))
    out = pl.pallas_call(
        _ada_kernel,
        grid=(depth, n // tn),
        in_specs=[
            pl.BlockSpec((8, d), lambda l, j: (0, 0)),
            pl.BlockSpec((None, d, tn), lambda l, j: (l, 0, j)),
            pl.BlockSpec((None, 1, tn), lambda l, j: (l, 0, j)),
        ],
        out_specs=pl.BlockSpec((None, 8, tn), lambda l, j: (l, 0, j)),
        out_shape=jax.ShapeDtypeStruct((depth, 8, n), F32),
        compiler_params=_params("parallel", "parallel"),
        name="ada_mod",
    )(c8, ada_w, ada_b.reshape(depth, 1, n))
    return out[:, 0, :]


def _norm_mod(x, w, sc, sh):
    y = x * lax.rsqrt(jnp.mean(x * x, axis=-1, keepdims=True) + NORM_EPS)
    return (y * w) * (1.0 + sc) + sh


def _norm_mod_kernel(x_ref, w_ref, sc_ref, sh_ref, o_ref):
    o_ref[...] = _norm_mod(x_ref[...], w_ref[...], sc_ref[...], sh_ref[...]).astype(o_ref.dtype)


def norm_mod(x, w, sc, sh, out_dtype):
    t, d = x.shape
    tm = 512
    vec = pl.BlockSpec((1, d), lambda i: (0, 0))
    return pl.pallas_call(
        _norm_mod_kernel,
        grid=(t // tm,),
        in_specs=[pl.BlockSpec((tm, d), lambda i: (i, 0)), vec, vec, vec],
        out_specs=pl.BlockSpec((tm, d), lambda i: (i, 0)),
        out_shape=jax.ShapeDtypeStruct((t, d), out_dtype),
        compiler_params=_params("parallel"),
        name="norm_mod",
    )(x, w, sc, sh)


def _gelu_exact(x):
    return 0.5 * x * (1.0 + lax.erf(x * (2.0 ** -0.5)))


def _mm_kernel(a_ref, w_ref, *refs, mode):
    *refs, wb = refs
    @pl.when(pl.program_id(1) == 0)
    def _():
        wb[...] = w_ref[...].astype(BF16)

    acc = jnp.dot(a_ref[...], wb[...], preferred_element_type=F32)
    if mode == "plain":
        (o_ref,) = refs
        o_ref[...] = acc.astype(o_ref.dtype)
    elif mode == "bias_gelu":
        b_ref, o_ref = refs
        o_ref[...] = _gelu_exact(acc + b_ref[...]).astype(o_ref.dtype)
    else:
        b_ref, res_ref, gate_ref, o_ref = refs
        o_ref[...] = res_ref[...] + gate_ref[...] * (acc + b_ref[...])


def matmul(a, w, n, *, mode, out_dtype, tm, tn, bias=None, res=None, gate=None):
    m, k = a.shape
    assert m % tm == 0 and n % tn == 0 and w.shape[1] >= n
    row = pl.BlockSpec((1, tn), lambda j, i: (0, j))
    in_specs = [pl.BlockSpec((tm, k), lambda j, i: (i, 0)), pl.BlockSpec((k, tn), lambda j, i: (0, j))]
    args = [a, w]
    if mode != "plain":
        in_specs.append(row)
        args.append(bias)
    if mode == "residual":
        in_specs += [pl.BlockSpec((tm, tn), lambda j, i: (i, j)), row]
        args += [res, gate]
    return pl.pallas_call(
        functools.partial(_mm_kernel, mode=mode),
        grid=(n // tn, m // tm),
        in_specs=in_specs,
        out_specs=pl.BlockSpec((tm, tn), lambda j, i: (i, j)),
        out_shape=jax.ShapeDtypeStruct((m, n), out_dtype),
        scratch_shapes=[pltpu.VMEM((k, tn), BF16)],
        compiler_params=_params("parallel", "arbitrary"),
        name="matmul_" + mode,
    )(*args)


def _softplus(x):
    return jnp.maximum(x, 0.0) + jnp.log1p(jnp.exp(-jnp.abs(x)))


def _unit_lower_inverses(lows, n_block):
    n = lows[0].shape[0]
    r = _iota((n, n), 0)
    c = _iota((n, n), 1)
    eye = jnp.where(r == c, 1.0, 0.0).astype(F32)

    def mm(a, b):
        return jnp.dot(a, b, preferred_element_type=F32)

    def bf(ms):
        return [m.astype(BF16) for m in ms]

    base = (r // 8) == (c // 8)
    lows_b = bf(lows)
    ds = [jnp.where(base, low, 0.0) for low in lows]
    ds_b = [jnp.where(base, low, jnp.zeros_like(low)) for low in lows_b]
    xs = [eye - d for d in ds]
    ps_b = bf([mm(d, d) for d in ds_b])
    xs = [x + mm(xb, p) for x, xb, p in zip(xs, bf(xs), ps_b)]
    ps_b = bf([mm(p, p) for p in ps_b])
    xs = [x + mm(xb, p) for x, xb, p in zip(xs, bf(xs), ps_b)]
    s = 8
    while s < n_block:
        pair = ((r // (2 * s)) == (c // (2 * s))) & ((r // s) != (c // s))
        xs_b = bf(xs)
        ts_b = bf([mm(jnp.where(pair, low, jnp.zeros_like(low)), xb) for low, xb in zip(lows_b, xs_b)])
        xs = [x - mm(xb, t) for x, xb, t in zip(xs, xs_b, ts_b)]
        s *= 2
    return xs


def _gdn_prep_kernel(q_ref, k_ref, v_ref, qh_ref, kh_ref, vh_ref, wq_ref, wk_ref, wv_ref, ab_ref, alog_ref, dtb_ref,
                     qg_ref, kd_ref, u_ref, w_ref, a_ref, gcum_ref,
                     win_q, win_k, win_v, gcum_s, gtot_s, beta_s):
    i = pl.program_id(0)
    h = pl.program_id(1)
    tm = q_ref.shape[0]
    halo = qh_ref.shape[0]
    cs = DN_CHUNK
    hd = DN_HEAD_DIM

    def conv_silu(x_ref, halo_ref, w_ref, win_ref):
        hal = halo_ref[...].astype(F32)
        win_ref[0:halo, :] = jnp.where(i == 0, jnp.zeros_like(hal), hal)
        win_ref[halo:halo + tm, :] = x_ref[...].astype(F32)
        w = w_ref[...]
        acc = None
        for j in range(DN_CONV):
            start = halo - (DN_CONV - 1) + j
            term = win_ref[start:start + tm, :] * w[j:j + 1, :]
            acc = term if acc is None else acc + term
        return _silu(acc)

    def l2norm(x):
        return x * lax.rsqrt(jnp.sum(x * x, axis=-1, keepdims=True) + NORM_EPS)

    @pl.when(h == 0)
    def _():
        ab = ab_ref[...]
        g = -jnp.exp(alog_ref[...]) * _softplus(ab + dtb_ref[...])
        sub = 4 * DN_CHUNK
        r = _iota((sub, sub), 0)
        c = _iota((sub, sub), 1)
        same = (r // DN_CHUNK) == (c // DN_CHUNK)
        tri = jnp.where(same & (c <= r), 1.0, 0.0).astype(F32)
        blk = jnp.where(same, 1.0, 0.0).astype(F32)
        for s0 in range(0, tm, sub):
            g_sub = g[s0:s0 + sub]
            gcum = jnp.dot(tri, g_sub, precision=HIGHEST, preferred_element_type=F32)
            gcum_s[s0:s0 + sub, :] = gcum
            gcum_ref[s0:s0 + sub, :] = gcum
            gtot_s[s0:s0 + sub, :] = jnp.dot(blk, g_sub, precision=HIGHEST, preferred_element_type=F32)
        beta_s[...] = _sigmoid(ab)

    q = l2norm(conv_silu(q_ref, qh_ref, wq_ref, win_q)) * (DN_HEAD_DIM ** -0.5)
    k = l2norm(conv_silu(k_ref, kh_ref, wk_ref, win_k))
    v = conv_silu(v_ref, vh_ref, wv_ref, win_v)

    def head_gates(ref, first_lane):
        rolled = pltpu.roll(ref[...], (LANES - first_lane) % LANES, 1)
        return [jnp.broadcast_to(rolled[:, s:s + 1], (tm, hd)) for s in range(2)]

    gc = head_gates(gcum_s, 2 * h)
    gt = head_gates(gtot_s, 2 * h)
    be = head_gates(beta_s, DN_V_HEADS + 2 * h)
    vb, kbg = [], []
    for s in range(2):
        cols = slice(s * hd, (s + 1) * hd)
        eg = jnp.exp(gc[s])
        qg_ref[:, cols] = (q * eg).astype(BF16)
        kd_ref[:, cols] = (k * jnp.exp(gt[s] - gc[s])).astype(BF16)
        vb.append((v[:, cols] * be[s]).astype(BF16))
        kbg.append((k * be[s] * eg).astype(BF16))
    qb = q.astype(BF16)
    kb = k.astype(BF16)

    n2 = 2 * cs
    r = _iota((n2, n2), 0)
    c = _iota((n2, n2), 1)
    same_head = (r // cs) == (c // cs)
    causal = same_head & (c <= r)
    strict = same_head & (c < r)
    top = _iota((cs, n2), 1) < cs
    nt = (((1,), (1,)), ((), ()))
    chunks = [slice(ci * cs, (ci + 1) * cs) for ci in range(tm // cs)]

    def stack_heads(x, rows):
        return jnp.concatenate([x[0][rows], x[1][rows]], axis=0)

    k_st = [jnp.concatenate([kb[rows], kb[rows]], axis=0) for rows in chunks]
    q_st = [jnp.concatenate([qb[rows], qb[rows]], axis=0) for rows in chunks]
    kk = [lax.dot_general(ks, ks, nt, preferred_element_type=F32) for ks in k_st]
    qk = [lax.dot_general(qs, ks, nt, preferred_element_type=F32) for qs, ks in zip(q_st, k_st)]
    g_col = [stack_heads(gc, rows) for rows in chunks]
    b_col = [stack_heads(be, rows) for rows in chunks]
    decay = [jnp.exp(jnp.where(causal, g - g.T, 0.0)) for g in g_col]
    lows = [jnp.where(strict, b * kk_i * d, 0.0) for b, kk_i, d in zip(b_col, kk, decay)]
    tinvs = [t.astype(BF16) for t in _unit_lower_inverses(lows, cs)]
    us = [jnp.dot(t, stack_heads(vb, rows), preferred_element_type=F32).astype(BF16) for t, rows in zip(tinvs, chunks)]
    ws = [jnp.dot(t, stack_heads(kbg, rows), preferred_element_type=F32).astype(BF16) for t, rows in zip(tinvs, chunks)]
    for rows, u2, w2, qk_i, d in zip(chunks, us, ws, qk, decay):
        intra = jnp.where(causal, qk_i * d, 0.0)
        a_ref[rows, :] = jnp.where(top, intra[:cs], intra[cs:]).astype(BF16)
        u_ref[rows, :hd] = u2[:cs]
        u_ref[rows, hd:] = u2[cs:]
        w_ref[rows, :hd] = w2[:cs]
        w_ref[rows, hd:] = w2[cs:]


def gdn_prep(proj, ab, conv_w, a_log, dt_bias):
    t = proj.shape[0]
    tm = GDN_TILE
    halo = 16
    hb = tm // halo
    hd = DN_HEAD_DIM
    nq = DN_QK_HEADS

    def halo_map(off):
        return lambda i, h: (jnp.maximum(i * hb - 1, 0), off + h)

    pad = jnp.zeros((1, LANES - DN_V_HEADS), F32)
    alog = jnp.concatenate([a_log.reshape(1, -1), pad], axis=1)
    dtb = jnp.concatenate([dt_bias.reshape(1, -1), pad], axis=1)
    big = jax.ShapeDtypeStruct((t, DN_V_DIM), BF16)
    small = jax.ShapeDtypeStruct((t, DN_QK_DIM), BF16)
    gshape = jax.ShapeDtypeStruct((t, LANES), F32)
    big_spec = pl.BlockSpec((tm, 2 * hd), lambda i, h: (i, h))
    small_spec = pl.BlockSpec((tm, hd), lambda i, h: (i, h))
    g_spec = pl.BlockSpec((tm, LANES), lambda i, h: (i, 0))
    row = pl.BlockSpec((1, LANES), lambda i, h: (0, 0))
    return pl.pallas_call(
        _gdn_prep_kernel,
        grid=(t // tm, nq),
        in_specs=[
            pl.BlockSpec((tm, hd), lambda i, h: (i, h)),
            pl.BlockSpec((tm, hd), lambda i, h: (i, nq + h)),
            pl.BlockSpec((tm, 2 * hd), lambda i, h: (i, nq + h)),
            pl.BlockSpec((halo, hd), halo_map(0)),
            pl.BlockSpec((halo, hd), halo_map(nq)),
            pl.BlockSpec((halo, 2 * hd), lambda i, h: (jnp.maximum(i * hb - 1, 0), nq + h)),
            pl.BlockSpec((DN_CONV, hd), lambda i, h: (0, h)),
            pl.BlockSpec((DN_CONV, hd), lambda i, h: (0, nq + h)),
            pl.BlockSpec((DN_CONV, 2 * hd), lambda i, h: (0, nq + h)),
            g_spec, row, row,
        ],
        out_specs=[big_spec, big_spec, big_spec, big_spec, small_spec, g_spec],
        out_shape=[big, big, big, big, small, gshape],
        scratch_shapes=[
            pltpu.VMEM((tm + halo, hd), F32), pltpu.VMEM((tm + halo, hd), F32), pltpu.VMEM((tm + halo, 2 * hd), F32),
            pltpu.VMEM((tm, LANES), F32), pltpu.VMEM((tm, LANES), F32), pltpu.VMEM((tm, LANES), F32),
        ],
        compiler_params=_params("parallel", "arbitrary"),
        name="gdn_prep",
    )(proj, proj, proj, proj, proj, proj, conv_w, conv_w, conv_w, ab, alog, dtb)


def _gdn_scan_kernel(glast_ref, qg_ref, kd_ref, u_ref, w_ref, a_ref, z_ref, onw_ref, og_ref, s_ref):
    ci = pl.program_id(0)
    cs = DN_CHUNK
    hd = DN_HEAD_DIM
    nh = DN_V_HEADS

    @pl.when(ci == 0)
    def _():
        s_ref[...] = jnp.zeros_like(s_ref)

    onw = onw_ref[...]
    tn = (((0,), (0,)), ((), ()))
    lane = _iota((cs, 2 * cs), 1)
    cols = [slice(hv * hd, (hv + 1) * hd) for hv in range(nh)]
    states = [s_ref[hv] for hv in range(nh)]
    ws = [jnp.dot(jnp.concatenate([w_ref[:, cols[hv]], qg_ref[:, cols[hv]]], axis=0), states[hv].astype(BF16),
                  preferred_element_type=F32) for hv in range(nh)]
    v_new = [(u_ref[:, cols[hv]].astype(F32) - ws[hv][:cs]).astype(BF16) for hv in range(nh)]
    for hv in range(nh):
        decay = jnp.exp(jnp.full((1, hd), glast_ref[ci, hv], F32))
        s_ref[hv] = states[hv] * decay + lax.dot_general(kd_ref[:, cols[hv]], v_new[hv], tn, preferred_element_type=F32)
    for pair in range(nh // 2):
        a2 = a_ref[:, pair * 2 * cs:(pair + 1) * 2 * cs]
        v2 = jnp.concatenate([v_new[2 * pair], v_new[2 * pair + 1]], axis=0)
        for s in range(2):
            hv = 2 * pair + s
            a_s = jnp.where((lane // cs) == s, a2, jnp.zeros_like(a2))
            o = ws[hv][cs:] + jnp.dot(a_s, v2, preferred_element_type=F32)
            on = o * lax.rsqrt(jnp.mean(o * o, axis=-1, keepdims=True) + NORM_EPS) * onw
            og_ref[:, cols[hv]] = (on * _silu(z_ref[:, cols[hv]].astype(F32))).astype(BF16)


def gdn_scan(glast, qg, kd, u, w, intra, proj, o_norm_w):
    t = qg.shape[0]
    cs = DN_CHUNK
    big_spec = pl.BlockSpec((cs, DN_V_DIM), lambda c, g: (c, 0))
    return pl.pallas_call(
        _gdn_scan_kernel,
        grid_spec=pltpu.PrefetchScalarGridSpec(
            num_scalar_prefetch=1,
            grid=(t // cs,),
            in_specs=[big_spec, big_spec, big_spec, big_spec,
                      pl.BlockSpec((cs, DN_V_HEADS * cs), lambda c, g: (c, 0)),
                      pl.BlockSpec((cs, DN_V_DIM), lambda c, g: (c, DN_CONV_CH // DN_V_DIM)),
                      pl.BlockSpec((1, DN_HEAD_DIM), lambda c, g: (0, 0))],
            out_specs=big_spec,
            scratch_shapes=[pltpu.VMEM((DN_V_HEADS, DN_HEAD_DIM, DN_HEAD_DIM), F32)],
        ),
        out_shape=jax.ShapeDtypeStruct((t, DN_V_DIM), BF16),
        compiler_params=_params("arbitrary"),
        name="gdn_scan",
    )(glast, qg, kd, u, w, intra, proj, o_norm_w.reshape(1, -1))


def gated_deltanet_block(x, h, gate, w_in, conv_w, a_log, dt_bias, o_norm_w, w_out):
    t = x.shape[0]
    n_main = DN_CONV_CH + DN_V_DIM
    w_ab = jnp.pad(w_in[:, n_main:], ((0, 0), (0, LANES - 2 * DN_V_HEADS)))
    proj = matmul(h, w_in, n_main, mode="plain", out_dtype=BF16, tm=1024, tn=1024)
    ab = matmul(h, w_ab, LANES, mode="plain", out_dtype=F32, tm=1024, tn=LANES)
    qg, kd, u, w, intra, gcum = gdn_prep(proj, ab, conv_w, a_log, dt_bias)
    glast = gcum.reshape(t // DN_CHUNK, DN_CHUNK, LANES)[:, DN_CHUNK - 1, :DN_V_HEADS]
    og = gdn_scan(glast, qg, kd, u, w, intra, proj, o_norm_w)
    zero_bias = jnp.zeros((1, D_MODEL), F32)
    return matmul(og, w_out, D_MODEL, mode="residual", out_dtype=F32, tm=1024, tn=512,
                  bias=zero_bias, res=x, gate=gate)


def _sgu_spatial_kernel(u_ref, v_ref, lnw_ref, lnb_ref, wsp_ref, bsp_ref, o_ref):
    cs = SGU_CHUNK
    gd = SGU_WIDTH // SGU_GROUPS
    v = v_ref[...].astype(F32)
    mu = jnp.mean(v, axis=-1, keepdims=True)
    var = jnp.mean(jnp.square(v - mu), axis=-1, keepdims=True)
    vn = ((v - mu) * lax.rsqrt(var + NORM_EPS) * lnw_ref[...] + lnb_ref[...]).astype(BF16)
    r = _iota((cs, cs), 0)
    c = _iota((cs, cs), 1)
    bsp = bsp_ref[...]
    for g in range(SGU_GROUPS):
        cols = slice(g * gd, (g + 1) * gd)
        wg = jnp.where(c <= r, wsp_ref[g], 0.0).astype(BF16)
        sp = jnp.dot(wg, vn[:, cols], preferred_element_type=F32) + bsp[:, g:g + 1]
        o_ref[:, cols] = (u_ref[:, cols].astype(F32) * sp).astype(BF16)


def sgu_spatial(zz, ln_w, ln_b, w_sp, b_sp):
    t = zz.shape[0]
    cs = SGU_CHUNK
    wd = SGU_WIDTH
    bsp_t = jnp.pad(b_sp.T, ((0, 0), (0, LANES - SGU_GROUPS)))
    row = pl.BlockSpec((1, wd), lambda i: (0, 0))
    return pl.pallas_call(
        _sgu_spatial_kernel,
        grid=(t // cs,),
        in_specs=[
            pl.BlockSpec((cs, wd), lambda i: (i, 0)),
            pl.BlockSpec((cs, wd), lambda i: (i, 1)),
            row, row,
            pl.BlockSpec((SGU_GROUPS, cs, cs), lambda i: (0, 0, 0)),
            pl.BlockSpec((cs, LANES), lambda i: (0, 0)),
        ],
        out_specs=pl.BlockSpec((cs, wd), lambda i: (i, 0)),
        out_shape=jax.ShapeDtypeStruct((t, wd), BF16),
        compiler_params=_params("parallel"),
        name="sgu_spatial",
    )(zz, zz, ln_w.reshape(1, -1), ln_b.reshape(1, -1), w_sp, bsp_t)


def chunked_gmlp_block(x, h, gate, w_in, b_in, ln_w, ln_b, w_sp, b_sp, w_out, b_out):
    zz = matmul(h, w_in, 2 * SGU_WIDTH, mode="bias_gelu", out_dtype=BF16, tm=1024, tn=1024, bias=b_in.reshape(1, -1))
    su = sgu_spatial(zz, ln_w, ln_b, w_sp, b_sp)
    return matmul(su, w_out, D_MODEL, mode="residual", out_dtype=F32, tm=1024, tn=512,
                  bias=b_out.reshape(1, -1), res=x, gate=gate)


def _dot_split3(a, b):
    a_hi = a.astype(BF16)
    b_hi = b.astype(BF16)
    a_lo = (a - a_hi.astype(F32)).astype(BF16)
    b_lo = (b - b_hi.astype(F32)).astype(BF16)
    return (jnp.dot(a_hi, b_hi, preferred_element_type=F32)
            + (jnp.dot(a_lo, b_hi, preferred_element_type=F32) + jnp.dot(a_hi, b_lo, preferred_element_type=F32)))


def _router_kernel(x_ref, w_ref, sc_ref, sh_ref, wr_ref, br_ref, h_ref, idx_ref, gate_ref, pos_ref, cnt_ref, carry):
    i = pl.program_id(0)
    tm = x_ref.shape[0]

    @pl.when(i == 0)
    def _():
        carry[...] = jnp.zeros_like(carry)

    h = _norm_mod(x_ref[...], w_ref[...], sc_ref[...], sh_ref[...])
    h_ref[...] = h
    lane = _iota((tm, LANES), 1).astype(F32)
    neg = jnp.float32(-jnp.inf)
    logits = _dot_split3(h, wr_ref[...]) + br_ref[...]
    logits = jnp.where(lane < N_EXPERTS, logits, neg)
    vals, idxs = [], []
    for _ in range(TOP_K):
        m = jnp.max(logits, axis=-1, keepdims=True)
        ix = jnp.min(jnp.where(logits == m, lane, float(LANES)), axis=-1, keepdims=True)
        vals.append(m)
        idxs.append(ix)
        logits = jnp.where(lane == ix, neg, logits)
    es = [jnp.exp(v - vals[0]) for v in vals]
    denom = es[0] + es[1] + es[2] + es[3]
    multi = jnp.zeros((tm, LANES), F32)
    for ix in idxs:
        multi = jnp.where(lane == ix, 1.0, multi)
    r = _iota((tm, tm), 0)
    c = _iota((tm, tm), 1)
    before = jnp.where(c < r, 1.0, 0.0).astype(BF16)
    rank = jnp.dot(before, multi.astype(BF16), preferred_element_type=F32) + carry[0:1, :]
    idx_t = jnp.zeros((tm, LANES), F32)
    gate_t = jnp.zeros((tm, LANES), F32)
    pos_t = jnp.zeros((tm, LANES), F32)
    for kk in range(TOP_K):
        pk = jnp.sum(jnp.where(lane == idxs[kk], rank, 0.0), axis=-1, keepdims=True)
        idx_t = jnp.where(lane == kk, idxs[kk], idx_t)
        gate_t = jnp.where(lane == kk, es[kk] / denom, gate_t)
        pos_t = jnp.where(lane == kk, pk, pos_t)
    idx_ref[...] = idx_t.astype(I32)
    gate_ref[...] = gate_t
    pos_ref[...] = pos_t.astype(I32)
    carry[...] = carry[...] + jnp.sum(multi, axis=0, keepdims=True)
    cnt_ref[...] = carry[...]


def moe_router(x, w, sc, sh, w_router, b_router):
    t, d = x.shape
    tm = 256
    wr = jnp.pad(w_router, ((0, 0), (0, LANES - N_EXPERTS)))
    br = jnp.pad(b_router.reshape(1, -1), ((0, 0), (0, LANES - N_EXPERTS)))
    vec = pl.BlockSpec((1, d), lambda i: (0, 0))
    tile = pl.BlockSpec((tm, LANES), lambda i: (i, 0))
    return pl.pallas_call(
        _router_kernel,
        grid=(t // tm,),
        in_specs=[pl.BlockSpec((tm, d), lambda i: (i, 0)), vec, vec, vec,
                  pl.BlockSpec((d, LANES), lambda i: (0, 0)), pl.BlockSpec((1, LANES), lambda i: (0, 0))],
        out_specs=[pl.BlockSpec((tm, d), lambda i: (i, 0)), tile, tile, tile, pl.BlockSpec((8, LANES), lambda i: (0, 0))],
        out_shape=[jax.ShapeDtypeStruct((t, d), F32), jax.ShapeDtypeStruct((t, LANES), I32),
                   jax.ShapeDtypeStruct((t, LANES), F32), jax.ShapeDtypeStruct((t, LANES), I32),
                   jax.ShapeDtypeStruct((8, LANES), F32)],
        scratch_shapes=[pltpu.VMEM((8, LANES), F32)],
        compiler_params=_params("arbitrary"),
        name="moe_router",
    )(x, w, sc, sh, wr, br)


def _dispatch_kernel(dest_ref, ztile_ref, h_ref, xs_ref, zbuf, sem, zsem):
    i = pl.program_id(0)
    tm = h_ref.shape[0]
    zt = zbuf.shape[0]

    @pl.when(i == 0)
    def _():
        zbuf[...] = jnp.zeros_like(zbuf)

        def zero_copy(e):
            row = pl.multiple_of(ztile_ref[e], zt)
            return pltpu.make_async_copy(zbuf, xs_ref.at[pl.ds(row, zt), :], zsem)

        for e in range(N_EXPERTS):
            @pl.when(ztile_ref[e] >= 0)
            def _():
                zero_copy(e).start()
        for e in range(N_EXPERTS):
            @pl.when(ztile_ref[e] >= 0)
            def _():
                zero_copy(e).wait()

        def zero_tail(j, carry):
            row = pl.multiple_of(j * zt, zt)
            cp = pltpu.make_async_copy(zbuf, xs_ref.at[pl.ds(row, zt), :], zsem)
            cp.start()
            cp.wait()
            return carry

        lax.fori_loop(ztile_ref[N_EXPERTS], xs_ref.shape[0] // zt, zero_tail, 0)

    def row_copy(r, d):
        return pltpu.make_async_copy(h_ref.at[pl.ds(r, 1), :], xs_ref.at[pl.ds(d, 1), :], sem)

    def issue(r, carry):
        base = (i * tm + r) * TOP_K
        for kk in range(TOP_K):
            row_copy(r, dest_ref[base + kk]).start(priority=kk % 2)
        return carry

    lax.fori_loop(0, tm, issue, 0, unroll=8)

    for kk in range(TOP_K):
        pltpu.make_async_copy(h_ref, xs_ref.at[pl.ds(0, tm), :], sem).wait()


def moe_dispatch(dest, zero_plan, h, n_rows):
    t, d = h.shape
    tm = 256
    return pl.pallas_call(
        _dispatch_kernel,
        grid_spec=pltpu.PrefetchScalarGridSpec(
            num_scalar_prefetch=2,
            grid=(t // tm,),
            in_specs=[pl.BlockSpec((tm, d), lambda i, dst, zt: (i, 0))],
            out_specs=pl.BlockSpec(memory_space=pl.ANY),
            scratch_shapes=[pltpu.VMEM((MOE_TILE, d), F32), pltpu.SemaphoreType.DMA(()), pltpu.SemaphoreType.DMA(())],
        ),
        out_shape=jax.ShapeDtypeStruct((n_rows, d), F32),
        compiler_params=_params("arbitrary"),
        name="moe_dispatch",
    )(dest.reshape(-1), zero_plan, h)


def _stream_expert_weights(w_hbm, wbuf, stage, sem, layer, e_cur, e_next, tig, gprev, cur, phase=0, phases=1):
    n_pairs = wbuf.shape[1] // 2
    cw = stage.shape[-1]
    prev_calls = gprev * phases
    q = tig * phases + phase

    def chunk_copy(e, c):
        return pltpu.make_async_copy(w_hbm.at[layer, e, :, pl.ds(c * cw, cw)], stage.at[c % 2], sem.at[c % 2])

    def start_pair(e, p):
        chunk_copy(e, 2 * p).start()
        chunk_copy(e, 2 * p + 1).start()

    def retire_pair(e, p, slot):
        for c in (2 * p, 2 * p + 1):
            chunk_copy(e, c).wait()
            wbuf[slot, c] = stage[c % 2].astype(BF16)

    if phase == 0:
        @pl.when(tig == 0)
        def _():
            for p in range(n_pairs):
                @pl.when(prev_calls <= p)
                def _():
                    start_pair(e_cur, p)

                @pl.when(prev_calls <= p + 1)
                def _():
                    retire_pair(e_cur, p, cur)

    @pl.when(e_next >= 0)
    def _():
        for j in range(phase, n_pairs + 1, phases):
            @pl.when(q == j)
            def _():
                if j >= 1:
                    retire_pair(e_next, j - 1, 1 - cur)
                if j < n_pairs:
                    start_pair(e_next, j)


def _ffn_up_kernel(te_ref, tig_ref, gprev_ref, nxt_ref, gidx_ref, nu_ref, x_ref, w_hbm, b_ref, o_ref,
                   wbuf, stage, sem, *, layer):
    i = pl.program_id(0)
    f = o_ref.shape[1]
    nc = wbuf.shape[1] // 2
    fc = wbuf.shape[-1]

    @pl.when(i < nu_ref[0])
    def _():
        cur = gidx_ref[i] % 2
        phases = 2
        xb = x_ref[...].astype(BF16)
        for c in range(nc):
            if c % (nc // phases) == 0:
                _stream_expert_weights(w_hbm, wbuf, stage, sem, layer, te_ref[i], nxt_ref[i], tig_ref[i],
                                       gprev_ref[i], cur, phase=c // (nc // phases), phases=phases)
            g = jnp.dot(xb, wbuf[cur, c], preferred_element_type=F32) + b_ref[:, c * fc:(c + 1) * fc]
            u = jnp.dot(xb, wbuf[cur, nc + c], preferred_element_type=F32) + b_ref[:, f + c * fc:f + (c + 1) * fc]
            x_glu = jnp.minimum(g, SWIGLU_LIMIT)
            x_lin = jnp.clip(u, -SWIGLU_LIMIT, SWIGLU_LIMIT)
            act = x_glu * _sigmoid(SWIGLU_ALPHA * x_glu) * (x_lin + 1.0)
            o_ref[:, c * fc:(c + 1) * fc] = act.astype(BF16)

    @pl.when(i >= nu_ref[0])
    def _():
        o_ref[...] = jnp.zeros_like(o_ref)


def _ffn_down_kernel(te_ref, tig_ref, gprev_ref, nxt_ref, gidx_ref, nu_ref, a_ref, w_hbm, b_ref, o_ref,
                     wbuf, stage, sem, *, layer):
    i = pl.program_id(0)
    fc = wbuf.shape[-1]

    @pl.when(i < nu_ref[0])
    def _():
        cur = gidx_ref[i] % 2
        _stream_expert_weights(w_hbm, wbuf, stage, sem, layer, te_ref[i], nxt_ref[i], tig_ref[i], gprev_ref[i], cur)
        a = a_ref[...]
        for c in range(wbuf.shape[1]):
            cols = slice(c * fc, (c + 1) * fc)
            o_ref[:, cols] = jnp.dot(a, wbuf[cur, c], preferred_element_type=F32) + b_ref[:, cols]

    @pl.when(i >= nu_ref[0])
    def _():
        o_ref[...] = jnp.zeros_like(o_ref)


def moe_experts(plan, xs, layer, w_gate_up, b_gate_up, w_down, b_down):
    n_rows, d = xs.shape
    tm = MOE_TILE
    n_tiles = n_rows // tm
    f = w_down.shape[2]
    cw = f // 4
    n_plan = len(plan)
    expert_row = lambda i, te, *_: (te[i], 0, 0)
    any_spec = pl.BlockSpec(memory_space=pl.ANY)
    act = pl.pallas_call(
        functools.partial(_ffn_up_kernel, layer=layer),
        grid_spec=pltpu.PrefetchScalarGridSpec(
            num_scalar_prefetch=n_plan,
            grid=(n_tiles,),
            in_specs=[pl.BlockSpec((tm, d), lambda i, *p: (jnp.minimum(i, p[-1][0] - 1), 0)),
                      any_spec,
                      pl.BlockSpec((None, 1, 2 * f), expert_row)],
            out_specs=pl.BlockSpec((tm, f), lambda i, *p: (i, 0)),
            scratch_shapes=[pltpu.VMEM((2, 2 * f // cw, d, cw), BF16), pltpu.VMEM((2, d, cw), F32),
                            pltpu.SemaphoreType.DMA((2,))],
        ),
        out_shape=jax.ShapeDtypeStruct((n_rows, f), BF16),
        compiler_params=_params("arbitrary"),
        name="moe_ffn_up",
    )(*plan, xs, w_gate_up, b_gate_up.reshape(N_EXPERTS, 1, 2 * f))
    return pl.pallas_call(
        functools.partial(_ffn_down_kernel, layer=layer),
        grid_spec=pltpu.PrefetchScalarGridSpec(
            num_scalar_prefetch=n_plan,
            grid=(n_tiles,),
            in_specs=[pl.BlockSpec((tm, f), lambda i, *p: (i, 0)),
                      any_spec,
                      pl.BlockSpec((None, 1, d), expert_row)],
            out_specs=pl.BlockSpec((tm, d), lambda i, *p: (i, 0)),
            scratch_shapes=[pltpu.VMEM((2, d // cw, f, cw), BF16), pltpu.VMEM((2, f, cw), F32),
                            pltpu.SemaphoreType.DMA((2,))],
        ),
        out_shape=jax.ShapeDtypeStruct((n_rows, d), F32),
        compiler_params=_params("arbitrary"),
        name="moe_ffn_down",
    )(*plan, act, w_down, b_down.reshape(N_EXPERTS, 1, d))


def _combine_kernel(dest_ref, x_ref, gate_ref, g2_ref, nw_ref, nsc_ref, nsh_ref, y_ref, *refs, emit_x):
    if emit_x:
        o_ref, hn_ref, ybuf, sem = refs
    else:
        hn_ref, ybuf, sem = refs
    i = pl.program_id(0)
    n = pl.num_programs(0)
    tm = x_ref.shape[0]

    def row_copy(tile, slot, r, kk):
        d = dest_ref[(tile * tm + r) * TOP_K + kk]
        return pltpu.make_async_copy(y_ref.at[pl.ds(d, 1), :], ybuf.at[slot, kk, pl.ds(r, 1), :], sem.at[slot])

    def issue_tile(tile, slot):
        def body(r, carry):
            for kk in range(TOP_K):
                row_copy(tile, slot, r, kk).start(priority=kk % 2)
            return carry
        lax.fori_loop(0, tm, body, 0, unroll=8)

    def drain_tile(slot):
        for kk in range(TOP_K):
            pltpu.make_async_copy(y_ref.at[pl.ds(0, tm), :], ybuf.at[slot, kk], sem.at[slot]).wait()

    slot = i % 2

    @pl.when(i == 0)
    def _():
        issue_tile(0, 0)

    @pl.when(i + 1 < n)
    def _():
        issue_tile(i + 1, 1 - slot)

    drain_tile(slot)
    gates = gate_ref[...]
    acc = ybuf[slot, 0] * gates[:, 0:1]
    for kk in range(1, TOP_K):
        acc = acc + ybuf[slot, kk] * gates[:, kk:kk + 1]
    x_new = x_ref[...] + g2_ref[...] * acc
    if emit_x:
        o_ref[...] = x_new
    hn_ref[...] = _norm_mod(x_new, nw_ref[...], nsc_ref[...], nsh_ref[...]).astype(hn_ref.dtype)


def moe_combine(dest, x, gates, gate2, y, next_norm, next_dtype, emit_x):
    t, d = x.shape
    tm = 128
    tile = pl.BlockSpec((tm, d), lambda i, dst: (i, 0))
    vec = pl.BlockSpec((1, d), lambda i, dst: (0, 0))
    normed = jax.ShapeDtypeStruct((t, d), next_dtype)
    return pl.pallas_call(
        functools.partial(_combine_kernel, emit_x=emit_x),
        grid_spec=pltpu.PrefetchScalarGridSpec(
            num_scalar_prefetch=1,
            grid=(t // tm,),
            in_specs=[tile, pl.BlockSpec((tm, LANES), lambda i, dst: (i, 0)), vec, vec, vec, vec,
                      pl.BlockSpec(memory_space=pl.ANY)],
            out_specs=[tile, tile] if emit_x else tile,
            scratch_shapes=[pltpu.VMEM((2, TOP_K, tm, d), F32), pltpu.SemaphoreType.DMA((2,))],
        ),
        out_shape=[jax.ShapeDtypeStruct((t, d), F32), normed] if emit_x else normed,
        compiler_params=_params("arbitrary"),
        name="moe_combine",
    )(dest.reshape(-1), x, gates, gate2, *next_norm, y)


def moe_block(x, norm_w, sc, sh, gate2, layer, w_router, b_router, w_gate_up, b_gate_up, w_down, b_down,
              next_norm, next_dtype, emit_x):
    t, d = x.shape
    tm = MOE_TILE
    h, idx, gates, pos, counts = moe_router(x, norm_w, sc, sh, w_router, b_router)
    cnt = counts[0, :N_EXPERTS].astype(I32)
    padded = (cnt + tm - 1) // tm * tm
    pend = jnp.cumsum(padded)
    pstart = pend - padded
    n_tiles = (t * TOP_K + N_EXPERTS * (tm - 1) + tm - 1) // tm
    is_expert = idx[:, :TOP_K, None] == jnp.arange(N_EXPERTS, dtype=I32)
    dest = pos[:, :TOP_K] + jnp.sum(jnp.where(is_expert, pstart, 0), axis=-1)
    tile_start = jnp.arange(n_tiles, dtype=I32) * tm
    tile_expert = jnp.minimum(jnp.sum((pend[None, :] <= tile_start[:, None]).astype(I32), axis=1), N_EXPERTS - 1)
    n_used = (pend[-1:] // tm).astype(I32)
    zero_plan = jnp.concatenate([jnp.where(padded > 0, pend - tm, -1), n_used]).astype(I32)
    xs = moe_dispatch(dest, zero_plan, h, n_tiles * tm)
    experts = jnp.arange(N_EXPERTS, dtype=I32)
    group_tiles = padded // tm
    nonempty = group_tiles > 0
    later = (experts[None, :] > experts[:, None]) & nonempty[None, :]
    earlier = (experts[None, :] < experts[:, None]) & nonempty[None, :]
    next_e = jnp.min(jnp.where(later, experts[None, :], N_EXPERTS), axis=1)
    prev_e = jnp.max(jnp.where(earlier, experts[None, :], -1), axis=1)
    next_e = jnp.where(next_e < N_EXPERTS, next_e, -1)
    prev_tiles = jnp.where(prev_e >= 0, group_tiles[jnp.maximum(prev_e, 0)], 0)
    ordinal = jnp.cumsum(nonempty.astype(I32)) - nonempty.astype(I32)
    tile_in_group = jnp.arange(n_tiles, dtype=I32) - (pstart // tm)[tile_expert]
    plan = tuple(a.astype(I32) for a in (tile_expert, tile_in_group, prev_tiles[tile_expert], next_e[tile_expert],
                                         ordinal[tile_expert], n_used))
    y = moe_experts(plan, xs, layer, w_gate_up, b_gate_up, w_down, b_down)
    return moe_combine(dest, x, gates, gate2, y, next_norm, next_dtype, emit_x)


def kernel(x, c, ada_w, ada_b, norm_w, dn_w_in, dn_conv_w, dn_a_log, dn_dt_bias, dn_o_norm_w, dn_w_out, sgu_w_in, sgu_b_in, sgu_ln_w, sgu_ln_b, sgu_w_sp, sgu_b_sp, sgu_w_out, sgu_b_out, moe_w_router, moe_b_router, moe_w_gate_up, moe_b_gate_up, moe_w_down, moe_b_down, final_norm_w):
    bsz, seq, d = x.shape
    assert bsz == 1 and d == D_MODEL
    xt = x.reshape(seq, d)
    mod = ada_mod(c, ada_w, ada_b)
    mods = [[mod[i:i + 1, s * d:(s + 1) * d] for s in range(6)] for i in range(DEPTH)]
    zero = jnp.zeros((1, d), F32)
    h = norm_mod(xt, norm_w[0, 0:1], mods[0][1], mods[0][0], BF16)
    for i in range(DEPTH):
        sh1, sc1, gt1, sh2, sc2, gt2 = mods[i]
        last = i == DEPTH - 1
        next_norm = (final_norm_w.reshape(1, d), zero, zero) if last else (norm_w[i + 1, 0:1], mods[i + 1][1], mods[i + 1][0])
        j = i // 2
        if i % 2 == 0:
            xt = gated_deltanet_block(xt, h, gt1, dn_w_in[j], dn_conv_w[j], dn_a_log[j], dn_dt_bias[j],
                                      dn_o_norm_w[j], dn_w_out[j])
        else:
            xt = chunked_gmlp_block(xt, h, gt1, sgu_w_in[j], sgu_b_in[j], sgu_ln_w[j], sgu_ln_b[j], sgu_w_sp[j],
                                    sgu_b_sp[j], sgu_w_out[j], sgu_b_out[j])
        res = moe_block(xt, norm_w[i, 1:2], sc2, sh2, gt2, i, moe_w_router[i], moe_b_router[i],
                        moe_w_gate_up, moe_b_gate_up[i], moe_w_down, moe_b_down[i],
                        next_norm, F32 if last else BF16, not last)
        if last:
            return res.reshape(bsz, seq, d)
        xt, h = res
```

```python
import functools

import jax
import jax.numpy as jnp
from jax import lax
from jax.experimental import pallas as pl
from jax.experimental.pallas import tpu as pltpu

F32 = jnp.float32
BF16 = jnp.bfloat16
I32 = jnp.int32
HIGHEST = lax.Precision.HIGHEST

D_MODEL = 2048
DEPTH = 2
NORM_EPS = 1e-6
DN_HEAD_DIM = 128
DN_QK_HEADS = 16
DN_V_HEADS = 32
DN_QK_DIM = 2048
DN_V_DIM = 4096
DN_CONV_CH = 8192
DN_CONV = 4
DN_CHUNK = 64
SGU_WIDTH = 4096
SGU_CHUNK = 128
SGU_GROUPS = 32
N_EXPERTS = 32
TOP_K = 4
EXPERT_DIM = 2048
SWIGLU_LIMIT = 7.0
SWIGLU_ALPHA = 1.702

LANES = 128
VMEM_LIMIT = 58 * 1024 * 1024
MOE_TILE = 512
ROW_BLOCK = 128
GDN_TILE = 2048


def _params(*sem):
    return pltpu.CompilerParams(dimension_semantics=sem, vmem_limit_bytes=VMEM_LIMIT)


def _iota(shape, dim):
    return lax.broadcasted_iota(I32, shape, dim)


def _sigmoid(x):
    return 1.0 / (1.0 + jnp.exp(-x))


def _silu(x):
    return x * _sigmoid(x)


def _ada_kernel(c_ref, w_ref, b_ref, o_ref):
    c = c_ref[...]
    o_ref[...] = jnp.dot(_silu(c), w_ref[...], precision=HIGHEST, preferred_element_type=F32) + b_ref[...]


def ada_mod(c, ada_w, ada_b):
    depth, d, n = ada_w.shape
    tn = 1024
    c8 = jnp.broadcast_to(c, (8, d))
    out = pl.pallas_call(
        _ada_kernel,
        grid=(depth, n // tn),
        in_specs=[
            pl.BlockSpec((8, d), lambda l, j: (0, 0)),
            pl.BlockSpec((None, d, tn), lambda l, j: (l, 0, j)),
            pl.BlockSpec((None, 1, tn), lambda l, j: (l, 0, j)),
        ],
        out_specs=pl.BlockSpec((None, 8, tn), lambda l, j: (l, 0, j)),
        out_shape=jax.ShapeDtypeStruct((depth, 8, n), F32),
        compiler_params=_params("parallel", "parallel"),
        name="ada_mod",
    )(c8, ada_w, ada_b.reshape(depth, 1, n))
    return out[:, 0, :]


def _norm_mod(x, w, sc, sh):
    y = x * lax.rsqrt(jnp.mean(x * x, axis=-1, keepdims=True) + NORM_EPS)
    return (y * w) * (1.0 + sc) + sh


def _norm_mod_kernel(x_ref, w_ref, sc_ref, sh_ref, o_ref):
    o_ref[...] = _norm_mod(x_ref[...], w_ref[...], sc_ref[...], sh_ref[...]).astype(o_ref.dtype)


def norm_mod(x, w, sc, sh, out_dtype):
    t, d = x.shape
    tm = 512
    vec = pl.BlockSpec((1, d), lambda i: (0, 0))
    return pl.pallas_call(
        _norm_mod_kernel,
        grid=(t // tm,),
        in_specs=[pl.BlockSpec((tm, d), lambda i: (i, 0)), vec, vec, vec],
        out_specs=pl.BlockSpec((tm, d), lambda i: (i, 0)),
        out_shape=jax.ShapeDtypeStruct((t, d), out_dtype),
        compiler_params=_params("parallel"),
        name="norm_mod",
    )(x, w, sc, sh)


def _gelu_exact(x):
    return 0.5 * x * (1.0 + lax.erf(x * (2.0 ** -0.5)))


def _mm_kernel(a_ref, w_ref, *refs, mode):
    *refs, wb = refs
    @pl.when(pl.program_id(1) == 0)
    def _():
        wb[...] = w_ref[...].astype(BF16)

    acc = jnp.dot(a_ref[...], wb[...], preferred_element_type=F32)
    if mode == "plain":
        (o_ref,) = refs
        o_ref[...] = acc.astype(o_ref.dtype)
    elif mode == "bias_gelu":
        b_ref, o_ref = refs
        o_ref[...] = _gelu_exact(acc + b_ref[...]).astype(o_ref.dtype)
    else:
        b_ref, res_ref, gate_ref, o_ref = refs
        o_ref[...] = res_ref[...] + gate_ref[...] * (acc + b_ref[...])


def matmul(a, w, n, *, mode, out_dtype, tm, tn, bias=None, res=None, gate=None):
    m, k = a.shape
    assert m % tm == 0 and n % tn == 0 and w.shape[1] >= n
    row = pl.BlockSpec((1, tn), lambda j, i: (0, j))
    in_specs = [pl.BlockSpec((tm, k), lambda j, i: (i, 0)), pl.BlockSpec((k, tn), lambda j, i: (0, j))]
    args = [a, w]
    if mode != "plain":
        in_specs.append(row)
        args.append(bias)
    if mode == "residual":
        in_specs += [pl.BlockSpec((tm, tn), lambda j, i: (i, j)), row]
        args += [res, gate]
    return pl.pallas_call(
        functools.partial(_mm_kernel, mode=mode),
        grid=(n // tn, m // tm),
        in_specs=in_specs,
        out_specs=pl.BlockSpec((tm, tn), lambda j, i: (i, j)),
        out_shape=jax.ShapeDtypeStruct((m, n), out_dtype),
        scratch_shapes=[pltpu.VMEM((k, tn), BF16)],
        compiler_params=_params("parallel", "arbitrary"),
        name="matmul_" + mode,
    )(*args)


def _softplus(x):
    return jnp.maximum(x, 0.0) + jnp.log1p(jnp.exp(-jnp.abs(x)))


def _unit_lower_inverses(lows, n_block):
    n = lows[0].shape[0]
    r = _iota((n, n), 0)
    c = _iota((n, n), 1)
    eye = jnp.where(r == c, 1.0, 0.0).astype(F32)

    def mm(a, b):
        return jnp.dot(a, b, preferred_element_type=F32)

    def bf(ms):
        return [m.astype(BF16) for m in ms]

    base = (r // 8) == (c // 8)
    lows_b = bf(lows)
    ds = [jnp.where(base, low, 0.0) for low in lows]
    ds_b = [jnp.where(base, low, jnp.zeros_like(low)) for low in lows_b]
    xs = [eye - d for d in ds]
    ps_b = bf([mm(d, d) for d in ds_b])
    xs = [x + mm(xb, p) for x, xb, p in zip(xs, bf(xs), ps_b)]
    ps_b = bf([mm(p, p) for p in ps_b])
    xs = [x + mm(xb, p) for x, xb, p in zip(xs, bf(xs), ps_b)]
    s = 8
    while s < n_block:
        pair = ((r // (2 * s)) == (c // (2 * s))) & ((r // s) != (c // s))
        xs_b = bf(xs)
        ts_b = bf([mm(jnp.where(pair, low, jnp.zeros_like(low)), xb) for low, xb in zip(lows_b, xs_b)])
        xs = [x - mm(xb, t) for x, xb, t in zip(xs, xs_b, ts_b)]
        s *= 2
    return xs


def _gdn_prep_kernel(q_ref, k_ref, v_ref, qh_ref, kh_ref, vh_ref, wq_ref, wk_ref, wv_ref, ab_ref, alog_ref, dtb_ref,
                     qg_ref, kd_ref, u_ref, w_ref, a_ref, gcum_ref,
                     win_q, win_k, win_v, gcum_s, gtot_s, beta_s):
    i = pl.program_id(0)
    h = pl.program_id(1)
    tm = q_ref.shape[0]
    halo = qh_ref.shape[0]
    cs = DN_CHUNK
    hd = DN_HEAD_DIM

    def conv_silu(x_ref, halo_ref, w_ref, win_ref):
        hal = halo_ref[...].astype(F32)
        win_ref[0:halo, :] = jnp.where(i == 0, jnp.zeros_like(hal), hal)
        win_ref[halo:halo + tm, :] = x_ref[...].astype(F32)
        w = w_ref[...]
        acc = None
        for j in range(DN_CONV):
            start = halo - (DN_CONV - 1) + j
            term = win_ref[start:start + tm, :] * w[j:j + 1, :]
            acc = term if acc is None else acc + term
        return _silu(acc)

    def l2norm(x):
        return x * lax.rsqrt(jnp.sum(x * x, axis=-1, keepdims=True) + NORM_EPS)

    @pl.when(h == 0)
    def _():
        ab = ab_ref[...]
        g = -jnp.exp(alog_ref[...]) * _softplus(ab + dtb_ref[...])
        sub = 4 * DN_CHUNK
        r = _iota((sub, sub), 0)
        c = _iota((sub, sub), 1)
        same = (r // DN_CHUNK) == (c // DN_CHUNK)
        tri = jnp.where(same & (c <= r), 1.0, 0.0).astype(F32)
        blk = jnp.where(same, 1.0, 0.0).astype(F32)
        for s0 in range(0, tm, sub):
            g_sub = g[s0:s0 + sub]
            gcum = jnp.dot(tri, g_sub, precision=HIGHEST, preferred_element_type=F32)
            gcum_s[s0:s0 + sub, :] = gcum
            gcum_ref[s0:s0 + sub, :] = gcum
            gtot_s[s0:s0 + sub, :] = jnp.dot(blk, g_sub, precision=HIGHEST, preferred_element_type=F32)
        beta_s[...] = _sigmoid(ab)

    q = l2norm(conv_silu(q_ref, qh_ref, wq_ref, win_q)) * (DN_HEAD_DIM ** -0.5)
    k = l2norm(conv_silu(k_ref, kh_ref, wk_ref, win_k))
    v = conv_silu(v_ref, vh_ref, wv_ref, win_v)

    def head_gates(ref, first_lane):
        rolled = pltpu.roll(ref[...], (LANES - first_lane) % LANES, 1)
        return [jnp.broadcast_to(rolled[:, s:s + 1], (tm, hd)) for s in range(2)]

    gc = head_gates(gcum_s, 2 * h)
    gt = head_gates(gtot_s, 2 * h)
    be = head_gates(beta_s, DN_V_HEADS + 2 * h)
    vb, kbg = [], []
    for s in range(2):
        cols = slice(s * hd, (s + 1) * hd)
        eg = jnp.exp(gc[s])
        qg_ref[:, cols] = (q * eg).astype(BF16)
        kd_ref[:, cols] = (k * jnp.exp(gt[s] - gc[s])).astype(BF16)
        vb.append((v[:, cols] * be[s]).astype(BF16))
        kbg.append((k * be[s] * eg).astype(BF16))
    qb = q.astype(BF16)
    kb = k.astype(BF16)

    n2 = 2 * cs
    r = _iota((n2, n2), 0)
    c = _iota((n2, n2), 1)
    same_head = (r // cs) == (c // cs)
    causal = same_head & (c <= r)
    strict = same_head & (c < r)
    top = _iota((cs, n2), 1) < cs
    nt = (((1,), (1,)), ((), ()))
    chunks = [slice(ci * cs, (ci + 1) * cs) for ci in range(tm // cs)]

    def stack_heads(x, rows):
        return jnp.concatenate([x[0][rows], x[1][rows]], axis=0)

    k_st = [jnp.concatenate([kb[rows], kb[rows]], axis=0) for rows in chunks]
    q_st = [jnp.concatenate([qb[rows], qb[rows]], axis=0) for rows in chunks]
    kk = [lax.dot_general(ks, ks, nt, preferred_element_type=F32) for ks in k_st]
    qk = [lax.dot_general(qs, ks, nt, preferred_element_type=F32) for qs, ks in zip(q_st, k_st)]
    g_col = [stack_heads(gc, rows) for rows in chunks]
    b_col = [stack_heads(be, rows) for rows in chunks]
    decay = [jnp.exp(jnp.where(causal, g - g.T, 0.0)) for g in g_col]
    lows = [jnp.where(strict, b * kk_i * d, 0.0) for b, kk_i, d in zip(b_col, kk, decay)]
    tinvs = [t.astype(BF16) for t in _unit_lower_inverses(lows, cs)]
    us = [jnp.dot(t, stack_heads(vb, rows), preferred_element_type=F32).astype(BF16) for t, rows in zip(tinvs, chunks)]
    ws = [jnp.dot(t, stack_heads(kbg, rows), preferred_element_type=F32).astype(BF16) for t, rows in zip(tinvs, chunks)]
    for rows, u2, w2, qk_i, d in zip(chunks, us, ws, qk, decay):
        intra = jnp.where(causal, qk_i * d, 0.0)
        a_ref[rows, :] = jnp.where(top, intra[:cs], intra[cs:]).astype(BF16)
        u_ref[rows, :hd] = u2[:cs]
        u_ref[rows, hd:] = u2[cs:]
        w_ref[rows, :hd] = w2[:cs]
        w_ref[rows, hd:] = w2[cs:]


def gdn_prep(proj, ab, conv_w, a_log, dt_bias):
    t = proj.shape[0]
    tm = GDN_TILE
    halo = 16
    hb = tm // halo
    hd = DN_HEAD_DIM
    nq = DN_QK_HEADS

    def halo_map(off):
        return lambda i, h: (jnp.maximum(i * hb - 1, 0), off + h)

    pad = jnp.zeros((1, LANES - DN_V_HEADS), F32)
    alog = jnp.concatenate([a_log.reshape(1, -1), pad], axis=1)
    dtb = jnp.concatenate([dt_bias.reshape(1, -1), pad], axis=1)
    big = jax.ShapeDtypeStruct((t, DN_V_DIM), BF16)
    small = jax.ShapeDtypeStruct((t, DN_QK_DIM), BF16)
    gshape = jax.ShapeDtypeStruct((t, LANES), F32)
    big_spec = pl.BlockSpec((tm, 2 * hd), lambda i, h: (i, h))
    small_spec = pl.BlockSpec((tm, hd), lambda i, h: (i, h))
    g_spec = pl.BlockSpec((tm, LANES), lambda i, h: (i, 0))
    row = pl.BlockSpec((1, LANES), lambda i, h: (0, 0))
    return pl.pallas_call(
        _gdn_prep_kernel,
        grid=(t // tm, nq),
        in_specs=[
            pl.BlockSpec((tm, hd), lambda i, h: (i, h)),
            pl.BlockSpec((tm, hd), lambda i, h: (i, nq + h)),
            pl.BlockSpec((tm, 2 * hd), lambda i, h: (i, nq + h)),
            pl.BlockSpec((halo, hd), halo_map(0)),
            pl.BlockSpec((halo, hd), halo_map(nq)),
            pl.BlockSpec((halo, 2 * hd), lambda i, h: (jnp.maximum(i * hb - 1, 0), nq + h)),
            pl.BlockSpec((DN_CONV, hd), lambda i, h: (0, h)),
            pl.BlockSpec((DN_CONV, hd), lambda i, h: (0, nq + h)),
            pl.BlockSpec((DN_CONV, 2 * hd), lambda i, h: (0, nq + h)),
            g_spec, row, row,
        ],
        out_specs=[big_spec, big_spec, big_spec, big_spec, small_spec, g_spec],
        out_shape=[big, big, big, big, small, gshape],
        scratch_shapes=[
            pltpu.VMEM((tm + halo, hd), F32), pltpu.VMEM((tm + halo, hd), F32), pltpu.VMEM((tm + halo, 2 * hd), F32),
            pltpu.VMEM((tm, LANES), F32), pltpu.VMEM((tm, LANES), F32), pltpu.VMEM((tm, LANES), F32),
        ],
        compiler_params=_params("parallel", "arbitrary"),
        name="gdn_prep",
    )(proj, proj, proj, proj, proj, proj, conv_w, conv_w, conv_w, ab, alog, dtb)


def _gdn_scan_kernel(glast_ref, qg_ref, kd_ref, u_ref, w_ref, a_ref, z_ref, onw_ref, og_ref, s_ref):
    ci = pl.program_id(0)
    cs = DN_CHUNK
    hd = DN_HEAD_DIM
    nh = DN_V_HEADS

    @pl.when(ci == 0)
    def _():
        s_ref[...] = jnp.zeros_like(s_ref)

    onw = onw_ref[...]
    tn = (((0,), (0,)), ((), ()))
    lane = _iota((cs, 2 * cs), 1)
    cols = [slice(hv * hd, (hv + 1) * hd) for hv in range(nh)]
    states = [s_ref[hv] for hv in range(nh)]
    ws = [jnp.dot(jnp.concatenate([w_ref[:, cols[hv]], qg_ref[:, cols[hv]]], axis=0), states[hv].astype(BF16),
                  preferred_element_type=F32) for hv in range(nh)]
    v_new = [(u_ref[:, cols[hv]].astype(F32) - ws[hv][:cs]).astype(BF16) for hv in range(nh)]
    for hv in range(nh):
        decay = jnp.exp(jnp.full((1, hd), glast_ref[ci, hv], F32))
        s_ref[hv] = states[hv] * decay + lax.dot_general(kd_ref[:, cols[hv]], v_new[hv], tn, preferred_element_type=F32)
    for pair in range(nh // 2):
        a2 = a_ref[:, pair * 2 * cs:(pair + 1) * 2 * cs]
        v2 = jnp.concatenate([v_new[2 * pair], v_new[2 * pair + 1]], axis=0)
        for s in range(2):
            hv = 2 * pair + s
            a_s = jnp.where((lane // cs) == s, a2, jnp.zeros_like(a2))
            o = ws[hv][cs:] + jnp.dot(a_s, v2, preferred_element_type=F32)
            on = o * lax.rsqrt(jnp.mean(o * o, axis=-1, keepdims=True) + NORM_EPS) * onw
            og_ref[:, cols[hv]] = (on * _silu(z_ref[:, cols[hv]].astype(F32))).astype(BF16)


def gdn_scan(glast, qg, kd, u, w, intra, proj, o_norm_w):
    t = qg.shape[0]
    cs = DN_CHUNK
    big_spec = pl.BlockSpec((cs, DN_V_DIM), lambda c, g: (c, 0))
    return pl.pallas_call(
        _gdn_scan_kernel,
        grid_spec=pltpu.PrefetchScalarGridSpec(
            num_scalar_prefetch=1,
            grid=(t // cs,),
            in_specs=[big_spec, big_spec, big_spec, big_spec,
                      pl.BlockSpec((cs, DN_V_HEADS * cs), lambda c, g: (c, 0)),
                      pl.BlockSpec((cs, DN_V_DIM), lambda c, g: (c, DN_CONV_CH // DN_V_DIM)),
                      pl.BlockSpec((1, DN_HEAD_DIM), lambda c, g: (0, 0))],
            out_specs=big_spec,
            scratch_shapes=[pltpu.VMEM((DN_V_HEADS, DN_HEAD_DIM, DN_HEAD_DIM), F32)],
        ),
        out_shape=jax.ShapeDtypeStruct((t, DN_V_DIM), BF16),
        compiler_params=_params("arbitrary"),
        name="gdn_scan",
    )(glast, qg, kd, u, w, intra, proj, o_norm_w.reshape(1, -1))


def gated_deltanet_block(x, h, gate, w_in, conv_w, a_log, dt_bias, o_norm_w, w_out):
    t = x.shape[0]
    n_main = DN_CONV_CH + DN_V_DIM
    w_ab = jnp.pad(w_in[:, n_main:], ((0, 0), (0, LANES - 2 * DN_V_HEADS)))
    proj = matmul(h, w_in, n_main, mode="plain", out_dtype=BF16, tm=1024, tn=1024)
    ab = matmul(h, w_ab, LANES, mode="plain", out_dtype=F32, tm=1024, tn=LANES)
    qg, kd, u, w, intra, gcum = gdn_prep(proj, ab, conv_w, a_log, dt_bias)
    glast = gcum.reshape(t // DN_CHUNK, DN_CHUNK, LANES)[:, DN_CHUNK - 1, :DN_V_HEADS]
    og = gdn_scan(glast, qg, kd, u, w, intra, proj, o_norm_w)
    zero_bias = jnp.zeros((1, D_MODEL), F32)
    return matmul(og, w_out, D_MODEL, mode="residual", out_dtype=F32, tm=1024, tn=512,
                  bias=zero_bias, res=x, gate=gate)


def _sgu_spatial_kernel(u_ref, v_ref, lnw_ref, lnb_ref, wsp_ref, bsp_ref, o_ref):
    cs = SGU_CHUNK
    gd = SGU_WIDTH // SGU_GROUPS
    v = v_ref[...].astype(F32)
    mu = jnp.mean(v, axis=-1, keepdims=True)
    var = jnp.mean(jnp.square(v - mu), axis=-1, keepdims=True)
    vn = ((v - mu) * lax.rsqrt(var + NORM_EPS) * lnw_ref[...] + lnb_ref[...]).astype(BF16)
    r = _iota((cs, cs), 0)
    c = _iota((cs, cs), 1)
    bsp = bsp_ref[...]
    for g in range(SGU_GROUPS):
        cols = slice(g * gd, (g + 1) * gd)
        wg = jnp.where(c <= r, wsp_ref[g], 0.0).astype(BF16)
        sp = jnp.dot(wg, vn[:, cols], preferred_element_type=F32) + bsp[:, g:g + 1]
        o_ref[:, cols] = (u_ref[:, cols].astype(F32) * sp).astype(BF16)


def sgu_spatial(zz, ln_w, ln_b, w_sp, b_sp):
    t = zz.shape[0]
    cs = SGU_CHUNK
    wd = SGU_WIDTH
    bsp_t = jnp.pad(b_sp.T, ((0, 0), (0, LANES - SGU_GROUPS)))
    row = pl.BlockSpec((1, wd), lambda i: (0, 0))
    return pl.pallas_call(
        _sgu_spatial_kernel,
        grid=(t // cs,),
        in_specs=[
            pl.BlockSpec((cs, wd), lambda i: (i, 0)),
            pl.BlockSpec((cs, wd), lambda i: (i, 1)),
            row, row,
            pl.BlockSpec((SGU_GROUPS, cs, cs), lambda i: (0, 0, 0)),
            pl.BlockSpec((cs, LANES), lambda i: (0, 0)),
        ],
        out_specs=pl.BlockSpec((cs, wd), lambda i: (i, 0)),
        out_shape=jax.ShapeDtypeStruct((t, wd), BF16),
        compiler_params=_params("parallel"),
        name="sgu_spatial",
    )(zz, zz, ln_w.reshape(1, -1), ln_b.reshape(1, -1), w_sp, bsp_t)


def chunked_gmlp_block(x, h, gate, w_in, b_in, ln_w, ln_b, w_sp, b_sp, w_out, b_out):
    zz = matmul(h, w_in, 2 * SGU_WIDTH, mode="bias_gelu", out_dtype=BF16, tm=1024, tn=1024, bias=b_in.reshape(1, -1))
    su = sgu_spatial(zz, ln_w, ln_b, w_sp, b_sp)
    return matmul(su, w_out, D_MODEL, mode="residual", out_dtype=F32, tm=1024, tn=512,
                  bias=b_out.reshape(1, -1), res=x, gate=gate)


def _dot_split3(a, b):
    a_hi = a.astype(BF16)
    b_hi = b.astype(BF16)
    a_lo = (a - a_hi.astype(F32)).astype(BF16)
    b_lo = (b - b_hi.astype(F32)).astype(BF16)
    return (jnp.dot(a_hi, b_hi, preferred_element_type=F32)
            + (jnp.dot(a_lo, b_hi, preferred_element_type=F32) + jnp.dot(a_hi, b_lo, preferred_element_type=F32)))


def _router_kernel(x_ref, w_ref, sc_ref, sh_ref, wr_ref, br_ref, idx_ref, gate_ref, pos_ref, cnt_ref, carry):
    i = pl.program_id(0)
    tm = x_ref.shape[0]

    @pl.when(i == 0)
    def _():
        carry[...] = jnp.zeros_like(carry)

    h = _norm_mod(x_ref[...], w_ref[...], sc_ref[...], sh_ref[...])
    lane = _iota((tm, LANES), 1).astype(F32)
    neg = jnp.float32(-jnp.inf)
    logits = _dot_split3(h, wr_ref[...]) + br_ref[...]
    logits = jnp.where(lane < N_EXPERTS, logits, neg)
    vals, idxs = [], []
    for _ in range(TOP_K):
        m = jnp.max(logits, axis=-1, keepdims=True)
        ix = jnp.min(jnp.where(logits == m, lane, float(LANES)), axis=-1, keepdims=True)
        vals.append(m)
        idxs.append(ix)
        logits = jnp.where(lane == ix, neg, logits)
    es = [jnp.exp(v - vals[0]) for v in vals]
    denom = es[0] + es[1] + es[2] + es[3]
    multi = jnp.zeros((tm, LANES), F32)
    for ix in idxs:
        multi = jnp.where(lane == ix, 1.0, multi)
    r = _iota((tm, tm), 0)
    c = _iota((tm, tm), 1)
    before = jnp.where(c < r, 1.0, 0.0).astype(BF16)
    rank = jnp.dot(before, multi.astype(BF16), preferred_element_type=F32) + carry[0:1, :]
    idx_t = jnp.zeros((tm, LANES), F32)
    gate_t = jnp.zeros((tm, LANES), F32)
    pos_t = jnp.zeros((tm, LANES), F32)
    for kk in range(TOP_K):
        pk = jnp.sum(jnp.where(lane == idxs[kk], rank, 0.0), axis=-1, keepdims=True)
        idx_t = jnp.where(lane == kk, idxs[kk], idx_t)
        gate_t = jnp.where(lane == kk, es[kk] / denom, gate_t)
        pos_t = jnp.where(lane == kk, pk, pos_t)
    idx_ref[...] = idx_t.astype(I32)
    gate_ref[...] = gate_t
    pos_ref[...] = pos_t.astype(I32)
    carry[...] = carry[...] + jnp.sum(multi, axis=0, keepdims=True)
    cnt_ref[...] = carry[...]


def moe_router(x, w, sc, sh, w_router, b_router):
    t, d = x.shape
    tm = 256
    wr = jnp.pad(w_router, ((0, 0), (0, LANES - N_EXPERTS)))
    br = jnp.pad(b_router.reshape(1, -1), ((0, 0), (0, LANES - N_EXPERTS)))
    vec = pl.BlockSpec((1, d), lambda i: (0, 0))
    tile = pl.BlockSpec((tm, LANES), lambda i: (i, 0))
    return pl.pallas_call(
        _router_kernel,
        grid=(t // tm,),
        in_specs=[pl.BlockSpec((tm, d), lambda i: (i, 0)), vec, vec, vec,
                  pl.BlockSpec((d, LANES), lambda i: (0, 0)), pl.BlockSpec((1, LANES), lambda i: (0, 0))],
        out_specs=[tile, tile, tile, pl.BlockSpec((8, LANES), lambda i: (0, 0))],
        out_shape=[jax.ShapeDtypeStruct((t, LANES), I32), jax.ShapeDtypeStruct((t, LANES), F32),
                   jax.ShapeDtypeStruct((t, LANES), I32), jax.ShapeDtypeStruct((8, LANES), F32)],
        scratch_shapes=[pltpu.VMEM((8, LANES), F32)],
        compiler_params=_params("arbitrary"),
        name="moe_router",
    )(x, w, sc, sh, wr, br)


def _dispatch_kernel(dest_ref, ztile_ref, x_ref, w_ref, sc_ref, sh_ref, xs_ref, h_ref, zbuf, sem, zsem):
    i = pl.program_id(0)
    tm = h_ref.shape[0]
    zt = zbuf.shape[0]
    h_ref[...] = _norm_mod(x_ref[...], w_ref[...], sc_ref[...], sh_ref[...])

    @pl.when(i == 0)
    def _():
        zbuf[...] = jnp.zeros_like(zbuf)

        def zero_copy(e):
            row = pl.multiple_of(ztile_ref[e], zt)
            return pltpu.make_async_copy(zbuf, xs_ref.at[pl.ds(row, zt), :], zsem)

        for e in range(N_EXPERTS):
            @pl.when(ztile_ref[e] >= 0)
            def _():
                zero_copy(e).start()
        for e in range(N_EXPERTS):
            @pl.when(ztile_ref[e] >= 0)
            def _():
                zero_copy(e).wait()

        def zero_tail(j, carry):
            row = pl.multiple_of(j * zt, zt)
            cp = pltpu.make_async_copy(zbuf, xs_ref.at[pl.ds(row, zt), :], zsem)
            cp.start()
            cp.wait()
            return carry

        lax.fori_loop(ztile_ref[N_EXPERTS], xs_ref.shape[0] // zt, zero_tail, 0)

    def row_copy(r, d):
        return pltpu.make_async_copy(h_ref.at[pl.ds(r, 1), :], xs_ref.at[pl.ds(d, 1), :], sem)

    def issue(r, carry):
        base = (i * tm + r) * TOP_K
        for kk in range(TOP_K):
            row_copy(r, dest_ref[base + kk]).start(priority=kk % 2)
        return carry

    lax.fori_loop(0, tm, issue, 0, unroll=8)

    for kk in range(TOP_K):
        pltpu.make_async_copy(h_ref, xs_ref.at[pl.ds(0, tm), :], sem).wait()


def moe_dispatch(dest, zero_plan, x, w, sc, sh, n_rows):
    t, d = x.shape
    tm = 256
    vec = pl.BlockSpec((1, d), lambda i, dst, zt: (0, 0))
    return pl.pallas_call(
        _dispatch_kernel,
        grid_spec=pltpu.PrefetchScalarGridSpec(
            num_scalar_prefetch=2,
            grid=(t // tm,),
            in_specs=[pl.BlockSpec((tm, d), lambda i, dst, zt: (i, 0)), vec, vec, vec],
            out_specs=pl.BlockSpec(memory_space=pl.ANY),
            scratch_shapes=[pltpu.VMEM((tm, d), F32), pltpu.VMEM((MOE_TILE, d), F32),
                            pltpu.SemaphoreType.DMA(()), pltpu.SemaphoreType.DMA(())],
        ),
        out_shape=jax.ShapeDtypeStruct((n_rows, d), F32),
        compiler_params=_params("arbitrary"),
        name="moe_dispatch",
    )(dest.reshape(-1), zero_plan, x, w, sc, sh)


def _stream_expert_weights(w_hbm, wbuf, stage, sem, layer, e_cur, e_next, tig, gprev, cur, phase=0, phases=1):
    n_pairs = wbuf.shape[1] // 2
    cw = stage.shape[-1]
    prev_calls = gprev * phases
    q = tig * phases + phase

    def chunk_copy(e, c):
        return pltpu.make_async_copy(w_hbm.at[layer, e, :, pl.ds(c * cw, cw)], stage.at[c % 2], sem.at[c % 2])

    def start_pair(e, p):
        chunk_copy(e, 2 * p).start()
        chunk_copy(e, 2 * p + 1).start()

    def retire_pair(e, p, slot):
        for c in (2 * p, 2 * p + 1):
            chunk_copy(e, c).wait()
            wbuf[slot, c] = stage[c % 2].astype(BF16)

    if phase == 0:
        @pl.when(tig == 0)
        def _():
            for p in range(n_pairs):
                @pl.when(prev_calls <= p)
                def _():
                    start_pair(e_cur, p)

                @pl.when(prev_calls <= p + 1)
                def _():
                    retire_pair(e_cur, p, cur)

    @pl.when(e_next >= 0)
    def _():
        for j in range(phase, n_pairs + 1, phases):
            @pl.when(q == j)
            def _():
                if j >= 1:
                    retire_pair(e_next, j - 1, 1 - cur)
                if j < n_pairs:
                    start_pair(e_next, j)


def _for_row_bucket(n_blocks, tm, body):
    for k in range(1, tm // ROW_BLOCK + 1):
        @pl.when(n_blocks == k)
        def _():
            body(k * ROW_BLOCK)


def _ffn_up_kernel(te_ref, tig_ref, gprev_ref, nxt_ref, gidx_ref, nb_ref, nu_ref, x_ref, w_hbm, b_ref, o_ref,
                   wbuf, stage, sem, *, layer):
    i = pl.program_id(0)
    tm, f = o_ref.shape
    nc = wbuf.shape[1] // 2
    fc = wbuf.shape[-1]

    @pl.when(i < nu_ref[0])
    def _():
        cur = gidx_ref[i] % 2
        _stream_expert_weights(w_hbm, wbuf, stage, sem, layer, te_ref[i], nxt_ref[i], tig_ref[i], gprev_ref[i], cur)

        def compute(m):
            xb = x_ref[:m, :].astype(BF16)
            for c in range(nc):
                g = jnp.dot(xb, wbuf[cur, c], preferred_element_type=F32) + b_ref[:, c * fc:(c + 1) * fc]
                u = jnp.dot(xb, wbuf[cur, nc + c], preferred_element_type=F32) + b_ref[:, f + c * fc:f + (c + 1) * fc]
                x_glu = jnp.minimum(g, SWIGLU_LIMIT)
                x_lin = jnp.clip(u, -SWIGLU_LIMIT, SWIGLU_LIMIT)
                act = x_glu * _sigmoid(SWIGLU_ALPHA * x_glu) * (x_lin + 1.0)
                o_ref[:m, c * fc:(c + 1) * fc] = act.astype(BF16)
            if m < tm:
                o_ref[m:, :] = jnp.zeros((tm - m, f), BF16)

        _for_row_bucket(nb_ref[i], tm, compute)

    @pl.when(i >= nu_ref[0])
    def _():
        o_ref[...] = jnp.zeros_like(o_ref)


def _ffn_down_kernel(te_ref, tig_ref, gprev_ref, nxt_ref, gidx_ref, nb_ref, nu_ref, a_ref, w_hbm, b_ref, o_ref,
                     wbuf, stage, sem, *, layer):
    i = pl.program_id(0)
    tm, d = o_ref.shape
    fc = wbuf.shape[-1]

    @pl.when(i < nu_ref[0])
    def _():
        cur = gidx_ref[i] % 2
        _stream_expert_weights(w_hbm, wbuf, stage, sem, layer, te_ref[i], nxt_ref[i], tig_ref[i], gprev_ref[i], cur)

        def compute(m):
            a = a_ref[:m, :]
            for c in range(wbuf.shape[1]):
                cols = slice(c * fc, (c + 1) * fc)
                o_ref[:m, cols] = jnp.dot(a, wbuf[cur, c], preferred_element_type=F32) + b_ref[:, cols]
            if m < tm:
                o_ref[m:, :] = jnp.zeros((tm - m, d), F32)

        _for_row_bucket(nb_ref[i], tm, compute)

    @pl.when(i >= nu_ref[0])
    def _():
        o_ref[...] = jnp.zeros_like(o_ref)


def moe_experts(plan, xs, layer, w_gate_up, b_gate_up, w_down, b_down):
    n_rows, d = xs.shape
    tm = MOE_TILE
    n_tiles = n_rows // tm
    f = w_down.shape[2]
    cw = f // 4
    n_plan = len(plan)
    expert_row = lambda i, te, *_: (te[i], 0, 0)
    any_spec = pl.BlockSpec(memory_space=pl.ANY)
    act = pl.pallas_call(
        functools.partial(_ffn_up_kernel, layer=layer),
        grid_spec=pltpu.PrefetchScalarGridSpec(
            num_scalar_prefetch=n_plan,
            grid=(n_tiles,),
            in_specs=[pl.BlockSpec((tm, d), lambda i, *p: (jnp.minimum(i, p[-1][0] - 1), 0)),
                      any_spec,
                      pl.BlockSpec((None, 1, 2 * f), expert_row)],
            out_specs=pl.BlockSpec((tm, f), lambda i, *p: (i, 0)),
            scratch_shapes=[pltpu.VMEM((2, 2 * f // cw, d, cw), BF16), pltpu.VMEM((2, d, cw), F32),
                            pltpu.SemaphoreType.DMA((2,))],
        ),
        out_shape=jax.ShapeDtypeStruct((n_rows, f), BF16),
        compiler_params=_params("arbitrary"),
        name="moe_ffn_up",
    )(*plan, xs, w_gate_up, b_gate_up.reshape(N_EXPERTS, 1, 2 * f))
    return pl.pallas_call(
        functools.partial(_ffn_down_kernel, layer=layer),
        grid_spec=pltpu.PrefetchScalarGridSpec(
            num_scalar_prefetch=n_plan,
            grid=(n_tiles,),
            in_specs=[pl.BlockSpec((tm, f), lambda i, *p: (i, 0)),
                      any_spec,
                      pl.BlockSpec((None, 1, d), expert_row)],
            out_specs=pl.BlockSpec((tm, d), lambda i, *p: (i, 0)),
            scratch_shapes=[pltpu.VMEM((2, d // cw, f, cw), BF16), pltpu.VMEM((2, f, cw), F32),
                            pltpu.SemaphoreType.DMA((2,))],
        ),
        out_shape=jax.ShapeDtypeStruct((n_rows, d), F32),
        compiler_params=_params("arbitrary"),
        name="moe_ffn_down",
    )(*plan, act, w_down, b_down.reshape(N_EXPERTS, 1, d))


def _combine_kernel(dest_ref, x_ref, gate_ref, g2_ref, nw_ref, nsc_ref, nsh_ref, y_ref, *refs, emit_x):
    if emit_x:
        o_ref, hn_ref, ybuf, sem = refs
    else:
        hn_ref, ybuf, sem = refs
    i = pl.program_id(0)
    n = pl.num_programs(0)
    tm = x_ref.shape[0]

    def row_copy(tile, slot, r, kk):
        d = dest_ref[(tile * tm + r) * TOP_K + kk]
        return pltpu.make_async_copy(y_ref.at[pl.ds(d, 1), :], ybuf.at[slot, kk, pl.ds(r, 1), :], sem.at[slot])

    def issue_tile(tile, slot):
        def body(r, carry):
            for kk in range(TOP_K):
                row_copy(tile, slot, r, kk).start(priority=kk % 2)
            return carry
        lax.fori_loop(0, tm, body, 0, unroll=8)

    def drain_tile(slot):
        for kk in range(TOP_K):
            pltpu.make_async_copy(y_ref.at[pl.ds(0, tm), :], ybuf.at[slot, kk], sem.at[slot]).wait()

    slot = i % 2

    @pl.when(i == 0)
    def _():
        issue_tile(0, 0)

    @pl.when(i + 1 < n)
    def _():
        issue_tile(i + 1, 1 - slot)

    drain_tile(slot)
    gates = gate_ref[...]
    acc = ybuf[slot, 0] * gates[:, 0:1]
    for kk in range(1, TOP_K):
        acc = acc + ybuf[slot, kk] * gates[:, kk:kk + 1]
    x_new = x_ref[...] + g2_ref[...] * acc
    if emit_x:
        o_ref[...] = x_new
    hn_ref[...] = _norm_mod(x_new, nw_ref[...], nsc_ref[...], nsh_ref[...]).astype(hn_ref.dtype)


def moe_combine(dest, x, gates, gate2, y, next_norm, next_dtype, emit_x):
    t, d = x.shape
    tm = 128
    tile = pl.BlockSpec((tm, d), lambda i, dst: (i, 0))
    vec = pl.BlockSpec((1, d), lambda i, dst: (0, 0))
    normed = jax.ShapeDtypeStruct((t, d), next_dtype)
    return pl.pallas_call(
        functools.partial(_combine_kernel, emit_x=emit_x),
        grid_spec=pltpu.PrefetchScalarGridSpec(
            num_scalar_prefetch=1,
            grid=(t // tm,),
            in_specs=[tile, pl.BlockSpec((tm, LANES), lambda i, dst: (i, 0)), vec, vec, vec, vec,
                      pl.BlockSpec(memory_space=pl.ANY)],
            out_specs=[tile, tile] if emit_x else tile,
            scratch_shapes=[pltpu.VMEM((2, TOP_K, tm, d), F32), pltpu.SemaphoreType.DMA((2,))],
        ),
        out_shape=[jax.ShapeDtypeStruct((t, d), F32), normed] if emit_x else normed,
        compiler_params=_params("arbitrary"),
        name="moe_combine",
    )(dest.reshape(-1), x, gates, gate2, *next_norm, y)


def moe_block(x, norm_w, sc, sh, gate2, layer, w_router, b_router, w_gate_up, b_gate_up, w_down, b_down,
              next_norm, next_dtype, emit_x):
    t, d = x.shape
    tm = MOE_TILE
    idx, gates, pos, counts = moe_router(x, norm_w, sc, sh, w_router, b_router)
    cnt = counts[0, :N_EXPERTS].astype(I32)
    padded = (cnt + tm - 1) // tm * tm
    pend = jnp.cumsum(padded)
    pstart = pend - padded
    n_tiles = (t * TOP_K + N_EXPERTS * (tm - 1) + tm - 1) // tm
    is_expert = idx[:, :TOP_K, None] == jnp.arange(N_EXPERTS, dtype=I32)
    dest = pos[:, :TOP_K] + jnp.sum(jnp.where(is_expert, pstart, 0), axis=-1)
    tile_start = jnp.arange(n_tiles, dtype=I32) * tm
    tile_expert = jnp.minimum(jnp.sum((pend[None, :] <= tile_start[:, None]).astype(I32), axis=1), N_EXPERTS - 1)
    n_used = (pend[-1:] // tm).astype(I32)
    zero_plan = jnp.concatenate([jnp.where(padded > 0, pend - tm, -1), n_used]).astype(I32)
    xs = moe_dispatch(dest, zero_plan, x, norm_w, sc, sh, n_tiles * tm)
    experts = jnp.arange(N_EXPERTS, dtype=I32)
    group_tiles = padded // tm
    nonempty = group_tiles > 0
    later = (experts[None, :] > experts[:, None]) & nonempty[None, :]
    earlier = (experts[None, :] < experts[:, None]) & nonempty[None, :]
    next_e = jnp.min(jnp.where(later, experts[None, :], N_EXPERTS), axis=1)
    prev_e = jnp.max(jnp.where(earlier, experts[None, :], -1), axis=1)
    next_e = jnp.where(next_e < N_EXPERTS, next_e, -1)
    prev_tiles = jnp.where(prev_e >= 0, group_tiles[jnp.maximum(prev_e, 0)], 0)
    ordinal = jnp.cumsum(nonempty.astype(I32)) - nonempty.astype(I32)
    tile_in_group = jnp.arange(n_tiles, dtype=I32) - (pstart // tm)[tile_expert]
    rows_in_tile = jnp.clip(cnt[tile_expert] - tile_in_group * tm, 1, tm)
    row_blocks = (rows_in_tile + ROW_BLOCK - 1) // ROW_BLOCK
    plan = tuple(a.astype(I32) for a in (tile_expert, tile_in_group, prev_tiles[tile_expert], next_e[tile_expert],
                                         ordinal[tile_expert], row_blocks, n_used))
    y = moe_experts(plan, xs, layer, w_gate_up, b_gate_up, w_down, b_down)
    return moe_combine(dest, x, gates, gate2, y, next_norm, next_dtype, emit_x)


def kernel(x, c, ada_w, ada_b, norm_w, dn_w_in, dn_conv_w, dn_a_log, dn_dt_bias, dn_o_norm_w, dn_w_out, sgu_w_in, sgu_b_in, sgu_ln_w, sgu_ln_b, sgu_w_sp, sgu_b_sp, sgu_w_out, sgu_b_out, moe_w_router, moe_b_router, moe_w_gate_up, moe_b_gate_up, moe_w_down, moe_b_down, final_norm_w):
    bsz, seq, d = x.shape
    assert bsz == 1 and d == D_MODEL
    xt = x.reshape(seq, d)
    mod = ada_mod(c, ada_w, ada_b)
    mods = [[mod[i:i + 1, s * d:(s + 1) * d] for s in range(6)] for i in range(DEPTH)]
    zero = jnp.zeros((1, d), F32)
    h = norm_mod(xt, norm_w[0, 0:1], mods[0][1], mods[0][0], BF16)
    for i in range(DEPTH):
        sh1, sc1, gt1, sh2, sc2, gt2 = mods[i]
        last = i == DEPTH - 1
        next_norm = (final_norm_w.reshape(1, d), zero, zero) if last else (norm_w[i + 1, 0:1], mods[i + 1][1], mods[i + 1][0])
        j = i // 2
        if i % 2 == 0:
            xt = gated_deltanet_block(xt, h, gt1, dn_w_in[j], dn_conv_w[j], dn_a_log[j], dn_dt_bias[j],
                                      dn_o_norm_w[j], dn_w_out[j])
        else:
            xt = chunked_gmlp_block(xt, h, gt1, sgu_w_in[j], sgu_b_in[j], sgu_ln_w[j], sgu_ln_b[j], sgu_w_sp[j],
                                    sgu_b_sp[j], sgu_w_out[j], sgu_b_out[j])
        res = moe_block(xt, norm_w[i, 1:2], sc2, sh2, gt2, i, moe_w_router[i], moe_b_router[i],
                        moe_w_gate_up, moe_b_gate_up[i], moe_w_down, moe_b_down[i],
                        next_norm, F32 if last else BF16, not last)
        if last:
            return res.reshape(bsz, seq, d)
        xt, h = res
```

```python
import functools

import jax
import jax.numpy as jnp
from jax import lax
from jax.experimental import pallas as pl
from jax.experimental.pallas import tpu as pltpu

F32 = jnp.float32
BF16 = jnp.bfloat16
I32 = jnp.int32
HIGHEST = lax.Precision.HIGHEST

D_MODEL = 2048
DEPTH = 2
NORM_EPS = 1e-6
DN_HEAD_DIM = 128
DN_QK_HEADS = 16
DN_V_HEADS = 32
DN_QK_DIM = 2048
DN_V_DIM = 4096
DN_CONV_CH = 8192
DN_CONV = 4
DN_CHUNK = 64
SGU_WIDTH = 4096
SGU_CHUNK = 128
SGU_GROUPS = 32
N_EXPERTS = 32
TOP_K = 4
EXPERT_DIM = 2048
SWIGLU_LIMIT = 7.0
SWIGLU_ALPHA = 1.702

LANES = 128
VMEM_LIMIT = 58 * 1024 * 1024
MOE_TILE = 512
ROW_BLOCK = 128
GDN_TILE = 2048


def _params(*sem):
    return pltpu.CompilerParams(dimension_semantics=sem, vmem_limit_bytes=VMEM_LIMIT)


def _iota(shape, dim):
    return lax.broadcasted_iota(I32, shape, dim)


def _sigmoid(x):
    return 1.0 / (1.0 + jnp.exp(-x))


def _silu(x):
    return x * _sigmoid(x)


def _ada_kernel(c_ref, w_ref, b_ref, o_ref):
    c = c_ref[...]
    o_ref[...] = jnp.dot(_silu(c), w_ref[...], precision=HIGHEST, preferred_element_type=F32) + b_ref[...]


def ada_mod(c, ada_w, ada_b):
    depth, d, n = ada_w.shape
    tn = 1024
    c8 = jnp.broadcast_to(c, (8, d))
    out = pl.pallas_call(
        _ada_kernel,
        grid=(depth, n // tn),
        in_specs=[
            pl.BlockSpec((8, d), lambda l, j: (0, 0)),
            pl.BlockSpec((None, d, tn), lambda l, j: (l, 0, j)),
            pl.BlockSpec((None, 1, tn), lambda l, j: (l, 0, j)),
        ],
        out_specs=pl.BlockSpec((None, 8, tn), lambda l, j: (l, 0, j)),
        out_shape=jax.ShapeDtypeStruct((depth, 8, n), F32),
        compiler_params=_params("parallel", "parallel"),
        name="ada_mod",
    )(c8, ada_w, ada_b.reshape(depth, 1, n))
    return out[:, 0, :]


def _norm_mod(x, w, sc, sh):
    y = x * lax.rsqrt(jnp.mean(x * x, axis=-1, keepdims=True) + NORM_EPS)
    return (y * w) * (1.0 + sc) + sh


def _norm_mod_kernel(x_ref, w_ref, sc_ref, sh_ref, o_ref):
    o_ref[...] = _norm_mod(x_ref[...], w_ref[...], sc_ref[...], sh_ref[...]).astype(o_ref.dtype)


def norm_mod(x, w, sc, sh, out_dtype):
    t, d = x.shape
    tm = 512
    vec = pl.BlockSpec((1, d), lambda i: (0, 0))
    return pl.pallas_call(
        _norm_mod_kernel,
        grid=(t // tm,),
        in_specs=[pl.BlockSpec((tm, d), lambda i: (i, 0)), vec, vec, vec],
        out_specs=pl.BlockSpec((tm, d), lambda i: (i, 0)),
        out_shape=jax.ShapeDtypeStruct((t, d), out_dtype),
        compiler_params=_params("parallel"),
        name="norm_mod",
    )(x, w, sc, sh)


def _gelu_exact(x):
    return 0.5 * x * (1.0 + lax.erf(x * (2.0 ** -0.5)))


def _mm_kernel(a_ref, w_ref, *refs, mode):
    *refs, wb = refs
    @pl.when(pl.program_id(1) == 0)
    def _():
        wb[...] = w_ref[...].astype(BF16)

    acc = jnp.dot(a_ref[...], wb[...], preferred_element_type=F32)
    if mode == "plain":
        (o_ref,) = refs
        o_ref[...] = acc.astype(o_ref.dtype)
    elif mode == "bias_gelu":
        b_ref, o_ref = refs
        o_ref[...] = _gelu_exact(acc + b_ref[...]).astype(o_ref.dtype)
    else:
        b_ref, res_ref, gate_ref, o_ref = refs
        o_ref[...] = res_ref[...] + gate_ref[...] * (acc + b_ref[...])


def matmul(a, w, n, *, mode, out_dtype, tm, tn, layer=None, bias=None, res=None, gate=None):
    m, k = a.shape
    assert m % tm == 0 and n % tn == 0 and w.shape[-1] >= n
    row = pl.BlockSpec((1, tn), lambda j, i: (0, j))
    if layer is None:
        w_spec = pl.BlockSpec((k, tn), lambda j, i: (0, j))
    else:
        w_spec = pl.BlockSpec((None, k, tn), lambda j, i: (layer, 0, j))
    in_specs = [pl.BlockSpec((tm, k), lambda j, i: (i, 0)), w_spec]
    args = [a, w]
    if mode != "plain":
        in_specs.append(row)
        args.append(bias)
    if mode == "residual":
        in_specs += [pl.BlockSpec((tm, tn), lambda j, i: (i, j)), row]
        args += [res, gate]
    return pl.pallas_call(
        functools.partial(_mm_kernel, mode=mode),
        grid=(n // tn, m // tm),
        in_specs=in_specs,
        out_specs=pl.BlockSpec((tm, tn), lambda j, i: (i, j)),
        out_shape=jax.ShapeDtypeStruct((m, n), out_dtype),
        scratch_shapes=[pltpu.VMEM((k, tn), BF16)],
        compiler_params=_params("parallel", "arbitrary"),
        name="matmul_" + mode,
    )(*args)


def _softplus(x):
    return jnp.maximum(x, 0.0) + jnp.log1p(jnp.exp(-jnp.abs(x)))


def _unit_lower_inverses(lows, n_block):
    n = lows[0].shape[0]
    r = _iota((n, n), 0)
    c = _iota((n, n), 1)
    eye = jnp.where(r == c, 1.0, 0.0).astype(F32)

    def mm(a, b):
        return jnp.dot(a, b, preferred_element_type=F32)

    def bf(ms):
        return [m.astype(BF16) for m in ms]

    base = (r // 8) == (c // 8)
    lows_b = bf(lows)
    ds = [jnp.where(base, low, 0.0) for low in lows]
    ds_b = [jnp.where(base, low, jnp.zeros_like(low)) for low in lows_b]
    xs = [eye - d for d in ds]
    ps_b = bf([mm(d, d) for d in ds_b])
    xs = [x + mm(xb, p) for x, xb, p in zip(xs, bf(xs), ps_b)]
    ps_b = bf([mm(p, p) for p in ps_b])
    xs = [x + mm(xb, p) for x, xb, p in zip(xs, bf(xs), ps_b)]
    s = 8
    while s < n_block:
        pair = ((r // (2 * s)) == (c // (2 * s))) & ((r // s) != (c // s))
        xs_b = bf(xs)
        ts_b = bf([mm(jnp.where(pair, low, jnp.zeros_like(low)), xb) for low, xb in zip(lows_b, xs_b)])
        xs = [x - mm(xb, t) for x, xb, t in zip(xs, xs_b, ts_b)]
        s *= 2
    return xs


def _gdn_prep_kernel(q_ref, k_ref, v_ref, qh_ref, kh_ref, vh_ref, wq_ref, wk_ref, wv_ref, ab_ref, alog_ref, dtb_ref,
                     qg_ref, kd_ref, u_ref, w_ref, a_ref, gcum_ref,
                     win_q, win_k, win_v, gcum_s, gtot_s, beta_s):
    i = pl.program_id(0)
    h = pl.program_id(1)
    tm = q_ref.shape[0]
    halo = qh_ref.shape[0]
    cs = DN_CHUNK
    hd = DN_HEAD_DIM

    def conv_silu(x_ref, halo_ref, w_ref, win_ref):
        hal = halo_ref[...].astype(F32)
        win_ref[0:halo, :] = jnp.where(i == 0, jnp.zeros_like(hal), hal)
        win_ref[halo:halo + tm, :] = x_ref[...].astype(F32)
        w = w_ref[...]
        acc = None
        for j in range(DN_CONV):
            start = halo - (DN_CONV - 1) + j
            term = win_ref[start:start + tm, :] * w[j:j + 1, :]
            acc = term if acc is None else acc + term
        return _silu(acc)

    def l2norm(x):
        return x * lax.rsqrt(jnp.sum(x * x, axis=-1, keepdims=True) + NORM_EPS)

    @pl.when(h == 0)
    def _():
        ab = ab_ref[...]
        g = -jnp.exp(alog_ref[...]) * _softplus(ab + dtb_ref[...])
        sub = 4 * DN_CHUNK
        r = _iota((sub, sub), 0)
        c = _iota((sub, sub), 1)
        same = (r // DN_CHUNK) == (c // DN_CHUNK)
        tri = jnp.where(same & (c <= r), 1.0, 0.0).astype(F32)
        blk = jnp.where(same, 1.0, 0.0).astype(F32)
        for s0 in range(0, tm, sub):
            g_sub = g[s0:s0 + sub]
            gcum = jnp.dot(tri, g_sub, precision=HIGHEST, preferred_element_type=F32)
            gcum_s[s0:s0 + sub, :] = gcum
            gcum_ref[s0:s0 + sub, :] = gcum
            gtot_s[s0:s0 + sub, :] = jnp.dot(blk, g_sub, precision=HIGHEST, preferred_element_type=F32)
        beta_s[...] = _sigmoid(ab)

    q = l2norm(conv_silu(q_ref, qh_ref, wq_ref, win_q)) * (DN_HEAD_DIM ** -0.5)
    k = l2norm(conv_silu(k_ref, kh_ref, wk_ref, win_k))
    v = conv_silu(v_ref, vh_ref, wv_ref, win_v)

    def head_gates(ref, first_lane):
        rolled = pltpu.roll(ref[...], (LANES - first_lane) % LANES, 1)
        return [jnp.broadcast_to(rolled[:, s:s + 1], (tm, hd)) for s in range(2)]

    gc = head_gates(gcum_s, 2 * h)
    gt = head_gates(gtot_s, 2 * h)
    be = head_gates(beta_s, DN_V_HEADS + 2 * h)
    vb, kbg = [], []
    for s in range(2):
        cols = slice(s * hd, (s + 1) * hd)
        eg = jnp.exp(gc[s])
        qg_ref[:, cols] = (q * eg).astype(BF16)
        kd_ref[:, cols] = (k * jnp.exp(gt[s] - gc[s])).astype(BF16)
        vb.append((v[:, cols] * be[s]).astype(BF16))
        kbg.append((k * be[s] * eg).astype(BF16))
    qb = q.astype(BF16)
    kb = k.astype(BF16)

    n2 = 2 * cs
    r = _iota((n2, n2), 0)
    c = _iota((n2, n2), 1)
    same_head = (r // cs) == (c // cs)
    causal = same_head & (c <= r)
    strict = same_head & (c < r)
    top = _iota((cs, n2), 1) < cs
    nt = (((1,), (1,)), ((), ()))
    chunks = [slice(ci * cs, (ci + 1) * cs) for ci in range(tm // cs)]

    def stack_heads(x, rows):
        return jnp.concatenate([x[0][rows], x[1][rows]], axis=0)

    k_st = [jnp.concatenate([kb[rows], kb[rows]], axis=0) for rows in chunks]
    q_st = [jnp.concatenate([qb[rows], qb[rows]], axis=0) for rows in chunks]
    kk = [lax.dot_general(ks, ks, nt, preferred_element_type=F32) for ks in k_st]
    qk = [lax.dot_general(qs, ks, nt, preferred_element_type=F32) for qs, ks in zip(q_st, k_st)]
    g_col = [stack_heads(gc, rows) for rows in chunks]
    b_col = [stack_heads(be, rows) for rows in chunks]
    decay = [jnp.exp(jnp.where(causal, g - g.T, 0.0)) for g in g_col]
    lows = [jnp.where(strict, b * kk_i * d, 0.0) for b, kk_i, d in zip(b_col, kk, decay)]
    tinvs = [t.astype(BF16) for t in _unit_lower_inverses(lows, cs)]
    us = [jnp.dot(t, stack_heads(vb, rows), preferred_element_type=F32).astype(BF16) for t, rows in zip(tinvs, chunks)]
    ws = [jnp.dot(t, stack_heads(kbg, rows), preferred_element_type=F32).astype(BF16) for t, rows in zip(tinvs, chunks)]
    for rows, u2, w2, qk_i, d in zip(chunks, us, ws, qk, decay):
        intra = jnp.where(causal, qk_i * d, 0.0)
        a_ref[rows, :] = jnp.where(top, intra[:cs], intra[cs:]).astype(BF16)
        u_ref[rows, :hd] = u2[:cs]
        u_ref[rows, hd:] = u2[cs:]
        w_ref[rows, :hd] = w2[:cs]
        w_ref[rows, hd:] = w2[cs:]


def gdn_prep(proj, ab, conv_w, a_log, dt_bias):
    t = proj.shape[0]
    tm = GDN_TILE
    halo = 16
    hb = tm // halo
    hd = DN_HEAD_DIM
    nq = DN_QK_HEADS

    def halo_map(off):
        return lambda i, h: (jnp.maximum(i * hb - 1, 0), off + h)

    pad = jnp.zeros((1, LANES - DN_V_HEADS), F32)
    alog = jnp.concatenate([a_log.reshape(1, -1), pad], axis=1)
    dtb = jnp.concatenate([dt_bias.reshape(1, -1), pad], axis=1)
    big = jax.ShapeDtypeStruct((t, DN_V_DIM), BF16)
    small = jax.ShapeDtypeStruct((t, DN_QK_DIM), BF16)
    gshape = jax.ShapeDtypeStruct((t, LANES), F32)
    big_spec = pl.BlockSpec((tm, 2 * hd), lambda i, h: (i, h))
    small_spec = pl.BlockSpec((tm, hd), lambda i, h: (i, h))
    g_spec = pl.BlockSpec((tm, LANES), lambda i, h: (i, 0))
    row = pl.BlockSpec((1, LANES), lambda i, h: (0, 0))
    return pl.pallas_call(
        _gdn_prep_kernel,
        grid=(t // tm, nq),
        in_specs=[
            pl.BlockSpec((tm, hd), lambda i, h: (i, h)),
            pl.BlockSpec((tm, hd), lambda i, h: (i, nq + h)),
            pl.BlockSpec((tm, 2 * hd), lambda i, h: (i, nq + h)),
            pl.BlockSpec((halo, hd), halo_map(0)),
            pl.BlockSpec((halo, hd), halo_map(nq)),
            pl.BlockSpec((halo, 2 * hd), lambda i, h: (jnp.maximum(i * hb - 1, 0), nq + h)),
            pl.BlockSpec((DN_CONV, hd), lambda i, h: (0, h)),
            pl.BlockSpec((DN_CONV, hd), lambda i, h: (0, nq + h)),
            pl.BlockSpec((DN_CONV, 2 * hd), lambda i, h: (0, nq + h)),
            g_spec, row, row,
        ],
        out_specs=[big_spec, big_spec, big_spec, big_spec, small_spec, g_spec],
        out_shape=[big, big, big, big, small, gshape],
        scratch_shapes=[
            pltpu.VMEM((tm + halo, hd), F32), pltpu.VMEM((tm + halo, hd), F32), pltpu.VMEM((tm + halo, 2 * hd), F32),
            pltpu.VMEM((tm, LANES), F32), pltpu.VMEM((tm, LANES), F32), pltpu.VMEM((tm, LANES), F32),
        ],
        compiler_params=_params("parallel", "arbitrary"),
        name="gdn_prep",
    )(proj, proj, proj, proj, proj, proj, conv_w, conv_w, conv_w, ab, alog, dtb)


def _gdn_scan_kernel(glast_ref, qg_ref, kd_ref, u_ref, w_ref, a_ref, z_ref, onw_ref, og_ref, s_ref):
    ci = pl.program_id(0)
    cs = DN_CHUNK
    hd = DN_HEAD_DIM
    nh = DN_V_HEADS

    @pl.when(ci == 0)
    def _():
        s_ref[...] = jnp.zeros_like(s_ref)

    onw = onw_ref[...]
    tn = (((0,), (0,)), ((), ()))
    lane = _iota((cs, 2 * cs), 1)
    cols = [slice(hv * hd, (hv + 1) * hd) for hv in range(nh)]
    states = [s_ref[hv] for hv in range(nh)]
    ws = [jnp.dot(jnp.concatenate([w_ref[:, cols[hv]], qg_ref[:, cols[hv]]], axis=0), states[hv].astype(BF16),
                  preferred_element_type=F32) for hv in range(nh)]
    v_new = [(u_ref[:, cols[hv]].astype(F32) - ws[hv][:cs]).astype(BF16) for hv in range(nh)]
    for hv in range(nh):
        decay = jnp.exp(jnp.full((1, hd), glast_ref[ci, hv], F32))
        s_ref[hv] = states[hv] * decay + lax.dot_general(kd_ref[:, cols[hv]], v_new[hv], tn, preferred_element_type=F32)
    for pair in range(nh // 2):
        a2 = a_ref[:, pair * 2 * cs:(pair + 1) * 2 * cs]
        v2 = jnp.concatenate([v_new[2 * pair], v_new[2 * pair + 1]], axis=0)
        for s in range(2):
            hv = 2 * pair + s
            a_s = jnp.where((lane // cs) == s, a2, jnp.zeros_like(a2))
            o = ws[hv][cs:] + jnp.dot(a_s, v2, preferred_element_type=F32)
            on = o * lax.rsqrt(jnp.mean(o * o, axis=-1, keepdims=True) + NORM_EPS) * onw
            og_ref[:, cols[hv]] = (on * _silu(z_ref[:, cols[hv]].astype(F32))).astype(BF16)


def gdn_scan(glast, qg, kd, u, w, intra, proj, o_norm_w):
    t = qg.shape[0]
    cs = DN_CHUNK
    big_spec = pl.BlockSpec((cs, DN_V_DIM), lambda c, g: (c, 0))
    return pl.pallas_call(
        _gdn_scan_kernel,
        grid_spec=pltpu.PrefetchScalarGridSpec(
            num_scalar_prefetch=1,
            grid=(t // cs,),
            in_specs=[big_spec, big_spec, big_spec, big_spec,
                      pl.BlockSpec((cs, DN_V_HEADS * cs), lambda c, g: (c, 0)),
                      pl.BlockSpec((cs, DN_V_DIM), lambda c, g: (c, DN_CONV_CH // DN_V_DIM)),
                      pl.BlockSpec((1, DN_HEAD_DIM), lambda c, g: (0, 0))],
            out_specs=big_spec,
            scratch_shapes=[pltpu.VMEM((DN_V_HEADS, DN_HEAD_DIM, DN_HEAD_DIM), F32)],
        ),
        out_shape=jax.ShapeDtypeStruct((t, DN_V_DIM), BF16),
        compiler_params=_params("arbitrary"),
        name="gdn_scan",
    )(glast, qg, kd, u, w, intra, proj, o_norm_w.reshape(1, -1))


def gated_deltanet_block(x, h, gate, layer, w_in, conv_w, a_log, dt_bias, o_norm_w, w_out):
    t = x.shape[0]
    n_main = DN_CONV_CH + DN_V_DIM
    w_ab = jnp.pad(w_in[layer, :, n_main:], ((0, 0), (0, LANES - 2 * DN_V_HEADS)))
    proj = matmul(h, w_in, n_main, layer=layer, mode="plain", out_dtype=BF16, tm=1024, tn=1024)
    ab = matmul(h, w_ab, LANES, mode="plain", out_dtype=F32, tm=1024, tn=LANES)
    qg, kd, u, w, intra, gcum = gdn_prep(proj, ab, conv_w, a_log, dt_bias)
    glast = gcum.reshape(t // DN_CHUNK, DN_CHUNK, LANES)[:, DN_CHUNK - 1, :DN_V_HEADS]
    og = gdn_scan(glast, qg, kd, u, w, intra, proj, o_norm_w)
    zero_bias = jnp.zeros((1, D_MODEL), F32)
    return matmul(og, w_out, D_MODEL, layer=layer, mode="residual", out_dtype=F32, tm=1024, tn=512,
                  bias=zero_bias, res=x, gate=gate)


def _sgu_spatial_kernel(u_ref, v_ref, lnw_ref, lnb_ref, wsp_ref, bsp_ref, o_ref):
    cs = SGU_CHUNK
    gd = SGU_WIDTH // SGU_GROUPS
    v = v_ref[...].astype(F32)
    mu = jnp.mean(v, axis=-1, keepdims=True)
    var = jnp.mean(jnp.square(v - mu), axis=-1, keepdims=True)
    vn = ((v - mu) * lax.rsqrt(var + NORM_EPS) * lnw_ref[...] + lnb_ref[...]).astype(BF16)
    r = _iota((cs, cs), 0)
    c = _iota((cs, cs), 1)
    bsp = bsp_ref[...]
    for g in range(SGU_GROUPS):
        cols = slice(g * gd, (g + 1) * gd)
        wg = jnp.where(c <= r, wsp_ref[g], 0.0).astype(BF16)
        sp = jnp.dot(wg, vn[:, cols], preferred_element_type=F32) + bsp[:, g:g + 1]
        o_ref[:, cols] = (u_ref[:, cols].astype(F32) * sp).astype(BF16)


def sgu_spatial(zz, ln_w, ln_b, w_sp, b_sp):
    t = zz.shape[0]
    cs = SGU_CHUNK
    wd = SGU_WIDTH
    bsp_t = jnp.pad(b_sp.T, ((0, 0), (0, LANES - SGU_GROUPS)))
    row = pl.BlockSpec((1, wd), lambda i: (0, 0))
    return pl.pallas_call(
        _sgu_spatial_kernel,
        grid=(t // cs,),
        in_specs=[
            pl.BlockSpec((cs, wd), lambda i: (i, 0)),
            pl.BlockSpec((cs, wd), lambda i: (i, 1)),
            row, row,
            pl.BlockSpec((SGU_GROUPS, cs, cs), lambda i: (0, 0, 0)),
            pl.BlockSpec((cs, LANES), lambda i: (0, 0)),
        ],
        out_specs=pl.BlockSpec((cs, wd), lambda i: (i, 0)),
        out_shape=jax.ShapeDtypeStruct((t, wd), BF16),
        compiler_params=_params("parallel"),
        name="sgu_spatial",
    )(zz, zz, ln_w.reshape(1, -1), ln_b.reshape(1, -1), w_sp, bsp_t)


def chunked_gmlp_block(x, h, gate, layer, w_in, b_in, ln_w, ln_b, w_sp, b_sp, w_out, b_out):
    zz = matmul(h, w_in, 2 * SGU_WIDTH, layer=layer, mode="bias_gelu", out_dtype=BF16, tm=1024, tn=1024,
                bias=b_in.reshape(1, -1))
    su = sgu_spatial(zz, ln_w, ln_b, w_sp, b_sp)
    return matmul(su, w_out, D_MODEL, layer=layer, mode="residual", out_dtype=F32, tm=1024, tn=512,
                  bias=b_out.reshape(1, -1), res=x, gate=gate)


def _dot_split3(a, b):
    a_hi = a.astype(BF16)
    b_hi = b.astype(BF16)
    a_lo = (a - a_hi.astype(F32)).astype(BF16)
    b_lo = (b - b_hi.astype(F32)).astype(BF16)
    return (jnp.dot(a_hi, b_hi, preferred_element_type=F32)
            + (jnp.dot(a_lo, b_hi, preferred_element_type=F32) + jnp.dot(a_hi, b_lo, preferred_element_type=F32)))


def _router_kernel(x_ref, w_ref, sc_ref, sh_ref, wr_ref, br_ref, h_ref, idx_ref, gate_ref, pos_ref, cnt_ref, carry):
    i = pl.program_id(0)
    tm = x_ref.shape[0]

    @pl.when(i == 0)
    def _():
        carry[...] = jnp.zeros_like(carry)

    h = _norm_mod(x_ref[...], w_ref[...], sc_ref[...], sh_ref[...])
    h_ref[...] = h
    lane = _iota((tm, LANES), 1).astype(F32)
    neg = jnp.float32(-jnp.inf)
    logits = _dot_split3(h, wr_ref[...]) + br_ref[...]
    logits = jnp.where(lane < N_EXPERTS, logits, neg)
    vals, idxs = [], []
    for _ in range(TOP_K):
        m = jnp.max(logits, axis=-1, keepdims=True)
        ix = jnp.min(jnp.where(logits == m, lane, float(LANES)), axis=-1, keepdims=True)
        vals.append(m)
        idxs.append(ix)
        logits = jnp.where(lane == ix, neg, logits)
    es = [jnp.exp(v - vals[0]) for v in vals]
    denom = es[0] + es[1] + es[2] + es[3]
    multi = jnp.zeros((tm, LANES), F32)
    for ix in idxs:
        multi = jnp.where(lane == ix, 1.0, multi)
    r = _iota((tm, tm), 0)
    c = _iota((tm, tm), 1)
    before = jnp.where(c < r, 1.0, 0.0).astype(BF16)
    rank = jnp.dot(before, multi.astype(BF16), preferred_element_type=F32) + carry[0:1, :]
    idx_t = jnp.zeros((tm, LANES), F32)
    gate_t = jnp.zeros((tm, LANES), F32)
    pos_t = jnp.zeros((tm, LANES), F32)
    for kk in range(TOP_K):
        pk = jnp.sum(jnp.where(lane == idxs[kk], rank, 0.0), axis=-1, keepdims=True)
        idx_t = jnp.where(lane == kk, idxs[kk], idx_t)
        gate_t = jnp.where(lane == kk, es[kk] / denom, gate_t)
        pos_t = jnp.where(lane == kk, pk, pos_t)
    idx_ref[...] = idx_t.astype(I32)
    gate_ref[...] = gate_t
    pos_ref[...] = pos_t.astype(I32)
    carry[...] = carry[...] + jnp.sum(multi, axis=0, keepdims=True)
    cnt_ref[...] = carry[...]


def moe_router(x, w, sc, sh, w_router, b_router):
    t, d = x.shape
    tm = 256
    wr = jnp.pad(w_router, ((0, 0), (0, LANES - N_EXPERTS)))
    br = jnp.pad(b_router.reshape(1, -1), ((0, 0), (0, LANES - N_EXPERTS)))
    vec = pl.BlockSpec((1, d), lambda i: (0, 0))
    tile = pl.BlockSpec((tm, LANES), lambda i: (i, 0))
    return pl.pallas_call(
        _router_kernel,
        grid=(t // tm,),
        in_specs=[pl.BlockSpec((tm, d), lambda i: (i, 0)), vec, vec, vec,
                  pl.BlockSpec((d, LANES), lambda i: (0, 0)), pl.BlockSpec((1, LANES), lambda i: (0, 0))],
        out_specs=[pl.BlockSpec((tm, d), lambda i: (i, 0)), tile, tile, tile, pl.BlockSpec((8, LANES), lambda i: (0, 0))],
        out_shape=[jax.ShapeDtypeStruct((t, d), F32), jax.ShapeDtypeStruct((t, LANES), I32),
                   jax.ShapeDtypeStruct((t, LANES), F32), jax.ShapeDtypeStruct((t, LANES), I32),
                   jax.ShapeDtypeStruct((8, LANES), F32)],
        scratch_shapes=[pltpu.VMEM((8, LANES), F32)],
        compiler_params=_params("arbitrary"),
        name="moe_router",
    )(x, w, sc, sh, wr, br)


def _dispatch_kernel(dest_ref, ztile_ref, h_ref, xs_ref, zbuf, sem, zsem):
    i = pl.program_id(0)
    tm = h_ref.shape[0]
    zt = zbuf.shape[0]

    @pl.when(i == 0)
    def _():
        zbuf[...] = jnp.zeros_like(zbuf)

        def zero_copy(e):
            row = pl.multiple_of(ztile_ref[e], zt)
            return pltpu.make_async_copy(zbuf, xs_ref.at[pl.ds(row, zt), :], zsem)

        for e in range(N_EXPERTS):
            @pl.when(ztile_ref[e] >= 0)
            def _():
                zero_copy(e).start()
        for e in range(N_EXPERTS):
            @pl.when(ztile_ref[e] >= 0)
            def _():
                zero_copy(e).wait()

        def zero_tail(j, carry):
            row = pl.multiple_of(j * zt, zt)
            cp = pltpu.make_async_copy(zbuf, xs_ref.at[pl.ds(row, zt), :], zsem)
            cp.start()
            cp.wait()
            return carry

        lax.fori_loop(ztile_ref[N_EXPERTS], xs_ref.shape[0] // zt, zero_tail, 0)

    def row_copy(r, d):
        return pltpu.make_async_copy(h_ref.at[pl.ds(r, 1), :], xs_ref.at[pl.ds(d, 1), :], sem)

    def issue(r, carry):
        base = (i * tm + r) * TOP_K
        for kk in range(TOP_K):
            row_copy(r, dest_ref[base + kk]).start(priority=kk % 2)
        return carry

    lax.fori_loop(0, tm, issue, 0, unroll=8)

    for kk in range(TOP_K):
        pltpu.make_async_copy(h_ref, xs_ref.at[pl.ds(0, tm), :], sem).wait()


def moe_dispatch(dest, zero_plan, h, n_rows):
    t, d = h.shape
    tm = 256
    return pl.pallas_call(
        _dispatch_kernel,
        grid_spec=pltpu.PrefetchScalarGridSpec(
            num_scalar_prefetch=2,
            grid=(t // tm,),
            in_specs=[pl.BlockSpec((tm, d), lambda i, dst, zt: (i, 0))],
            out_specs=pl.BlockSpec(memory_space=pl.ANY),
            scratch_shapes=[pltpu.VMEM((MOE_TILE, d), F32), pltpu.SemaphoreType.DMA(()), pltpu.SemaphoreType.DMA(())],
        ),
        out_shape=jax.ShapeDtypeStruct((n_rows, d), F32),
        compiler_params=_params("arbitrary"),
        name="moe_dispatch",
    )(dest.reshape(-1), zero_plan, h)


def _stream_expert_weights(w_hbm, wbuf, stage, sem, layer, e_cur, e_next, tig, gprev, cur, phase=0, phases=1):
    n_pairs = wbuf.shape[1] // 2
    cw = stage.shape[-1]
    prev_calls = gprev * phases
    q = tig * phases + phase

    def chunk_copy(e, c):
        return pltpu.make_async_copy(w_hbm.at[layer, e, :, pl.ds(c * cw, cw)], stage.at[c % 2], sem.at[c % 2])

    def start_pair(e, p):
        chunk_copy(e, 2 * p).start()
        chunk_copy(e, 2 * p + 1).start()

    def retire_pair(e, p, slot):
        for c in (2 * p, 2 * p + 1):
            chunk_copy(e, c).wait()
            wbuf[slot, c] = stage[c % 2].astype(BF16)

    if phase == 0:
        @pl.when(tig == 0)
        def _():
            for p in range(n_pairs):
                @pl.when(prev_calls <= p)
                def _():
                    start_pair(e_cur, p)

                @pl.when(prev_calls <= p + 1)
                def _():
                    retire_pair(e_cur, p, cur)

    @pl.when(e_next >= 0)
    def _():
        for j in range(phase, n_pairs + 1, phases):
            @pl.when(q == j)
            def _():
                if j >= 1:
                    retire_pair(e_next, j - 1, 1 - cur)
                if j < n_pairs:
                    start_pair(e_next, j)


def _for_row_bucket(n_blocks, tm, body):
    for k in range(1, tm // ROW_BLOCK + 1):
        @pl.when(n_blocks == k)
        def _():
            body(k * ROW_BLOCK)


def _ffn_up_kernel(te_ref, tig_ref, gprev_ref, nxt_ref, gidx_ref, nb_ref, nu_ref, x_ref, w_hbm, b_ref, o_ref,
                   wbuf, stage, sem, *, layer):
    i = pl.program_id(0)
    tm, f = o_ref.shape
    nc = wbuf.shape[1] // 2
    fc = wbuf.shape[-1]

    @pl.when(i < nu_ref[0])
    def _():
        cur = gidx_ref[i] % 2
        _stream_expert_weights(w_hbm, wbuf, stage, sem, layer, te_ref[i], nxt_ref[i], tig_ref[i], gprev_ref[i], cur)

        def compute(m):
            xb = x_ref[:m, :].astype(BF16)
            for c in range(nc):
                g = jnp.dot(xb, wbuf[cur, c], preferred_element_type=F32) + b_ref[:, c * fc:(c + 1) * fc]
                u = jnp.dot(xb, wbuf[cur, nc + c], preferred_element_type=F32) + b_ref[:, f + c * fc:f + (c + 1) * fc]
                x_glu = jnp.minimum(g, SWIGLU_LIMIT)
                x_lin = jnp.clip(u, -SWIGLU_LIMIT, SWIGLU_LIMIT)
                act = x_glu * _sigmoid(SWIGLU_ALPHA * x_glu) * (x_lin + 1.0)
                o_ref[:m, c * fc:(c + 1) * fc] = act.astype(BF16)
            if m < tm:
                o_ref[m:, :] = jnp.zeros((tm - m, f), BF16)

        _for_row_bucket(nb_ref[i], tm, compute)

    @pl.when(i >= nu_ref[0])
    def _():
        o_ref[...] = jnp.zeros_like(o_ref)


def _ffn_down_kernel(te_ref, tig_ref, gprev_ref, nxt_ref, gidx_ref, nb_ref, nu_ref, a_ref, w_hbm, b_ref, o_ref,
                     wbuf, stage, sem, *, layer):
    i = pl.program_id(0)
    tm, d = o_ref.shape
    fc = wbuf.shape[-1]

    @pl.when(i < nu_ref[0])
    def _():
        cur = gidx_ref[i] % 2
        _stream_expert_weights(w_hbm, wbuf, stage, sem, layer, te_ref[i], nxt_ref[i], tig_ref[i], gprev_ref[i], cur)

        def compute(m):
            a = a_ref[:m, :]
            for c in range(wbuf.shape[1]):
                cols = slice(c * fc, (c + 1) * fc)
                o_ref[:m, cols] = jnp.dot(a, wbuf[cur, c], preferred_element_type=F32) + b_ref[:, cols]
            if m < tm:
                o_ref[m:, :] = jnp.zeros((tm - m, d), F32)

        _for_row_bucket(nb_ref[i], tm, compute)

    @pl.when(i >= nu_ref[0])
    def _():
        o_ref[...] = jnp.zeros_like(o_ref)


def moe_experts(plan, xs, layer, w_gate_up, b_gate_up, w_down, b_down):
    n_rows, d = xs.shape
    tm = MOE_TILE
    n_tiles = n_rows // tm
    f = w_down.shape[2]
    cw = f // 4
    n_plan = len(plan)
    expert_row = lambda i, te, *_: (te[i], 0, 0)
    any_spec = pl.BlockSpec(memory_space=pl.ANY)
    act = pl.pallas_call(
        functools.partial(_ffn_up_kernel, layer=layer),
        grid_spec=pltpu.PrefetchScalarGridSpec(
            num_scalar_prefetch=n_plan,
            grid=(n_tiles,),
            in_specs=[pl.BlockSpec((tm, d), lambda i, *p: (jnp.minimum(i, p[-1][0] - 1), 0)),
                      any_spec,
                      pl.BlockSpec((None, 1, 2 * f), expert_row)],
            out_specs=pl.BlockSpec((tm, f), lambda i, *p: (i, 0)),
            scratch_shapes=[pltpu.VMEM((2, 2 * f // cw, d, cw), BF16), pltpu.VMEM((2, d, cw), F32),
                            pltpu.SemaphoreType.DMA((2,))],
        ),
        out_shape=jax.ShapeDtypeStruct((n_rows, f), BF16),
        compiler_params=_params("arbitrary"),
        name="moe_ffn_up",
    )(*plan, xs, w_gate_up, b_gate_up.reshape(N_EXPERTS, 1, 2 * f))
    return pl.pallas_call(
        functools.partial(_ffn_down_kernel, layer=layer),
        grid_spec=pltpu.PrefetchScalarGridSpec(
            num_scalar_prefetch=n_plan,
            grid=(n_tiles,),
            in_specs=[pl.BlockSpec((tm, f), lambda i, *p: (i, 0)),
                      any_spec,
                      pl.BlockSpec((None, 1, d), expert_row)],
            out_specs=pl.BlockSpec((tm, d), lambda i, *p: (i, 0)),
            scratch_shapes=[pltpu.VMEM((2, d // cw, f, cw), BF16), pltpu.VMEM((2, f, cw), F32),
                            pltpu.SemaphoreType.DMA((2,))],
        ),
        out_shape=jax.ShapeDtypeStruct((n_rows, d), F32),
        compiler_params=_params("arbitrary"),
        name="moe_ffn_down",
    )(*plan, act, w_down, b_down.reshape(N_EXPERTS, 1, d))


def _combine_kernel(dest_ref, x_ref, gate_ref, g2_ref, nw_ref, nsc_ref, nsh_ref, y_ref, *refs, emit_x):
    if emit_x:
        o_ref, hn_ref, ybuf, sem = refs
    else:
        hn_ref, ybuf, sem = refs
    i = pl.program_id(0)
    n = pl.num_programs(0)
    tm = x_ref.shape[0]

    def row_copy(tile, slot, r, kk):
        d = dest_ref[(tile * tm + r) * TOP_K + kk]
        return pltpu.make_async_copy(y_ref.at[pl.ds(d, 1), :], ybuf.at[slot, kk, pl.ds(r, 1), :], sem.at[slot])

    def issue_tile(tile, slot):
        def body(r, carry):
            for kk in range(TOP_K):
                row_copy(tile, slot, r, kk).start(priority=kk % 2)
            return carry
        lax.fori_loop(0, tm, body, 0, unroll=8)

    def drain_tile(slot):
        for kk in range(TOP_K):
            pltpu.make_async_copy(y_ref.at[pl.ds(0, tm), :], ybuf.at[slot, kk], sem.at[slot]).wait()

    slot = i % 2

    @pl.when(i == 0)
    def _():
        issue_tile(0, 0)

    @pl.when(i + 1 < n)
    def _():
        issue_tile(i + 1, 1 - slot)

    drain_tile(slot)
    gates = gate_ref[...]
    acc = ybuf[slot, 0] * gates[:, 0:1]
    for kk in range(1, TOP_K):
        acc = acc + ybuf[slot, kk] * gates[:, kk:kk + 1]
    x_new = x_ref[...] + g2_ref[...] * acc
    if emit_x:
        o_ref[...] = x_new
    hn_ref[...] = _norm_mod(x_new, nw_ref[...], nsc_ref[...], nsh_ref[...]).astype(hn_ref.dtype)


def moe_combine(dest, x, gates, gate2, y, next_norm, next_dtype, emit_x):
    t, d = x.shape
    tm = 128
    tile = pl.BlockSpec((tm, d), lambda i, dst: (i, 0))
    vec = pl.BlockSpec((1, d), lambda i, dst: (0, 0))
    normed = jax.ShapeDtypeStruct((t, d), next_dtype)
    return pl.pallas_call(
        functools.partial(_combine_kernel, emit_x=emit_x),
        grid_spec=pltpu.PrefetchScalarGridSpec(
            num_scalar_prefetch=1,
            grid=(t // tm,),
            in_specs=[tile, pl.BlockSpec((tm, LANES), lambda i, dst: (i, 0)), vec, vec, vec, vec,
                      pl.BlockSpec(memory_space=pl.ANY)],
            out_specs=[tile, tile] if emit_x else tile,
            scratch_shapes=[pltpu.VMEM((2, TOP_K, tm, d), F32), pltpu.SemaphoreType.DMA((2,))],
        ),
        out_shape=[jax.ShapeDtypeStruct((t, d), F32), normed] if emit_x else normed,
        compiler_params=_params("arbitrary"),
        name="moe_combine",
    )(dest.reshape(-1), x, gates, gate2, *next_norm, y)


def moe_block(x, norm_w, sc, sh, gate2, layer, w_router, b_router, w_gate_up, b_gate_up, w_down, b_down,
              next_norm, next_dtype, emit_x):
    t, d = x.shape
    tm = MOE_TILE
    h, idx, gates, pos, counts = moe_router(x, norm_w, sc, sh, w_router, b_router)
    cnt = counts[0, :N_EXPERTS].astype(I32)
    padded = (cnt + tm - 1) // tm * tm
    pend = jnp.cumsum(padded)
    pstart = pend - padded
    n_tiles = (t * TOP_K + N_EXPERTS * (tm - 1) + tm - 1) // tm
    is_expert = idx[:, :TOP_K, None] == jnp.arange(N_EXPERTS, dtype=I32)
    dest = pos[:, :TOP_K] + jnp.sum(jnp.where(is_expert, pstart, 0), axis=-1)
    tile_start = jnp.arange(n_tiles, dtype=I32) * tm
    tile_expert = jnp.minimum(jnp.sum((pend[None, :] <= tile_start[:, None]).astype(I32), axis=1), N_EXPERTS - 1)
    n_used = (pend[-1:] // tm).astype(I32)
    zero_plan = jnp.concatenate([jnp.where(padded > 0, pend - tm, -1), n_used]).astype(I32)
    xs = moe_dispatch(dest, zero_plan, h, n_tiles * tm)
    experts = jnp.arange(N_EXPERTS, dtype=I32)
    group_tiles = padded // tm
    nonempty = group_tiles > 0
    later = (experts[None, :] > experts[:, None]) & nonempty[None, :]
    earlier = (experts[None, :] < experts[:, None]) & nonempty[None, :]
    next_e = jnp.min(jnp.where(later, experts[None, :], N_EXPERTS), axis=1)
    prev_e = jnp.max(jnp.where(earlier, experts[None, :], -1), axis=1)
    next_e = jnp.where(next_e < N_EXPERTS, next_e, -1)
    prev_tiles = jnp.where(prev_e >= 0, group_tiles[jnp.maximum(prev_e, 0)], 0)
    ordinal = jnp.cumsum(nonempty.astype(I32)) - nonempty.astype(I32)
    tile_in_group = jnp.arange(n_tiles, dtype=I32) - (pstart // tm)[tile_expert]
    rows_in_tile = jnp.clip(cnt[tile_expert] - tile_in_group * tm, 1, tm)
    row_blocks = (rows_in_tile + ROW_BLOCK - 1) // ROW_BLOCK
    plan = tuple(a.astype(I32) for a in (tile_expert, tile_in_group, prev_tiles[tile_expert], next_e[tile_expert],
                                         ordinal[tile_expert], row_blocks, n_used))
    y = moe_experts(plan, xs, layer, w_gate_up, b_gate_up, w_down, b_down)
    return moe_combine(dest, x, gates, gate2, y, next_norm, next_dtype, emit_x)


def kernel(x, c, ada_w, ada_b, norm_w, dn_w_in, dn_conv_w, dn_a_log, dn_dt_bias, dn_o_norm_w, dn_w_out, sgu_w_in, sgu_b_in, sgu_ln_w, sgu_ln_b, sgu_w_sp, sgu_b_sp, sgu_w_out, sgu_b_out, moe_w_router, moe_b_router, moe_w_gate_up, moe_b_gate_up, moe_w_down, moe_b_down, final_norm_w):
    bsz, seq, d = x.shape
    assert bsz == 1 and d == D_MODEL
    xt = x.reshape(seq, d)
    mod = ada_mod(c, ada_w, ada_b)
    mods = [[mod[i:i + 1, s * d:(s + 1) * d] for s in range(6)] for i in range(DEPTH)]
    zero = jnp.zeros((1, d), F32)
    h = norm_mod(xt, norm_w[0, 0:1], mods[0][1], mods[0][0], BF16)
    for i in range(DEPTH):
        sh1, sc1, gt1, sh2, sc2, gt2 = mods[i]
        last = i == DEPTH - 1
        next_norm = (final_norm_w.reshape(1, d), zero, zero) if last else (norm_w[i + 1, 0:1], mods[i + 1][1], mods[i + 1][0])
        j = i // 2
        if i % 2 == 0:
            xt = gated_deltanet_block(xt, h, gt1, j, dn_w_in, dn_conv_w[j], dn_a_log[j], dn_dt_bias[j],
                                      dn_o_norm_w[j], dn_w_out)
        else:
            xt = chunked_gmlp_block(xt, h, gt1, j, sgu_w_in, sgu_b_in[j], sgu_ln_w[j], sgu_ln_b[j], sgu_w_sp[j],
                                    sgu_b_sp[j], sgu_w_out, sgu_b_out[j])
        res = moe_block(xt, norm_w[i, 1:2], sc2, sh2, gt2, i, moe_w_router[i], moe_b_router[i],
                        moe_w_gate_up, moe_b_gate_up[i], moe_w_down, moe_b_down[i],
                        next_norm, F32 if last else BF16, not last)
        if last:
            return res.reshape(bsz, seq, d)
        xt, h = res
```

```python
import functools

import jax
import jax.numpy as jnp
from jax import lax
from jax.experimental import pallas as pl
from jax.experimental.pallas import tpu as pltpu

F32 = jnp.float32
BF16 = jnp.bfloat16
I32 = jnp.int32
HIGHEST = lax.Precision.HIGHEST

D_MODEL = 2048
DEPTH = 2
NORM_EPS = 1e-6
DN_HEAD_DIM = 128
DN_QK_HEADS = 16
DN_V_HEADS = 32
DN_QK_DIM = 2048
DN_V_DIM = 4096
DN_CONV_CH = 8192
DN_CONV = 4
DN_CHUNK = 64
SGU_WIDTH = 4096
SGU_CHUNK = 128
SGU_GROUPS = 32
N_EXPERTS = 32
TOP_K = 4
EXPERT_DIM = 2048
SWIGLU_LIMIT = 7.0
SWIGLU_ALPHA = 1.702

LANES = 128
VMEM_LIMIT = 58 * 1024 * 1024
MOE_TILE = 512
ROW_BLOCK = 128
ROW_SPLIT = 16
GDN_TILE = 2048


def _params(*sem):
    return pltpu.CompilerParams(dimension_semantics=sem, vmem_limit_bytes=VMEM_LIMIT)


def _iota(shape, dim):
    return lax.broadcasted_iota(I32, shape, dim)


def _sigmoid(x):
    return 1.0 / (1.0 + jnp.exp(-x))


def _silu(x):
    return x * _sigmoid(x)


def _ada_kernel(c_ref, w_ref, b_ref, o_ref):
    c = c_ref[...]
    o_ref[...] = jnp.dot(_silu(c), w_ref[...], precision=HIGHEST, preferred_element_type=F32) + b_ref[...]


def ada_mod(c, ada_w, ada_b):
    depth, d, n = ada_w.shape
    tn = 1024
    c8 = jnp.broadcast_to(c, (8, d))
    out = pl.pallas_call(
        _ada_kernel,
        grid=(depth, n // tn),
        in_specs=[
            pl.BlockSpec((8, d), lambda l, j: (0, 0)),
            pl.BlockSpec((None, d, tn), lambda l, j: (l, 0, j)),
            pl.BlockSpec((None, 1, tn), lambda l, j: (l, 0, j)),
        ],
        out_specs=pl.BlockSpec((None, 8, tn), lambda l, j: (l, 0, j)),
        out_shape=jax.ShapeDtypeStruct((depth, 8, n), F32),
        compiler_params=_params("parallel", "parallel"),
        name="ada_mod",
    )(c8, ada_w, ada_b.reshape(depth, 1, n))
    return out[:, 0, :]


def _norm_mod(x, w, sc, sh):
    y = x * lax.rsqrt(jnp.mean(x * x, axis=-1, keepdims=True) + NORM_EPS)
    return (y * w) * (1.0 + sc) + sh


def _norm_mod_kernel(x_ref, w_ref, sc_ref, sh_ref, o_ref):
    o_ref[...] = _norm_mod(x_ref[...], w_ref[...], sc_ref[...], sh_ref[...]).astype(o_ref.dtype)


def norm_mod(x, w, sc, sh, out_dtype):
    t, d = x.shape
    tm = 512
    vec = pl.BlockSpec((1, d), lambda i: (0, 0))
    return pl.pallas_call(
        _norm_mod_kernel,
        grid=(t // tm,),
        in_specs=[pl.BlockSpec((tm, d), lambda i: (i, 0)), vec, vec, vec],
        out_specs=pl.BlockSpec((tm, d), lambda i: (i, 0)),
        out_shape=jax.ShapeDtypeStruct((t, d), out_dtype),
        compiler_params=_params("parallel"),
        name="norm_mod",
    )(x, w, sc, sh)


def _gelu_exact(x):
    return 0.5 * x * (1.0 + lax.erf(x * (2.0 ** -0.5)))


def _mm_kernel(a_ref, w_ref, *refs, mode):
    *refs, wb = refs
    @pl.when(pl.program_id(1) == 0)
    def _():
        wb[...] = w_ref[...].astype(BF16)

    acc = jnp.dot(a_ref[...], wb[...], preferred_element_type=F32)
    if mode == "plain":
        (o_ref,) = refs
        o_ref[...] = acc.astype(o_ref.dtype)
    elif mode == "bias_gelu":
        b_ref, o_ref = refs
        o_ref[...] = _gelu_exact(acc + b_ref[...]).astype(o_ref.dtype)
    else:
        b_ref, res_ref, gate_ref, o_ref = refs
        o_ref[...] = res_ref[...] + gate_ref[...] * (acc + b_ref[...])


def matmul(a, w, n, *, mode, out_dtype, tm, tn, layer=None, bias=None, res=None, gate=None):
    m, k = a.shape
    assert m % tm == 0 and n % tn == 0 and w.shape[-1] >= n
    row = pl.BlockSpec((1, tn), lambda j, i: (0, j))
    if layer is None:
        w_spec = pl.BlockSpec((k, tn), lambda j, i: (0, j))
    else:
        w_spec = pl.BlockSpec((None, k, tn), lambda j, i: (layer, 0, j))
    in_specs = [pl.BlockSpec((tm, k), lambda j, i: (i, 0)), w_spec]
    args = [a, w]
    if mode != "plain":
        in_specs.append(row)
        args.append(bias)
    if mode == "residual":
        in_specs += [pl.BlockSpec((tm, tn), lambda j, i: (i, j)), row]
        args += [res, gate]
    return pl.pallas_call(
        functools.partial(_mm_kernel, mode=mode),
        grid=(n // tn, m // tm),
        in_specs=in_specs,
        out_specs=pl.BlockSpec((tm, tn), lambda j, i: (i, j)),
        out_shape=jax.ShapeDtypeStruct((m, n), out_dtype),
        scratch_shapes=[pltpu.VMEM((k, tn), BF16)],
        compiler_params=_params("parallel", "arbitrary"),
        name="matmul_" + mode,
    )(*args)


def _softplus(x):
    return jnp.maximum(x, 0.0) + jnp.log1p(jnp.exp(-jnp.abs(x)))


def _unit_lower_inverses(lows, n_block):
    n = lows[0].shape[0]
    r = _iota((n, n), 0)
    c = _iota((n, n), 1)
    eye = jnp.where(r == c, 1.0, 0.0).astype(F32)

    def mm(a, b):
        return jnp.dot(a, b, preferred_element_type=F32)

    def bf(ms):
        return [m.astype(BF16) for m in ms]

    base = (r // 8) == (c // 8)
    lows_b = bf(lows)
    ds = [jnp.where(base, low, 0.0) for low in lows]
    ds_b = [jnp.where(base, low, jnp.zeros_like(low)) for low in lows_b]
    xs = [eye - d for d in ds]
    ps_b = bf([mm(d, d) for d in ds_b])
    xs = [x + mm(xb, p) for x, xb, p in zip(xs, bf(xs), ps_b)]
    ps_b = bf([mm(p, p) for p in ps_b])
    xs = [x + mm(xb, p) for x, xb, p in zip(xs, bf(xs), ps_b)]
    s = 8
    while s < n_block:
        pair = ((r // (2 * s)) == (c // (2 * s))) & ((r // s) != (c // s))
        xs_b = bf(xs)
        ts_b = bf([mm(jnp.where(pair, low, jnp.zeros_like(low)), xb) for low, xb in zip(lows_b, xs_b)])
        xs = [x - mm(xb, t) for x, xb, t in zip(xs, xs_b, ts_b)]
        s *= 2
    return xs


def _gdn_prep_kernel(q_ref, k_ref, v_ref, qh_ref, kh_ref, vh_ref, wq_ref, wk_ref, wv_ref, ab_ref, alog_ref, dtb_ref,
                     qg_ref, kd_ref, u_ref, w_ref, a_ref, gcum_ref,
                     win_q, win_k, win_v, gcum_s, gtot_s, beta_s):
    i = pl.program_id(0)
    h = pl.program_id(1)
    tm = q_ref.shape[0]
    halo = qh_ref.shape[0]
    cs = DN_CHUNK
    hd = DN_HEAD_DIM

    def conv_silu(x_ref, halo_ref, w_ref, win_ref):
        hal = halo_ref[...].astype(F32)
        win_ref[0:halo, :] = jnp.where(i == 0, jnp.zeros_like(hal), hal)
        win_ref[halo:halo + tm, :] = x_ref[...].astype(F32)
        w = w_ref[...]
        acc = None
        for j in range(DN_CONV):
            start = halo - (DN_CONV - 1) + j
            term = win_ref[start:start + tm, :] * w[j:j + 1, :]
            acc = term if acc is None else acc + term
        return _silu(acc)

    def l2norm(x):
        return x * lax.rsqrt(jnp.sum(x * x, axis=-1, keepdims=True) + NORM_EPS)

    @pl.when(h == 0)
    def _():
        ab = ab_ref[...]
        g = -jnp.exp(alog_ref[...]) * _softplus(ab + dtb_ref[...])
        sub = 4 * DN_CHUNK
        r = _iota((sub, sub), 0)
        c = _iota((sub, sub), 1)
        same = (r // DN_CHUNK) == (c // DN_CHUNK)
        tri = jnp.where(same & (c <= r), 1.0, 0.0).astype(F32)
        blk = jnp.where(same, 1.0, 0.0).astype(F32)
        for s0 in range(0, tm, sub):
            g_sub = g[s0:s0 + sub]
            gcum = jnp.dot(tri, g_sub, precision=HIGHEST, preferred_element_type=F32)
            gcum_s[s0:s0 + sub, :] = gcum
            gcum_ref[s0:s0 + sub, :] = gcum
            gtot_s[s0:s0 + sub, :] = jnp.dot(blk, g_sub, precision=HIGHEST, preferred_element_type=F32)
        beta_s[...] = _sigmoid(ab)

    q = l2norm(conv_silu(q_ref, qh_ref, wq_ref, win_q)) * (DN_HEAD_DIM ** -0.5)
    k = l2norm(conv_silu(k_ref, kh_ref, wk_ref, win_k))
    v = conv_silu(v_ref, vh_ref, wv_ref, win_v)

    def head_gates(ref, first_lane):
        rolled = pltpu.roll(ref[...], (LANES - first_lane) % LANES, 1)
        return [jnp.broadcast_to(rolled[:, s:s + 1], (tm, hd)) for s in range(2)]

    gc = head_gates(gcum_s, 2 * h)
    gt = head_gates(gtot_s, 2 * h)
    be = head_gates(beta_s, DN_V_HEADS + 2 * h)
    vb, kbg = [], []
    for s in range(2):
        cols = slice(s * hd, (s + 1) * hd)
        eg = jnp.exp(gc[s])
        qg_ref[:, cols] = (q * eg).astype(BF16)
        kd_ref[:, cols] = (k * jnp.exp(gt[s] - gc[s])).astype(BF16)
        vb.append((v[:, cols] * be[s]).astype(BF16))
        kbg.append((k * be[s] * eg).astype(BF16))
    qb = q.astype(BF16)
    kb = k.astype(BF16)

    n2 = 2 * cs
    r = _iota((n2, n2), 0)
    c = _iota((n2, n2), 1)
    same_head = (r // cs) == (c // cs)
    causal = same_head & (c <= r)
    strict = same_head & (c < r)
    top = _iota((cs, n2), 1) < cs
    nt = (((1,), (1,)), ((), ()))
    chunks = [slice(ci * cs, (ci + 1) * cs) for ci in range(tm // cs)]

    def stack_heads(x, rows):
        return jnp.concatenate([x[0][rows], x[1][rows]], axis=0)

    k_st = [jnp.concatenate([kb[rows], kb[rows]], axis=0) for rows in chunks]
    q_st = [jnp.concatenate([qb[rows], qb[rows]], axis=0) for rows in chunks]
    kk = [lax.dot_general(ks, ks, nt, preferred_element_type=F32) for ks in k_st]
    qk = [lax.dot_general(qs, ks, nt, preferred_element_type=F32) for qs, ks in zip(q_st, k_st)]
    g_col = [stack_heads(gc, rows) for rows in chunks]
    b_col = [stack_heads(be, rows) for rows in chunks]
    decay = [jnp.exp(jnp.where(causal, g - g.T, 0.0)) for g in g_col]
    lows = [jnp.where(strict, b * kk_i * d, 0.0) for b, kk_i, d in zip(b_col, kk, decay)]
    tinvs = [t.astype(BF16) for t in _unit_lower_inverses(lows, cs)]
    us = [jnp.dot(t, stack_heads(vb, rows), preferred_element_type=F32).astype(BF16) for t, rows in zip(tinvs, chunks)]
    ws = [jnp.dot(t, stack_heads(kbg, rows), preferred_element_type=F32).astype(BF16) for t, rows in zip(tinvs, chunks)]
    for rows, u2, w2, qk_i, d in zip(chunks, us, ws, qk, decay):
        intra = jnp.where(causal, qk_i * d, 0.0)
        a_ref[rows, :] = jnp.where(top, intra[:cs], intra[cs:]).astype(BF16)
        u_ref[rows, :hd] = u2[:cs]
        u_ref[rows, hd:] = u2[cs:]
        w_ref[rows, :hd] = w2[:cs]
        w_ref[rows, hd:] = w2[cs:]


def gdn_prep(proj, ab, conv_w, a_log, dt_bias):
    t = proj.shape[0]
    tm = GDN_TILE
    halo = 16
    hb = tm // halo
    hd = DN_HEAD_DIM
    nq = DN_QK_HEADS

    def halo_map(off):
        return lambda i, h: (jnp.maximum(i * hb - 1, 0), off + h)

    pad = jnp.zeros((1, LANES - DN_V_HEADS), F32)
    alog = jnp.concatenate([a_log.reshape(1, -1), pad], axis=1)
    dtb = jnp.concatenate([dt_bias.reshape(1, -1), pad], axis=1)
    big = jax.ShapeDtypeStruct((t, DN_V_DIM), BF16)
    small = jax.ShapeDtypeStruct((t, DN_QK_DIM), BF16)
    gshape = jax.ShapeDtypeStruct((t, LANES), F32)
    big_spec = pl.BlockSpec((tm, 2 * hd), lambda i, h: (i, h))
    small_spec = pl.BlockSpec((tm, hd), lambda i, h: (i, h))
    g_spec = pl.BlockSpec((tm, LANES), lambda i, h: (i, 0))
    row = pl.BlockSpec((1, LANES), lambda i, h: (0, 0))
    return pl.pallas_call(
        _gdn_prep_kernel,
        grid=(t // tm, nq),
        in_specs=[
            pl.BlockSpec((tm, hd), lambda i, h: (i, h)),
            pl.BlockSpec((tm, hd), lambda i, h: (i, nq + h)),
            pl.BlockSpec((tm, 2 * hd), lambda i, h: (i, nq + h)),
            pl.BlockSpec((halo, hd), halo_map(0)),
            pl.BlockSpec((halo, hd), halo_map(nq)),
            pl.BlockSpec((halo, 2 * hd), lambda i, h: (jnp.maximum(i * hb - 1, 0), nq + h)),
            pl.BlockSpec((DN_CONV, hd), lambda i, h: (0, h)),
            pl.BlockSpec((DN_CONV, hd), lambda i, h: (0, nq + h)),
            pl.BlockSpec((DN_CONV, 2 * hd), lambda i, h: (0, nq + h)),
            g_spec, row, row,
        ],
        out_specs=[big_spec, big_spec, big_spec, big_spec, small_spec, g_spec],
        out_shape=[big, big, big, big, small, gshape],
        scratch_shapes=[
            pltpu.VMEM((tm + halo, hd), F32), pltpu.VMEM((tm + halo, hd), F32), pltpu.VMEM((tm + halo, 2 * hd), F32),
            pltpu.VMEM((tm, LANES), F32), pltpu.VMEM((tm, LANES), F32), pltpu.VMEM((tm, LANES), F32),
        ],
        compiler_params=_params("parallel", "arbitrary"),
        name="gdn_prep",
    )(proj, proj, proj, proj, proj, proj, conv_w, conv_w, conv_w, ab, alog, dtb)


def _gdn_scan_kernel(glast_ref, qg_ref, kd_ref, u_ref, w_ref, a_ref, z_ref, onw_ref, og_ref, s_ref):
    ci = pl.program_id(0)
    cs = DN_CHUNK
    hd = DN_HEAD_DIM
    nh = DN_V_HEADS

    @pl.when(ci == 0)
    def _():
        s_ref[...] = jnp.zeros_like(s_ref)

    onw = onw_ref[...]
    tn = (((0,), (0,)), ((), ()))
    lane = _iota((cs, 2 * cs), 1)
    cols = [slice(hv * hd, (hv + 1) * hd) for hv in range(nh)]
    states = [s_ref[hv] for hv in range(nh)]
    ws = [jnp.dot(jnp.concatenate([w_ref[:, cols[hv]], qg_ref[:, cols[hv]]], axis=0), states[hv].astype(BF16),
                  preferred_element_type=F32) for hv in range(nh)]
    v_new = [(u_ref[:, cols[hv]].astype(F32) - ws[hv][:cs]).astype(BF16) for hv in range(nh)]
    for hv in range(nh):
        decay = jnp.exp(jnp.full((1, hd), glast_ref[ci, hv], F32))
        s_ref[hv] = states[hv] * decay + lax.dot_general(kd_ref[:, cols[hv]], v_new[hv], tn, preferred_element_type=F32)
    for pair in range(nh // 2):
        a2 = a_ref[:, pair * 2 * cs:(pair + 1) * 2 * cs]
        v2 = jnp.concatenate([v_new[2 * pair], v_new[2 * pair + 1]], axis=0)
        for s in range(2):
            hv = 2 * pair + s
            a_s = jnp.where((lane // cs) == s, a2, jnp.zeros_like(a2))
            o = ws[hv][cs:] + jnp.dot(a_s, v2, preferred_element_type=F32)
            on = o * lax.rsqrt(jnp.mean(o * o, axis=-1, keepdims=True) + NORM_EPS) * onw
            og_ref[:, cols[hv]] = (on * _silu(z_ref[:, cols[hv]].astype(F32))).astype(BF16)


def gdn_scan(glast, qg, kd, u, w, intra, proj, o_norm_w):
    t = qg.shape[0]
    cs = DN_CHUNK
    big_spec = pl.BlockSpec((cs, DN_V_DIM), lambda c, g: (c, 0))
    return pl.pallas_call(
        _gdn_scan_kernel,
        grid_spec=pltpu.PrefetchScalarGridSpec(
            num_scalar_prefetch=1,
            grid=(t // cs,),
            in_specs=[big_spec, big_spec, big_spec, big_spec,
                      pl.BlockSpec((cs, DN_V_HEADS * cs), lambda c, g: (c, 0)),
                      pl.BlockSpec((cs, DN_V_DIM), lambda c, g: (c, DN_CONV_CH // DN_V_DIM)),
                      pl.BlockSpec((1, DN_HEAD_DIM), lambda c, g: (0, 0))],
            out_specs=big_spec,
            scratch_shapes=[pltpu.VMEM((DN_V_HEADS, DN_HEAD_DIM, DN_HEAD_DIM), F32)],
        ),
        out_shape=jax.ShapeDtypeStruct((t, DN_V_DIM), BF16),
        compiler_params=_params("arbitrary"),
        name="gdn_scan",
    )(glast, qg, kd, u, w, intra, proj, o_norm_w.reshape(1, -1))


def gated_deltanet_block(x, h, gate, layer, w_in, conv_w, a_log, dt_bias, o_norm_w, w_out):
    t = x.shape[0]
    n_main = DN_CONV_CH + DN_V_DIM
    w_ab = jnp.pad(w_in[layer, :, n_main:], ((0, 0), (0, LANES - 2 * DN_V_HEADS)))
    proj = matmul(h, w_in, n_main, layer=layer, mode="plain", out_dtype=BF16, tm=1024, tn=1024)
    ab = matmul(h, w_ab, LANES, mode="plain", out_dtype=F32, tm=1024, tn=LANES)
    qg, kd, u, w, intra, gcum = gdn_prep(proj, ab, conv_w, a_log, dt_bias)
    glast = gcum.reshape(t // DN_CHUNK, DN_CHUNK, LANES)[:, DN_CHUNK - 1, :DN_V_HEADS]
    og = gdn_scan(glast, qg, kd, u, w, intra, proj, o_norm_w)
    zero_bias = jnp.zeros((1, D_MODEL), F32)
    return matmul(og, w_out, D_MODEL, layer=layer, mode="residual", out_dtype=F32, tm=1024, tn=512,
                  bias=zero_bias, res=x, gate=gate)


def _sgu_spatial_kernel(u_ref, v_ref, lnw_ref, lnb_ref, wsp_ref, bsp_ref, o_ref):
    cs = SGU_CHUNK
    gd = SGU_WIDTH // SGU_GROUPS
    v = v_ref[...].astype(F32)
    mu = jnp.mean(v, axis=-1, keepdims=True)
    var = jnp.mean(jnp.square(v - mu), axis=-1, keepdims=True)
    vn = ((v - mu) * lax.rsqrt(var + NORM_EPS) * lnw_ref[...] + lnb_ref[...]).astype(BF16)
    r = _iota((cs, cs), 0)
    c = _iota((cs, cs), 1)
    bsp = bsp_ref[...]
    for g in range(SGU_GROUPS):
        cols = slice(g * gd, (g + 1) * gd)
        wg = jnp.where(c <= r, wsp_ref[g], 0.0).astype(BF16)
        sp = jnp.dot(wg, vn[:, cols], preferred_element_type=F32) + bsp[:, g:g + 1]
        o_ref[:, cols] = (u_ref[:, cols].astype(F32) * sp).astype(BF16)


def sgu_spatial(zz, ln_w, ln_b, w_sp, b_sp):
    t = zz.shape[0]
    cs = SGU_CHUNK
    wd = SGU_WIDTH
    bsp_t = jnp.pad(b_sp.T, ((0, 0), (0, LANES - SGU_GROUPS)))
    row = pl.BlockSpec((1, wd), lambda i: (0, 0))
    return pl.pallas_call(
        _sgu_spatial_kernel,
        grid=(t // cs,),
        in_specs=[
            pl.BlockSpec((cs, wd), lambda i: (i, 0)),
            pl.BlockSpec((cs, wd), lambda i: (i, 1)),
            row, row,
            pl.BlockSpec((SGU_GROUPS, cs, cs), lambda i: (0, 0, 0)),
            pl.BlockSpec((cs, LANES), lambda i: (0, 0)),
        ],
        out_specs=pl.BlockSpec((cs, wd), lambda i: (i, 0)),
        out_shape=jax.ShapeDtypeStruct((t, wd), BF16),
        compiler_params=_params("parallel"),
        name="sgu_spatial",
    )(zz, zz, ln_w.reshape(1, -1), ln_b.reshape(1, -1), w_sp, bsp_t)


def chunked_gmlp_block(x, h, gate, layer, w_in, b_in, ln_w, ln_b, w_sp, b_sp, w_out, b_out):
    zz = matmul(h, w_in, 2 * SGU_WIDTH, layer=layer, mode="bias_gelu", out_dtype=BF16, tm=1024, tn=1024,
                bias=b_in.reshape(1, -1))
    su = sgu_spatial(zz, ln_w, ln_b, w_sp, b_sp)
    return matmul(su, w_out, D_MODEL, layer=layer, mode="residual", out_dtype=F32, tm=1024, tn=512,
                  bias=b_out.reshape(1, -1), res=x, gate=gate)


def _dot_split3(a, b):
    a_hi = a.astype(BF16)
    b_hi = b.astype(BF16)
    a_lo = (a - a_hi.astype(F32)).astype(BF16)
    b_lo = (b - b_hi.astype(F32)).astype(BF16)
    return (jnp.dot(a_hi, b_hi, preferred_element_type=F32)
            + (jnp.dot(a_lo, b_hi, preferred_element_type=F32) + jnp.dot(a_hi, b_lo, preferred_element_type=F32)))


def _router_kernel(x_ref, w_ref, sc_ref, sh_ref, wr_ref, br_ref, h_ref, idx_ref, gate_ref, pos_ref, cnt_ref, carry):
    i = pl.program_id(0)
    tm = x_ref.shape[0]

    @pl.when(i == 0)
    def _():
        carry[...] = jnp.zeros_like(carry)

    h = _norm_mod(x_ref[...], w_ref[...], sc_ref[...], sh_ref[...])
    for j in range(ROW_SPLIT):
        h_ref[:, j, :] = h[:, j * LANES:(j + 1) * LANES]
    lane = _iota((tm, LANES), 1).astype(F32)
    neg = jnp.float32(-jnp.inf)
    logits = _dot_split3(h, wr_ref[...]) + br_ref[...]
    logits = jnp.where(lane < N_EXPERTS, logits, neg)
    vals, idxs = [], []
    for _ in range(TOP_K):
        m = jnp.max(logits, axis=-1, keepdims=True)
        ix = jnp.min(jnp.where(logits == m, lane, float(LANES)), axis=-1, keepdims=True)
        vals.append(m)
        idxs.append(ix)
        logits = jnp.where(lane == ix, neg, logits)
    es = [jnp.exp(v - vals[0]) for v in vals]
    denom = es[0] + es[1] + es[2] + es[3]
    multi = jnp.zeros((tm, LANES), F32)
    for ix in idxs:
        multi = jnp.where(lane == ix, 1.0, multi)
    r = _iota((tm, tm), 0)
    c = _iota((tm, tm), 1)
    before = jnp.where(c < r, 1.0, 0.0).astype(BF16)
    rank = jnp.dot(before, multi.astype(BF16), preferred_element_type=F32) + carry[0:1, :]
    idx_t = jnp.zeros((tm, LANES), F32)
    gate_t = jnp.zeros((tm, LANES), F32)
    pos_t = jnp.zeros((tm, LANES), F32)
    for kk in range(TOP_K):
        pk = jnp.sum(jnp.where(lane == idxs[kk], rank, 0.0), axis=-1, keepdims=True)
        idx_t = jnp.where(lane == kk, idxs[kk], idx_t)
        gate_t = jnp.where(lane == kk, es[kk] / denom, gate_t)
        pos_t = jnp.where(lane == kk, pk, pos_t)
    idx_ref[...] = idx_t.astype(I32)
    gate_ref[...] = gate_t
    pos_ref[...] = pos_t.astype(I32)
    carry[...] = carry[...] + jnp.sum(multi, axis=0, keepdims=True)
    cnt_ref[...] = carry[...]


def moe_router(x, w, sc, sh, w_router, b_router):
    t, d = x.shape
    assert d == ROW_SPLIT * LANES
    tm = 256
    wr = jnp.pad(w_router, ((0, 0), (0, LANES - N_EXPERTS)))
    br = jnp.pad(b_router.reshape(1, -1), ((0, 0), (0, LANES - N_EXPERTS)))
    vec = pl.BlockSpec((1, d), lambda i: (0, 0))
    tile = pl.BlockSpec((tm, LANES), lambda i: (i, 0))
    return pl.pallas_call(
        _router_kernel,
        grid=(t // tm,),
        in_specs=[pl.BlockSpec((tm, d), lambda i: (i, 0)), vec, vec, vec,
                  pl.BlockSpec((d, LANES), lambda i: (0, 0)), pl.BlockSpec((1, LANES), lambda i: (0, 0))],
        out_specs=[pl.BlockSpec((tm, ROW_SPLIT, LANES), lambda i: (i, 0, 0)), tile, tile, tile,
                   pl.BlockSpec((8, LANES), lambda i: (0, 0))],
        out_shape=[jax.ShapeDtypeStruct((t, ROW_SPLIT, LANES), F32), jax.ShapeDtypeStruct((t, LANES), I32),
                   jax.ShapeDtypeStruct((t, LANES), F32), jax.ShapeDtypeStruct((t, LANES), I32),
                   jax.ShapeDtypeStruct((8, LANES), F32)],
        scratch_shapes=[pltpu.VMEM((8, LANES), F32)],
        compiler_params=_params("arbitrary"),
        name="moe_router",
    )(x, w, sc, sh, wr, br)


def _dispatch_kernel(dest_ref, ztile_ref, h_ref, xs_ref, zbuf, sem, zsem):
    i = pl.program_id(0)
    tm = h_ref.shape[0]
    zt = zbuf.shape[0]

    @pl.when(i == 0)
    def _():
        zbuf[...] = jnp.zeros_like(zbuf)

        def zero_copy(e):
            row = pl.multiple_of(ztile_ref[e], zt)
            return pltpu.make_async_copy(zbuf, xs_ref.at[pl.ds(row, zt)], zsem)

        for e in range(N_EXPERTS):
            @pl.when(ztile_ref[e] >= 0)
            def _():
                zero_copy(e).start()
        for e in range(N_EXPERTS):
            @pl.when(ztile_ref[e] >= 0)
            def _():
                zero_copy(e).wait()

        def zero_tail(j, carry):
            row = pl.multiple_of(j * zt, zt)
            cp = pltpu.make_async_copy(zbuf, xs_ref.at[pl.ds(row, zt)], zsem)
            cp.start()
            cp.wait()
            return carry

        lax.fori_loop(ztile_ref[N_EXPERTS], xs_ref.shape[0] // zt, zero_tail, 0)

    def row_copy(r, d):
        return pltpu.make_async_copy(h_ref.at[pl.ds(r, 1)], xs_ref.at[pl.ds(d, 1)], sem)

    def issue(r, carry):
        base = (i * tm + r) * TOP_K
        for kk in range(TOP_K):
            row_copy(r, dest_ref[base + kk]).start(priority=kk % 2)
        return carry

    lax.fori_loop(0, tm, issue, 0, unroll=8)

    for kk in range(TOP_K):
        pltpu.make_async_copy(h_ref, xs_ref.at[pl.ds(0, tm)], sem).wait()


def moe_dispatch(dest, zero_plan, h, n_rows):
    t = h.shape[0]
    row = h.shape[1:]
    tm = 256
    return pl.pallas_call(
        _dispatch_kernel,
        grid_spec=pltpu.PrefetchScalarGridSpec(
            num_scalar_prefetch=2,
            grid=(t // tm,),
            in_specs=[pl.BlockSpec((tm,) + row, lambda i, dst, zt: (i, 0, 0))],
            out_specs=pl.BlockSpec(memory_space=pl.ANY),
            scratch_shapes=[pltpu.VMEM((MOE_TILE,) + row, F32), pltpu.SemaphoreType.DMA(()), pltpu.SemaphoreType.DMA(())],
        ),
        out_shape=jax.ShapeDtypeStruct((n_rows,) + row, F32),
        compiler_params=_params("arbitrary"),
        name="moe_dispatch",
    )(dest.reshape(-1), zero_plan, h)


def _stream_expert_weights(w_hbm, wbuf, stage, sem, layer, e_cur, e_next, tig, gprev, cur, phase=0, phases=1):
    n_pairs = wbuf.shape[1] // 2
    cw = stage.shape[-1]
    prev_calls = gprev * phases
    q = tig * phases + phase

    def chunk_copy(e, c):
        return pltpu.make_async_copy(w_hbm.at[layer, e, :, pl.ds(c * cw, cw)], stage.at[c % 2], sem.at[c % 2])

    def start_pair(e, p):
        chunk_copy(e, 2 * p).start()
        chunk_copy(e, 2 * p + 1).start()

    def retire_pair(e, p, slot):
        for c in (2 * p, 2 * p + 1):
            chunk_copy(e, c).wait()
            wbuf[slot, c] = stage[c % 2].astype(BF16)

    if phase == 0:
        @pl.when(tig == 0)
        def _():
            for p in range(n_pairs):
                @pl.when(prev_calls <= p)
                def _():
                    start_pair(e_cur, p)

                @pl.when(prev_calls <= p + 1)
                def _():
                    retire_pair(e_cur, p, cur)

    @pl.when(e_next >= 0)
    def _():
        for j in range(phase, n_pairs + 1, phases):
            @pl.when(q == j)
            def _():
                if j >= 1:
                    retire_pair(e_next, j - 1, 1 - cur)
                if j < n_pairs:
                    start_pair(e_next, j)


def _for_row_bucket(n_blocks, tm, body):
    for k in range(1, tm // ROW_BLOCK + 1):
        @pl.when(n_blocks == k)
        def _():
            body(k * ROW_BLOCK)


def _ffn_up_kernel(te_ref, tig_ref, gprev_ref, nxt_ref, gidx_ref, nb_ref, nu_ref, x_ref, w_hbm, b_ref, o_ref,
                   wbuf, stage, sem, *, layer):
    i = pl.program_id(0)
    tm, f = o_ref.shape
    nc = wbuf.shape[1] // 2
    fc = wbuf.shape[-1]

    @pl.when(i < nu_ref[0])
    def _():
        cur = gidx_ref[i] % 2
        _stream_expert_weights(w_hbm, wbuf, stage, sem, layer, te_ref[i], nxt_ref[i], tig_ref[i], gprev_ref[i], cur)

        def compute(m):
            xb = jnp.concatenate([x_ref[:m, j, :] for j in range(ROW_SPLIT)], axis=1).astype(BF16)
            for c in range(nc):
                g = jnp.dot(xb, wbuf[cur, c], preferred_element_type=F32) + b_ref[:, c * fc:(c + 1) * fc]
                u = jnp.dot(xb, wbuf[cur, nc + c], preferred_element_type=F32) + b_ref[:, f + c * fc:f + (c + 1) * fc]
                x_glu = jnp.minimum(g, SWIGLU_LIMIT)
                x_lin = jnp.clip(u, -SWIGLU_LIMIT, SWIGLU_LIMIT)
                act = x_glu * _sigmoid(SWIGLU_ALPHA * x_glu) * (x_lin + 1.0)
                o_ref[:m, c * fc:(c + 1) * fc] = act.astype(BF16)
            if m < tm:
                o_ref[m:, :] = jnp.zeros((tm - m, f), BF16)

        _for_row_bucket(nb_ref[i], tm, compute)

    @pl.when(i >= nu_ref[0])
    def _():
        o_ref[...] = jnp.zeros_like(o_ref)


def _ffn_down_kernel(te_ref, tig_ref, gprev_ref, nxt_ref, gidx_ref, nb_ref, nu_ref, a_ref, w_hbm, b_ref, o_ref,
                     wbuf, stage, sem, *, layer):
    i = pl.program_id(0)
    tm = o_ref.shape[0]
    fc = wbuf.shape[-1]

    @pl.when(i < nu_ref[0])
    def _():
        cur = gidx_ref[i] % 2
        _stream_expert_weights(w_hbm, wbuf, stage, sem, layer, te_ref[i], nxt_ref[i], tig_ref[i], gprev_ref[i], cur)

        def compute(m):
            a = a_ref[:m, :]
            for c in range(wbuf.shape[1]):
                res = jnp.dot(a, wbuf[cur, c], preferred_element_type=F32) + b_ref[:, c * fc:(c + 1) * fc]
                for jj in range(fc // LANES):
                    o_ref[:m, c * (fc // LANES) + jj, :] = res[:, jj * LANES:(jj + 1) * LANES]
            if m < tm:
                o_ref[m:, :, :] = jnp.zeros((tm - m,) + o_ref.shape[1:], F32)

        _for_row_bucket(nb_ref[i], tm, compute)

    @pl.when(i >= nu_ref[0])
    def _():
        o_ref[...] = jnp.zeros_like(o_ref)


def moe_experts(plan, xs, layer, w_gate_up, b_gate_up, w_down, b_down):
    n_rows = xs.shape[0]
    row = xs.shape[1:]
    tm = MOE_TILE
    n_tiles = n_rows // tm
    f, d = w_down.shape[2:]
    cw = f // 4
    n_plan = len(plan)
    expert_row = lambda i, te, *_: (te[i], 0, 0)
    any_spec = pl.BlockSpec(memory_space=pl.ANY)
    act = pl.pallas_call(
        functools.partial(_ffn_up_kernel, layer=layer),
        grid_spec=pltpu.PrefetchScalarGridSpec(
            num_scalar_prefetch=n_plan,
            grid=(n_tiles,),
            in_specs=[pl.BlockSpec((tm,) + row, lambda i, *p: (jnp.minimum(i, p[-1][0] - 1), 0, 0)),
                      any_spec,
                      pl.BlockSpec((None, 1, 2 * f), expert_row)],
            out_specs=pl.BlockSpec((tm, f), lambda i, *p: (i, 0)),
            scratch_shapes=[pltpu.VMEM((2, 2 * f // cw, d, cw), BF16), pltpu.VMEM((2, d, cw), F32),
                            pltpu.SemaphoreType.DMA((2,))],
        ),
        out_shape=jax.ShapeDtypeStruct((n_rows, f), BF16),
        compiler_params=_params("arbitrary"),
        name="moe_ffn_up",
    )(*plan, xs, w_gate_up, b_gate_up.reshape(N_EXPERTS, 1, 2 * f))
    return pl.pallas_call(
        functools.partial(_ffn_down_kernel, layer=layer),
        grid_spec=pltpu.PrefetchScalarGridSpec(
            num_scalar_prefetch=n_plan,
            grid=(n_tiles,),
            in_specs=[pl.BlockSpec((tm, f), lambda i, *p: (i, 0)),
                      any_spec,
                      pl.BlockSpec((None, 1, d), expert_row)],
            out_specs=pl.BlockSpec((tm,) + row, lambda i, *p: (i, 0, 0)),
            scratch_shapes=[pltpu.VMEM((2, d // cw, f, cw), BF16), pltpu.VMEM((2, f, cw), F32),
                            pltpu.SemaphoreType.DMA((2,))],
        ),
        out_shape=jax.ShapeDtypeStruct((n_rows,) + row, F32),
        compiler_params=_params("arbitrary"),
        name="moe_ffn_down",
    )(*plan, act, w_down, b_down.reshape(N_EXPERTS, 1, d))


def _combine_kernel(dest_ref, x_ref, gate_ref, g2_ref, nw_ref, nsc_ref, nsh_ref, y_ref, *refs, emit_x):
    if emit_x:
        o_ref, hn_ref, ybuf, sem = refs
    else:
        hn_ref, ybuf, sem = refs
    i = pl.program_id(0)
    n = pl.num_programs(0)
    tm = x_ref.shape[0]

    def row_copy(tile, slot, r, kk):
        d = dest_ref[(tile * tm + r) * TOP_K + kk]
        return pltpu.make_async_copy(y_ref.at[pl.ds(d, 1)], ybuf.at[slot, kk, pl.ds(r, 1)], sem.at[slot])

    def issue_tile(tile, slot):
        def body(r, carry):
            for kk in range(TOP_K):
                row_copy(tile, slot, r, kk).start(priority=kk % 2)
            return carry
        lax.fori_loop(0, tm, body, 0, unroll=8)

    def drain_tile(slot):
        for kk in range(TOP_K):
            pltpu.make_async_copy(y_ref.at[pl.ds(0, tm)], ybuf.at[slot, kk], sem.at[slot]).wait()

    slot = i % 2

    @pl.when(i == 0)
    def _():
        issue_tile(0, 0)

    @pl.when(i + 1 < n)
    def _():
        issue_tile(i + 1, 1 - slot)

    drain_tile(slot)
    gates = gate_ref[...]
    slabs = []
    for j in range(ROW_SPLIT):
        part = ybuf[slot, 0, :, j, :] * gates[:, 0:1]
        for kk in range(1, TOP_K):
            part = part + ybuf[slot, kk, :, j, :] * gates[:, kk:kk + 1]
        slabs.append(part)
    x_new = x_ref[...] + g2_ref[...] * jnp.concatenate(slabs, axis=1)
    if emit_x:
        o_ref[...] = x_new
    hn_ref[...] = _norm_mod(x_new, nw_ref[...], nsc_ref[...], nsh_ref[...]).astype(hn_ref.dtype)


def moe_combine(dest, x, gates, gate2, y, next_norm, next_dtype, emit_x):
    t, d = x.shape
    tm = 128
    tile = pl.BlockSpec((tm, d), lambda i, dst: (i, 0))
    vec = pl.BlockSpec((1, d), lambda i, dst: (0, 0))
    normed = jax.ShapeDtypeStruct((t, d), next_dtype)
    return pl.pallas_call(
        functools.partial(_combine_kernel, emit_x=emit_x),
        grid_spec=pltpu.PrefetchScalarGridSpec(
            num_scalar_prefetch=1,
            grid=(t // tm,),
            in_specs=[tile, pl.BlockSpec((tm, LANES), lambda i, dst: (i, 0)), vec, vec, vec, vec,
                      pl.BlockSpec(memory_space=pl.ANY)],
            out_specs=[tile, tile] if emit_x else tile,
            scratch_shapes=[pltpu.VMEM((2, TOP_K, tm) + y.shape[1:], F32), pltpu.SemaphoreType.DMA((2,))],
        ),
        out_shape=[jax.ShapeDtypeStruct((t, d), F32), normed] if emit_x else normed,
        compiler_params=_params("arbitrary"),
        name="moe_combine",
    )(dest.reshape(-1), x, gates, gate2, *next_norm, y)


def moe_block(x, norm_w, sc, sh, gate2, layer, w_router, b_router, w_gate_up, b_gate_up, w_down, b_down,
              next_norm, next_dtype, emit_x):
    t, d = x.shape
    tm = MOE_TILE
    h, idx, gates, pos, counts = moe_router(x, norm_w, sc, sh, w_router, b_router)
    cnt = counts[0, :N_EXPERTS].astype(I32)
    padded = (cnt + tm - 1) // tm * tm
    pend = jnp.cumsum(padded)
    pstart = pend - padded
    n_tiles = (t * TOP_K + N_EXPERTS * (tm - 1) + tm - 1) // tm
    is_expert = idx[:, :TOP_K, None] == jnp.arange(N_EXPERTS, dtype=I32)
    dest = pos[:, :TOP_K] + jnp.sum(jnp.where(is_expert, pstart, 0), axis=-1)
    tile_start = jnp.arange(n_tiles, dtype=I32) * tm
    tile_expert = jnp.minimum(jnp.sum((pend[None, :] <= tile_start[:, None]).astype(I32), axis=1), N_EXPERTS - 1)
    n_used = (pend[-1:] // tm).astype(I32)
    zero_plan = jnp.concatenate([jnp.where(padded > 0, pend - tm, -1), n_used]).astype(I32)
    xs = moe_dispatch(dest, zero_plan, h, n_tiles * tm)
    experts = jnp.arange(N_EXPERTS, dtype=I32)
    group_tiles = padded // tm
    nonempty = group_tiles > 0
    later = (experts[None, :] > experts[:, None]) & nonempty[None, :]
    earlier = (experts[None, :] < experts[:, None]) & nonempty[None, :]
    next_e = jnp.min(jnp.where(later, experts[None, :], N_EXPERTS), axis=1)
    prev_e = jnp.max(jnp.where(earlier, experts[None, :], -1), axis=1)
    next_e = jnp.where(next_e < N_EXPERTS, next_e, -1)
    prev_tiles = jnp.where(prev_e >= 0, group_tiles[jnp.maximum(prev_e, 0)], 0)
    ordinal = jnp.cumsum(nonempty.astype(I32)) - nonempty.astype(I32)
    tile_in_group = jnp.arange(n_tiles, dtype=I32) - (pstart // tm)[tile_expert]
    rows_in_tile = jnp.clip(cnt[tile_expert] - tile_in_group * tm, 1, tm)
    row_blocks = (rows_in_tile + ROW_BLOCK - 1) // ROW_BLOCK
    plan = tuple(a.astype(I32) for a in (tile_expert, tile_in_group, prev_tiles[tile_expert], next_e[tile_expert],
                                         ordinal[tile_expert], row_blocks, n_used))
    y = moe_experts(plan, xs, layer, w_gate_up, b_gate_up, w_down, b_down)
    return moe_combine(dest, x, gates, gate2, y, next_norm, next_dtype, emit_x)


def kernel(x, c, ada_w, ada_b, norm_w, dn_w_in, dn_conv_w, dn_a_log, dn_dt_bias, dn_o_norm_w, dn_w_out, sgu_w_in, sgu_b_in, sgu_ln_w, sgu_ln_b, sgu_w_sp, sgu_b_sp, sgu_w_out, sgu_b_out, moe_w_router, moe_b_router, moe_w_gate_up, moe_b_gate_up, moe_w_down, moe_b_down, final_norm_w):
    bsz, seq, d = x.shape
    assert bsz == 1 and d == D_MODEL
    xt = x.reshape(seq, d)
    mod = ada_mod(c, ada_w, ada_b)
    mods = [[mod[i:i + 1, s * d:(s + 1) * d] for s in range(6)] for i in range(DEPTH)]
    zero = jnp.zeros((1, d), F32)
    h = norm_mod(xt, norm_w[0, 0:1], mods[0][1], mods[0][0], BF16)
    for i in range(DEPTH):
        sh1, sc1, gt1, sh2, sc2, gt2 = mods[i]
        last = i == DEPTH - 1
        next_norm = (final_norm_w.reshape(1, d), zero, zero) if last else (norm_w[i + 1, 0:1], mods[i + 1][1], mods[i + 1][0])
        j = i // 2
        if i % 2 == 0:
            xt = gated_deltanet_block(xt, h, gt1, j, dn_w_in, dn_conv_w[j], dn_a_log[j], dn_dt_bias[j],
                                      dn_o_norm_w[j], dn_w_out)
        else:
            xt = chunked_gmlp_block(xt, h, gt1, j, sgu_w_in, sgu_b_in[j], sgu_ln_w[j], sgu_ln_b[j], sgu_w_sp[j],
                                    sgu_b_sp[j], sgu_w_out, sgu_b_out[j])
        res = moe_block(xt, norm_w[i, 1:2], sc2, sh2, gt2, i, moe_w_router[i], moe_b_router[i],
                        moe_w_gate_up, moe_b_gate_up[i], moe_w_down, moe_b_down[i],
                        next_norm, F32 if last else BF16, not last)
        if last:
            return res.reshape(bsz, seq, d)
        xt, h = res
```

```python
import functools

import jax
import jax.numpy as jnp
from jax import lax
from jax.experimental import pallas as pl
from jax.experimental.pallas import tpu as pltpu

F32 = jnp.float32
BF16 = jnp.bfloat16
I32 = jnp.int32
HIGHEST = lax.Precision.HIGHEST

D_MODEL = 2048
DEPTH = 2
NORM_EPS = 1e-6
DN_HEAD_DIM = 128
DN_QK_HEADS = 16
DN_V_HEADS = 32
DN_QK_DIM = 2048
DN_V_DIM = 4096
DN_CONV_CH = 8192
DN_CONV = 4
DN_CHUNK = 64
SGU_WIDTH = 4096
SGU_CHUNK = 128
SGU_GROUPS = 32
N_EXPERTS = 32
TOP_K = 4
EXPERT_DIM = 2048
SWIGLU_LIMIT = 7.0
SWIGLU_ALPHA = 1.702

LANES = 128
VMEM_LIMIT = 58 * 1024 * 1024
MOE_TILE = 512
ROW_BLOCK = 128
GDN_TILE = 2048


def _params(*sem):
    return pltpu.CompilerParams(dimension_semantics=sem, vmem_limit_bytes=VMEM_LIMIT)


def _iota(shape, dim):
    return lax.broadcasted_iota(I32, shape, dim)


def _sigmoid(x):
    return 1.0 / (1.0 + jnp.exp(-x))


def _silu(x):
    return x * _sigmoid(x)


def _ada_kernel(c_ref, w_ref, b_ref, o_ref):
    c = c_ref[...]
    o_ref[...] = jnp.dot(_silu(c), w_ref[...], precision=HIGHEST, preferred_element_type=F32) + b_ref[...]


def ada_mod(c, ada_w, ada_b):
    depth, d, n = ada_w.shape
    tn = 1024
    c8 = jnp.broadcast_to(c, (8, d))
    out = pl.pallas_call(
        _ada_kernel,
        grid=(depth, n // tn),
        in_specs=[
            pl.BlockSpec((8, d), lambda l, j: (0, 0)),
            pl.BlockSpec((None, d, tn), lambda l, j: (l, 0, j)),
            pl.BlockSpec((None, 1, tn), lambda l, j: (l, 0, j)),
        ],
        out_specs=pl.BlockSpec((None, 8, tn), lambda l, j: (l, 0, j)),
        out_shape=jax.ShapeDtypeStruct((depth, 8, n), F32),
        compiler_params=_params("parallel", "parallel"),
        name="ada_mod",
    )(c8, ada_w, ada_b.reshape(depth, 1, n))
    return out[:, 0, :]


def _norm_mod(x, w, sc, sh):
    y = x * lax.rsqrt(jnp.mean(x * x, axis=-1, keepdims=True) + NORM_EPS)
    return (y * w) * (1.0 + sc) + sh


def _norm_mod_kernel(x_ref, w_ref, sc_ref, sh_ref, o_ref):
    o_ref[...] = _norm_mod(x_ref[...], w_ref[...], sc_ref[...], sh_ref[...]).astype(o_ref.dtype)


def norm_mod(x, w, sc, sh, out_dtype):
    t, d = x.shape
    tm = 512
    vec = pl.BlockSpec((1, d), lambda i: (0, 0))
    return pl.pallas_call(
        _norm_mod_kernel,
        grid=(t // tm,),
        in_specs=[pl.BlockSpec((tm, d), lambda i: (i, 0)), vec, vec, vec],
        out_specs=pl.BlockSpec((tm, d), lambda i: (i, 0)),
        out_shape=jax.ShapeDtypeStruct((t, d), out_dtype),
        compiler_params=_params("parallel"),
        name="norm_mod",
    )(x, w, sc, sh)


def _gelu_exact(x):
    return 0.5 * x * (1.0 + lax.erf(x * (2.0 ** -0.5)))


def _mm_kernel(a_ref, w_ref, *refs, mode):
    *refs, wb = refs
    @pl.when(pl.program_id(1) == 0)
    def _():
        wb[...] = w_ref[...].astype(BF16)

    acc = jnp.dot(a_ref[...], wb[...], preferred_element_type=F32)
    if mode == "plain":
        (o_ref,) = refs
        o_ref[...] = acc.astype(o_ref.dtype)
    elif mode == "bias_gelu":
        b_ref, o_ref = refs
        o_ref[...] = _gelu_exact(acc + b_ref[...]).astype(o_ref.dtype)
    else:
        b_ref, res_ref, gate_ref, o_ref = refs
        o_ref[...] = res_ref[...] + gate_ref[...] * (acc + b_ref[...])


def matmul(a, w, n, *, mode, out_dtype, tm, tn, layer=None, bias=None, res=None, gate=None):
    m, k = a.shape
    assert m % tm == 0 and n % tn == 0 and w.shape[-1] >= n
    row = pl.BlockSpec((1, tn), lambda j, i: (0, j))
    if layer is None:
        w_spec = pl.BlockSpec((k, tn), lambda j, i: (0, j))
    else:
        w_spec = pl.BlockSpec((None, k, tn), lambda j, i: (layer, 0, j))
    in_specs = [pl.BlockSpec((tm, k), lambda j, i: (i, 0)), w_spec]
    args = [a, w]
    if mode != "plain":
        in_specs.append(row)
        args.append(bias)
    if mode == "residual":
        in_specs += [pl.BlockSpec((tm, tn), lambda j, i: (i, j)), row]
        args += [res, gate]
    return pl.pallas_call(
        functools.partial(_mm_kernel, mode=mode),
        grid=(n // tn, m // tm),
        in_specs=in_specs,
        out_specs=pl.BlockSpec((tm, tn), lambda j, i: (i, j)),
        out_shape=jax.ShapeDtypeStruct((m, n), out_dtype),
        scratch_shapes=[pltpu.VMEM((k, tn), BF16)],
        compiler_params=_params("parallel", "arbitrary"),
        name="matmul_" + mode,
    )(*args)


def _softplus(x):
    return jnp.maximum(x, 0.0) + jnp.log1p(jnp.exp(-jnp.abs(x)))


def _unit_lower_inverses(lows, n_block):
    n = lows[0].shape[0]
    r = _iota((n, n), 0)
    c = _iota((n, n), 1)
    eye = jnp.where(r == c, 1.0, 0.0).astype(F32)

    def mm(a, b):
        return jnp.dot(a, b, preferred_element_type=F32)

    def bf(ms):
        return [m.astype(BF16) for m in ms]

    base = (r // 8) == (c // 8)
    lows_b = bf(lows)
    ds = [jnp.where(base, low, 0.0) for low in lows]
    ds_b = [jnp.where(base, low, jnp.zeros_like(low)) for low in lows_b]
    xs = [eye - d for d in ds]
    ps_b = bf([mm(d, d) for d in ds_b])
    xs = [x + mm(xb, p) for x, xb, p in zip(xs, bf(xs), ps_b)]
    ps_b = bf([mm(p, p) for p in ps_b])
    xs = [x + mm(xb, p) for x, xb, p in zip(xs, bf(xs), ps_b)]
    s = 8
    while s < n_block:
        pair = ((r // (2 * s)) == (c // (2 * s))) & ((r // s) != (c // s))
        xs_b = bf(xs)
        ts_b = bf([mm(jnp.where(pair, low, jnp.zeros_like(low)), xb) for low, xb in zip(lows_b, xs_b)])
        xs = [x - mm(xb, t) for x, xb, t in zip(xs, xs_b, ts_b)]
        s *= 2
    return xs


def _gdn_prep_kernel(q_ref, k_ref, v_ref, qh_ref, kh_ref, vh_ref, wq_ref, wk_ref, wv_ref, ab_ref, alog_ref, dtb_ref,
                     qg_ref, kd_ref, u_ref, w_ref, a_ref, gcum_ref,
                     win_q, win_k, win_v, gcum_s, gtot_s, beta_s):
    i = pl.program_id(0)
    h = pl.program_id(1)
    tm = q_ref.shape[0]
    halo = qh_ref.shape[0]
    cs = DN_CHUNK
    hd = DN_HEAD_DIM

    def conv_silu(x_ref, halo_ref, w_ref, win_ref):
        hal = halo_ref[...].astype(F32)
        win_ref[0:halo, :] = jnp.where(i == 0, jnp.zeros_like(hal), hal)
        win_ref[halo:halo + tm, :] = x_ref[...].astype(F32)
        w = w_ref[...]
        acc = None
        for j in range(DN_CONV):
            start = halo - (DN_CONV - 1) + j
            term = win_ref[start:start + tm, :] * w[j:j + 1, :]
            acc = term if acc is None else acc + term
        return _silu(acc)

    def l2norm(x):
        return x * lax.rsqrt(jnp.sum(x * x, axis=-1, keepdims=True) + NORM_EPS)

    @pl.when(h == 0)
    def _():
        ab = ab_ref[...]
        g = -jnp.exp(alog_ref[...]) * _softplus(ab + dtb_ref[...])
        sub = 4 * DN_CHUNK
        r = _iota((sub, sub), 0)
        c = _iota((sub, sub), 1)
        same = (r // DN_CHUNK) == (c // DN_CHUNK)
        tri = jnp.where(same & (c <= r), 1.0, 0.0).astype(F32)
        blk = jnp.where(same, 1.0, 0.0).astype(F32)
        for s0 in range(0, tm, sub):
            g_sub = g[s0:s0 + sub]
            gcum = jnp.dot(tri, g_sub, precision=HIGHEST, preferred_element_type=F32)
            gcum_s[s0:s0 + sub, :] = gcum
            gcum_ref[s0:s0 + sub, :] = gcum
            gtot_s[s0:s0 + sub, :] = jnp.dot(blk, g_sub, precision=HIGHEST, preferred_element_type=F32)
        beta_s[...] = _sigmoid(ab)

    q = l2norm(conv_silu(q_ref, qh_ref, wq_ref, win_q)) * (DN_HEAD_DIM ** -0.5)
    k = l2norm(conv_silu(k_ref, kh_ref, wk_ref, win_k))
    v = conv_silu(v_ref, vh_ref, wv_ref, win_v)

    def head_gates(ref, first_lane):
        rolled = pltpu.roll(ref[...], (LANES - first_lane) % LANES, 1)
        return [jnp.broadcast_to(rolled[:, s:s + 1], (tm, hd)) for s in range(2)]

    gc = head_gates(gcum_s, 2 * h)
    gt = head_gates(gtot_s, 2 * h)
    be = head_gates(beta_s, DN_V_HEADS + 2 * h)
    vb, kbg = [], []
    for s in range(2):
        cols = slice(s * hd, (s + 1) * hd)
        eg = jnp.exp(gc[s])
        qg_ref[:, cols] = (q * eg).astype(BF16)
        kd_ref[:, cols] = (k * jnp.exp(gt[s] - gc[s])).astype(BF16)
        vb.append((v[:, cols] * be[s]).astype(BF16))
        kbg.append((k * be[s] * eg).astype(BF16))
    qb = q.astype(BF16)
    kb = k.astype(BF16)

    n2 = 2 * cs
    r = _iota((n2, n2), 0)
    c = _iota((n2, n2), 1)
    same_head = (r // cs) == (c // cs)
    causal = same_head & (c <= r)
    strict = same_head & (c < r)
    top = _iota((cs, n2), 1) < cs
    nt = (((1,), (1,)), ((), ()))
    chunks = [slice(ci * cs, (ci + 1) * cs) for ci in range(tm // cs)]

    def stack_heads(x, rows):
        return jnp.concatenate([x[0][rows], x[1][rows]], axis=0)

    k_st = [jnp.concatenate([kb[rows], kb[rows]], axis=0) for rows in chunks]
    q_st = [jnp.concatenate([qb[rows], qb[rows]], axis=0) for rows in chunks]
    kk = [lax.dot_general(ks, ks, nt, preferred_element_type=F32) for ks in k_st]
    qk = [lax.dot_general(qs, ks, nt, preferred_element_type=F32) for qs, ks in zip(q_st, k_st)]
    g_col = [stack_heads(gc, rows) for rows in chunks]
    b_col = [stack_heads(be, rows) for rows in chunks]
    decay = [jnp.exp(jnp.where(causal, g - g.T, 0.0)) for g in g_col]
    lows = [jnp.where(strict, b * kk_i * d, 0.0) for b, kk_i, d in zip(b_col, kk, decay)]
    tinvs = [t.astype(BF16) for t in _unit_lower_inverses(lows, cs)]
    us = [jnp.dot(t, stack_heads(vb, rows), preferred_element_type=F32).astype(BF16) for t, rows in zip(tinvs, chunks)]
    ws = [jnp.dot(t, stack_heads(kbg, rows), preferred_element_type=F32).astype(BF16) for t, rows in zip(tinvs, chunks)]
    for rows, u2, w2, qk_i, d in zip(chunks, us, ws, qk, decay):
        intra = jnp.where(causal, qk_i * d, 0.0)
        a_ref[rows, :] = jnp.where(top, intra[:cs], intra[cs:]).astype(BF16)
        u_ref[rows, :hd] = u2[:cs]
        u_ref[rows, hd:] = u2[cs:]
        w_ref[rows, :hd] = w2[:cs]
        w_ref[rows, hd:] = w2[cs:]


def gdn_prep(proj, ab, conv_w, a_log, dt_bias):
    t = proj.shape[0]
    tm = GDN_TILE
    halo = 16
    hb = tm // halo
    hd = DN_HEAD_DIM
    nq = DN_QK_HEADS

    def halo_map(off):
        return lambda i, h: (jnp.maximum(i * hb - 1, 0), off + h)

    pad = jnp.zeros((1, LANES - DN_V_HEADS), F32)
    alog = jnp.concatenate([a_log.reshape(1, -1), pad], axis=1)
    dtb = jnp.concatenate([dt_bias.reshape(1, -1), pad], axis=1)
    big = jax.ShapeDtypeStruct((t, DN_V_DIM), BF16)
    small = jax.ShapeDtypeStruct((t, DN_QK_DIM), BF16)
    gshape = jax.ShapeDtypeStruct((t, LANES), F32)
    big_spec = pl.BlockSpec((tm, 2 * hd), lambda i, h: (i, h))
    small_spec = pl.BlockSpec((tm, hd), lambda i, h: (i, h))
    g_spec = pl.BlockSpec((tm, LANES), lambda i, h: (i, 0))
    row = pl.BlockSpec((1, LANES), lambda i, h: (0, 0))
    return pl.pallas_call(
        _gdn_prep_kernel,
        grid=(t // tm, nq),
        in_specs=[
            pl.BlockSpec((tm, hd), lambda i, h: (i, h)),
            pl.BlockSpec((tm, hd), lambda i, h: (i, nq + h)),
            pl.BlockSpec((tm, 2 * hd), lambda i, h: (i, nq + h)),
            pl.BlockSpec((halo, hd), halo_map(0)),
            pl.BlockSpec((halo, hd), halo_map(nq)),
            pl.BlockSpec((halo, 2 * hd), lambda i, h: (jnp.maximum(i * hb - 1, 0), nq + h)),
            pl.BlockSpec((DN_CONV, hd), lambda i, h: (0, h)),
            pl.BlockSpec((DN_CONV, hd), lambda i, h: (0, nq + h)),
            pl.BlockSpec((DN_CONV, 2 * hd), lambda i, h: (0, nq + h)),
            g_spec, row, row,
        ],
        out_specs=[big_spec, big_spec, big_spec, big_spec, small_spec, g_spec],
        out_shape=[big, big, big, big, small, gshape],
        scratch_shapes=[
            pltpu.VMEM((tm + halo, hd), F32), pltpu.VMEM((tm + halo, hd), F32), pltpu.VMEM((tm + halo, 2 * hd), F32),
            pltpu.VMEM((tm, LANES), F32), pltpu.VMEM((tm, LANES), F32), pltpu.VMEM((tm, LANES), F32),
        ],
        compiler_params=_params("parallel", "arbitrary"),
        name="gdn_prep",
    )(proj, proj, proj, proj, proj, proj, conv_w, conv_w, conv_w, ab, alog, dtb)


def _gdn_scan_kernel(glast_ref, qg_ref, kd_ref, u_ref, w_ref, a_ref, z_ref, onw_ref, og_ref, s_ref):
    ci = pl.program_id(0)
    cs = DN_CHUNK
    hd = DN_HEAD_DIM
    nh = DN_V_HEADS

    @pl.when(ci == 0)
    def _():
        s_ref[...] = jnp.zeros_like(s_ref)

    onw = onw_ref[...]
    tn = (((0,), (0,)), ((), ()))
    lane = _iota((cs, 2 * cs), 1)
    cols = [slice(hv * hd, (hv + 1) * hd) for hv in range(nh)]
    states = [s_ref[hv] for hv in range(nh)]
    ws = [jnp.dot(jnp.concatenate([w_ref[:, cols[hv]], qg_ref[:, cols[hv]]], axis=0), states[hv].astype(BF16),
                  preferred_element_type=F32) for hv in range(nh)]
    v_new = [(u_ref[:, cols[hv]].astype(F32) - ws[hv][:cs]).astype(BF16) for hv in range(nh)]
    for hv in range(nh):
        decay = jnp.exp(jnp.full((1, hd), glast_ref[ci, hv], F32))
        s_ref[hv] = states[hv] * decay + lax.dot_general(kd_ref[:, cols[hv]], v_new[hv], tn, preferred_element_type=F32)
    for pair in range(nh // 2):
        a2 = a_ref[:, pair * 2 * cs:(pair + 1) * 2 * cs]
        v2 = jnp.concatenate([v_new[2 * pair], v_new[2 * pair + 1]], axis=0)
        for s in range(2):
            hv = 2 * pair + s
            a_s = jnp.where((lane // cs) == s, a2, jnp.zeros_like(a2))
            o = ws[hv][cs:] + jnp.dot(a_s, v2, preferred_element_type=F32)
            on = o * lax.rsqrt(jnp.mean(o * o, axis=-1, keepdims=True) + NORM_EPS) * onw
            og_ref[:, cols[hv]] = (on * _silu(z_ref[:, cols[hv]].astype(F32))).astype(BF16)


def gdn_scan(glast, qg, kd, u, w, intra, proj, o_norm_w):
    t = qg.shape[0]
    cs = DN_CHUNK
    big_spec = pl.BlockSpec((cs, DN_V_DIM), lambda c, g: (c, 0))
    return pl.pallas_call(
        _gdn_scan_kernel,
        grid_spec=pltpu.PrefetchScalarGridSpec(
            num_scalar_prefetch=1,
            grid=(t // cs,),
            in_specs=[big_spec, big_spec, big_spec, big_spec,
                      pl.BlockSpec((cs, DN_V_HEADS * cs), lambda c, g: (c, 0)),
                      pl.BlockSpec((cs, DN_V_DIM), lambda c, g: (c, DN_CONV_CH // DN_V_DIM)),
                      pl.BlockSpec((1, DN_HEAD_DIM), lambda c, g: (0, 0))],
            out_specs=big_spec,
            scratch_shapes=[pltpu.VMEM((DN_V_HEADS, DN_HEAD_DIM, DN_HEAD_DIM), F32)],
        ),
        out_shape=jax.ShapeDtypeStruct((t, DN_V_DIM), BF16),
        compiler_params=_params("arbitrary"),
        name="gdn_scan",
    )(glast, qg, kd, u, w, intra, proj, o_norm_w.reshape(1, -1))


def gated_deltanet_block(x, h, gate, layer, w_in, conv_w, a_log, dt_bias, o_norm_w, w_out):
    t = x.shape[0]
    n_main = DN_CONV_CH + DN_V_DIM
    w_ab = jnp.pad(w_in[layer, :, n_main:], ((0, 0), (0, LANES - 2 * DN_V_HEADS)))
    proj = matmul(h, w_in, n_main, layer=layer, mode="plain", out_dtype=BF16, tm=1024, tn=1024)
    ab = matmul(h, w_ab, LANES, mode="plain", out_dtype=F32, tm=1024, tn=LANES)
    qg, kd, u, w, intra, gcum = gdn_prep(proj, ab, conv_w, a_log, dt_bias)
    glast = gcum.reshape(t // DN_CHUNK, DN_CHUNK, LANES)[:, DN_CHUNK - 1, :DN_V_HEADS]
    og = gdn_scan(glast, qg, kd, u, w, intra, proj, o_norm_w)
    zero_bias = jnp.zeros((1, D_MODEL), F32)
    return matmul(og, w_out, D_MODEL, layer=layer, mode="residual", out_dtype=F32, tm=1024, tn=512,
                  bias=zero_bias, res=x, gate=gate)


def _sgu_spatial_kernel(u_ref, v_ref, lnw_ref, lnb_ref, wsp_ref, bsp_ref, o_ref, wm_ref):
    cs = SGU_CHUNK
    gd = SGU_WIDTH // SGU_GROUPS

    @pl.when(pl.program_id(0) == 0)
    def _():
        r = _iota((cs, cs), 0)
        c = _iota((cs, cs), 1)
        for g in range(SGU_GROUPS):
            wm_ref[g] = jnp.where(c <= r, wsp_ref[g], 0.0).astype(BF16)

    v = v_ref[...].astype(F32)
    mu = jnp.mean(v, axis=-1, keepdims=True)
    var = jnp.mean(jnp.square(v - mu), axis=-1, keepdims=True)
    vn = ((v - mu) * lax.rsqrt(var + NORM_EPS) * lnw_ref[...] + lnb_ref[...]).astype(BF16)
    bsp = bsp_ref[...]
    for g in range(SGU_GROUPS):
        cols = slice(g * gd, (g + 1) * gd)
        wg = wm_ref[g]
        for ci in range(v_ref.shape[0] // cs):
            rows = slice(ci * cs, (ci + 1) * cs)
            sp = jnp.dot(wg, vn[rows, cols], preferred_element_type=F32) + bsp[:, g:g + 1]
            o_ref[rows, cols] = (u_ref[rows, cols].astype(F32) * sp).astype(BF16)


def sgu_spatial(zz, ln_w, ln_b, w_sp, b_sp):
    t = zz.shape[0]
    cs = SGU_CHUNK
    wd = SGU_WIDTH
    bsp_t = jnp.pad(b_sp.T, ((0, 0), (0, LANES - SGU_GROUPS)))
    row = pl.BlockSpec((1, wd), lambda i: (0, 0))
    tm = 4 * cs
    return pl.pallas_call(
        _sgu_spatial_kernel,
        grid=(t // tm,),
        in_specs=[
            pl.BlockSpec((tm, wd), lambda i: (i, 0)),
            pl.BlockSpec((tm, wd), lambda i: (i, 1)),
            row, row,
            pl.BlockSpec((SGU_GROUPS, cs, cs), lambda i: (0, 0, 0)),
            pl.BlockSpec((cs, LANES), lambda i: (0, 0)),
        ],
        out_specs=pl.BlockSpec((tm, wd), lambda i: (i, 0)),
        out_shape=jax.ShapeDtypeStruct((t, wd), BF16),
        scratch_shapes=[pltpu.VMEM((SGU_GROUPS, cs, cs), BF16)],
        compiler_params=_params("arbitrary"),
        name="sgu_spatial",
    )(zz, zz, ln_w.reshape(1, -1), ln_b.reshape(1, -1), w_sp, bsp_t)


def chunked_gmlp_block(x, h, gate, layer, w_in, b_in, ln_w, ln_b, w_sp, b_sp, w_out, b_out):
    zz = matmul(h, w_in, 2 * SGU_WIDTH, layer=layer, mode="bias_gelu", out_dtype=BF16, tm=1024, tn=1024,
                bias=b_in.reshape(1, -1))
    su = sgu_spatial(zz, ln_w, ln_b, w_sp, b_sp)
    return matmul(su, w_out, D_MODEL, layer=layer, mode="residual", out_dtype=F32, tm=1024, tn=512,
                  bias=b_out.reshape(1, -1), res=x, gate=gate)


def _dot_split3(a, b):
    a_hi = a.astype(BF16)
    b_hi = b.astype(BF16)
    a_lo = (a - a_hi.astype(F32)).astype(BF16)
    b_lo = (b - b_hi.astype(F32)).astype(BF16)
    return (jnp.dot(a_hi, b_hi, preferred_element_type=F32)
            + (jnp.dot(a_lo, b_hi, preferred_element_type=F32) + jnp.dot(a_hi, b_lo, preferred_element_type=F32)))


def _router_kernel(x_ref, w_ref, sc_ref, sh_ref, wr_ref, br_ref, h_ref, idx_ref, gate_ref, pos_ref, cnt_ref, carry):
    i = pl.program_id(0)
    tm = x_ref.shape[0]

    @pl.when(i == 0)
    def _():
        carry[...] = jnp.zeros_like(carry)

    h = _norm_mod(x_ref[...], w_ref[...], sc_ref[...], sh_ref[...])
    h_ref[...] = h
    lane = _iota((tm, LANES), 1).astype(F32)
    neg = jnp.float32(-jnp.inf)
    logits = _dot_split3(h, wr_ref[...]) + br_ref[...]
    logits = jnp.where(lane < N_EXPERTS, logits, neg)
    vals, idxs = [], []
    for _ in range(TOP_K):
        m = jnp.max(logits, axis=-1, keepdims=True)
        ix = jnp.min(jnp.where(logits == m, lane, float(LANES)), axis=-1, keepdims=True)
        vals.append(m)
        idxs.append(ix)
        logits = jnp.where(lane == ix, neg, logits)
    es = [jnp.exp(v - vals[0]) for v in vals]
    denom = es[0] + es[1] + es[2] + es[3]
    multi = jnp.zeros((tm, LANES), F32)
    for ix in idxs:
        multi = jnp.where(lane == ix, 1.0, multi)
    r = _iota((tm, tm), 0)
    c = _iota((tm, tm), 1)
    before = jnp.where(c < r, 1.0, 0.0).astype(BF16)
    rank = jnp.dot(before, multi.astype(BF16), preferred_element_type=F32) + carry[0:1, :]
    idx_t = jnp.zeros((tm, LANES), F32)
    gate_t = jnp.zeros((tm, LANES), F32)
    pos_t = jnp.zeros((tm, LANES), F32)
    for kk in range(TOP_K):
        pk = jnp.sum(jnp.where(lane == idxs[kk], rank, 0.0), axis=-1, keepdims=True)
        idx_t = jnp.where(lane == kk, idxs[kk], idx_t)
        gate_t = jnp.where(lane == kk, es[kk] / denom, gate_t)
        pos_t = jnp.where(lane == kk, pk, pos_t)
    idx_ref[...] = idx_t.astype(I32)
    gate_ref[...] = gate_t
    pos_ref[...] = pos_t.astype(I32)
    carry[...] = carry[...] + jnp.sum(multi, axis=0, keepdims=True)
    cnt_ref[...] = carry[...]


def moe_router(x, w, sc, sh, w_router, b_router):
    t, d = x.shape
    tm = 1024
    wr = jnp.pad(w_router, ((0, 0), (0, LANES - N_EXPERTS)))
    br = jnp.pad(b_router.reshape(1, -1), ((0, 0), (0, LANES - N_EXPERTS)))
    vec = pl.BlockSpec((1, d), lambda i: (0, 0))
    tile = pl.BlockSpec((tm, LANES), lambda i: (i, 0))
    return pl.pallas_call(
        _router_kernel,
        grid=(t // tm,),
        in_specs=[pl.BlockSpec((tm, d), lambda i: (i, 0)), vec, vec, vec,
                  pl.BlockSpec((d, LANES), lambda i: (0, 0)), pl.BlockSpec((1, LANES), lambda i: (0, 0))],
        out_specs=[pl.BlockSpec((tm, d), lambda i: (i, 0)), tile, tile, tile, pl.BlockSpec((8, LANES), lambda i: (0, 0))],
        out_shape=[jax.ShapeDtypeStruct((t, d), F32), jax.ShapeDtypeStruct((t, LANES), I32),
                   jax.ShapeDtypeStruct((t, LANES), F32), jax.ShapeDtypeStruct((t, LANES), I32),
                   jax.ShapeDtypeStruct((8, LANES), F32)],
        scratch_shapes=[pltpu.VMEM((8, LANES), F32)],
        compiler_params=_params("arbitrary"),
        name="moe_router",
    )(x, w, sc, sh, wr, br)


def _dispatch_kernel(dest_ref, ztile_ref, h_ref, xs_ref, zbuf, sem, zsem):
    i = pl.program_id(0)
    tm = h_ref.shape[0]
    zt = zbuf.shape[0]

    @pl.when(i == 0)
    def _():
        zbuf[...] = jnp.zeros_like(zbuf)

        def zero_copy(e):
            row = pl.multiple_of(ztile_ref[e], zt)
            return pltpu.make_async_copy(zbuf, xs_ref.at[pl.ds(row, zt), :], zsem)

        for e in range(N_EXPERTS):
            @pl.when(ztile_ref[e] >= 0)
            def _():
                zero_copy(e).start()
        for e in range(N_EXPERTS):
            @pl.when(ztile_ref[e] >= 0)
            def _():
                zero_copy(e).wait()

        def zero_tail(j, carry):
            row = pl.multiple_of(j * zt, zt)
            cp = pltpu.make_async_copy(zbuf, xs_ref.at[pl.ds(row, zt), :], zsem)
            cp.start()
            cp.wait()
            return carry

        lax.fori_loop(ztile_ref[N_EXPERTS], xs_ref.shape[0] // zt, zero_tail, 0)

    def row_copy(r, d):
        return pltpu.make_async_copy(h_ref.at[pl.ds(r, 1), :], xs_ref.at[pl.ds(d, 1), :], sem)

    def issue(r, carry):
        base = (i * tm + r) * TOP_K
        for kk in range(TOP_K):
            row_copy(r, dest_ref[base + kk]).start(priority=kk % 2)
        return carry

    lax.fori_loop(0, tm, issue, 0, unroll=8)

    for kk in range(TOP_K):
        pltpu.make_async_copy(h_ref, xs_ref.at[pl.ds(0, tm), :], sem).wait()


def moe_dispatch(dest, zero_plan, h, n_rows):
    t, d = h.shape
    tm = 256
    return pl.pallas_call(
        _dispatch_kernel,
        grid_spec=pltpu.PrefetchScalarGridSpec(
            num_scalar_prefetch=2,
            grid=(t // tm,),
            in_specs=[pl.BlockSpec((tm, d), lambda i, dst, zt: (i, 0))],
            out_specs=pl.BlockSpec(memory_space=pl.ANY),
            scratch_shapes=[pltpu.VMEM((MOE_TILE, d), F32), pltpu.SemaphoreType.DMA(()), pltpu.SemaphoreType.DMA(())],
        ),
        out_shape=jax.ShapeDtypeStruct((n_rows, d), F32),
        compiler_params=_params("arbitrary"),
        name="moe_dispatch",
    )(dest.reshape(-1), zero_plan, h)


def _stream_expert_weights(w_hbm, wbuf, stage, sem, layer, e_cur, e_next, tig, gprev, cur, phase=0, phases=1):
    n_pairs = wbuf.shape[1] // 2
    cw = stage.shape[-1]
    prev_calls = gprev * phases
    q = tig * phases + phase

    def chunk_copy(e, c):
        return pltpu.make_async_copy(w_hbm.at[layer, e, :, pl.ds(c * cw, cw)], stage.at[c % 2], sem.at[c % 2])

    def start_pair(e, p):
        chunk_copy(e, 2 * p).start()
        chunk_copy(e, 2 * p + 1).start()

    def retire_pair(e, p, slot):
        for c in (2 * p, 2 * p + 1):
            chunk_copy(e, c).wait()
            wbuf[slot, c] = stage[c % 2].astype(BF16)

    if phase == 0:
        @pl.when(tig == 0)
        def _():
            for p in range(n_pairs):
                @pl.when(prev_calls <= p)
                def _():
                    start_pair(e_cur, p)

                @pl.when(prev_calls <= p + 1)
                def _():
                    retire_pair(e_cur, p, cur)

    @pl.when(e_next >= 0)
    def _():
        for j in range(phase, n_pairs + 1, phases):
            @pl.when(q == j)
            def _():
                if j >= 1:
                    retire_pair(e_next, j - 1, 1 - cur)
                if j < n_pairs:
                    start_pair(e_next, j)


def _for_row_bucket(n_blocks, tm, body):
    for k in range(1, tm // ROW_BLOCK + 1):
        @pl.when(n_blocks == k)
        def _():
            body(k * ROW_BLOCK)


def _ffn_up_kernel(te_ref, tig_ref, gprev_ref, nxt_ref, gidx_ref, nb_ref, nu_ref, x_ref, w_hbm, b_ref, o_ref,
                   wbuf, stage, sem, *, layer):
    i = pl.program_id(0)
    tm, f = o_ref.shape
    nc = wbuf.shape[1] // 2
    fc = wbuf.shape[-1]

    @pl.when(i < nu_ref[0])
    def _():
        cur = gidx_ref[i] % 2
        _stream_expert_weights(w_hbm, wbuf, stage, sem, layer, te_ref[i], nxt_ref[i], tig_ref[i], gprev_ref[i], cur)

        def compute(m):
            xb = x_ref[:m, :].astype(BF16)
            for c in range(nc):
                g = jnp.dot(xb, wbuf[cur, c], preferred_element_type=F32) + b_ref[:, c * fc:(c + 1) * fc]
                u = jnp.dot(xb, wbuf[cur, nc + c], preferred_element_type=F32) + b_ref[:, f + c * fc:f + (c + 1) * fc]
                x_glu = jnp.minimum(g, SWIGLU_LIMIT)
                x_lin = jnp.clip(u, -SWIGLU_LIMIT, SWIGLU_LIMIT)
                act = x_glu * _sigmoid(SWIGLU_ALPHA * x_glu) * (x_lin + 1.0)
                o_ref[:m, c * fc:(c + 1) * fc] = act.astype(BF16)
            if m < tm:
                o_ref[m:, :] = jnp.zeros((tm - m, f), BF16)

        _for_row_bucket(nb_ref[i], tm, compute)

    @pl.when(i >= nu_ref[0])
    def _():
        o_ref[...] = jnp.zeros_like(o_ref)


def _ffn_down_kernel(te_ref, tig_ref, gprev_ref, nxt_ref, gidx_ref, nb_ref, nu_ref, a_ref, w_hbm, b_ref, o_ref,
                     wbuf, stage, sem, *, layer):
    i = pl.program_id(0)
    tm, d = o_ref.shape
    fc = wbuf.shape[-1]

    @pl.when(i < nu_ref[0])
    def _():
        cur = gidx_ref[i] % 2
        _stream_expert_weights(w_hbm, wbuf, stage, sem, layer, te_ref[i], nxt_ref[i], tig_ref[i], gprev_ref[i], cur)

        def compute(m):
            a = a_ref[:m, :]
            for c in range(wbuf.shape[1]):
                cols = slice(c * fc, (c + 1) * fc)
                o_ref[:m, cols] = jnp.dot(a, wbuf[cur, c], preferred_element_type=F32) + b_ref[:, cols]
            if m < tm:
                o_ref[m:, :] = jnp.zeros((tm - m, d), F32)

        _for_row_bucket(nb_ref[i], tm, compute)

    @pl.when(i >= nu_ref[0])
    def _():
        o_ref[...] = jnp.zeros_like(o_ref)


def moe_experts(plan, xs, layer, w_gate_up, b_gate_up, w_down, b_down):
    n_rows, d = xs.shape
    tm = MOE_TILE
    n_tiles = n_rows // tm
    f = w_down.shape[2]
    cw = f // 4
    n_plan = len(plan)
    expert_row = lambda i, te, *_: (te[i], 0, 0)
    any_spec = pl.BlockSpec(memory_space=pl.ANY)
    act = pl.pallas_call(
        functools.partial(_ffn_up_kernel, layer=layer),
        grid_spec=pltpu.PrefetchScalarGridSpec(
            num_scalar_prefetch=n_plan,
            grid=(n_tiles,),
            in_specs=[pl.BlockSpec((tm, d), lambda i, *p: (jnp.minimum(i, p[-1][0] - 1), 0)),
                      any_spec,
                      pl.BlockSpec((None, 1, 2 * f), expert_row)],
            out_specs=pl.BlockSpec((tm, f), lambda i, *p: (i, 0)),
            scratch_shapes=[pltpu.VMEM((2, 2 * f // cw, d, cw), BF16), pltpu.VMEM((2, d, cw), F32),
                            pltpu.SemaphoreType.DMA((2,))],
        ),
        out_shape=jax.ShapeDtypeStruct((n_rows, f), BF16),
        compiler_params=_params("arbitrary"),
        name="moe_ffn_up",
    )(*plan, xs, w_gate_up, b_gate_up.reshape(N_EXPERTS, 1, 2 * f))
    return pl.pallas_call(
        functools.partial(_ffn_down_kernel, layer=layer),
        grid_spec=pltpu.PrefetchScalarGridSpec(
            num_scalar_prefetch=n_plan,
            grid=(n_tiles,),
            in_specs=[pl.BlockSpec((tm, f), lambda i, *p: (i, 0)),
                      any_spec,
                      pl.BlockSpec((None, 1, d), expert_row)],
            out_specs=pl.BlockSpec((tm, d), lambda i, *p: (i, 0)),
            scratch_shapes=[pltpu.VMEM((2, d // cw, f, cw), BF16), pltpu.VMEM((2, f, cw), F32),
                            pltpu.SemaphoreType.DMA((2,))],
        ),
        out_shape=jax.ShapeDtypeStruct((n_rows, d), F32),
        compiler_params=_params("arbitrary"),
        name="moe_ffn_down",
    )(*plan, act, w_down, b_down.reshape(N_EXPERTS, 1, d))


def _combine_kernel(dest_ref, x_ref, gate_ref, g2_ref, nw_ref, nsc_ref, nsh_ref, y_ref, *refs, emit_x):
    if emit_x:
        o_ref, hn_ref, ybuf, sem = refs
    else:
        hn_ref, ybuf, sem = refs
    i = pl.program_id(0)
    n = pl.num_programs(0)
    tm = x_ref.shape[0]

    def row_copy(tile, slot, r, kk):
        d = dest_ref[(tile * tm + r) * TOP_K + kk]
        return pltpu.make_async_copy(y_ref.at[pl.ds(d, 1), :], ybuf.at[slot, kk, pl.ds(r, 1), :], sem.at[slot])

    def issue_tile(tile, slot):
        def body(r, carry):
            for kk in range(TOP_K):
                row_copy(tile, slot, r, kk).start(priority=kk % 2)
            return carry
        lax.fori_loop(0, tm, body, 0, unroll=8)

    def drain_tile(slot):
        for kk in range(TOP_K):
            pltpu.make_async_copy(y_ref.at[pl.ds(0, tm), :], ybuf.at[slot, kk], sem.at[slot]).wait()

    slot = i % 2

    @pl.when(i == 0)
    def _():
        issue_tile(0, 0)

    @pl.when(i + 1 < n)
    def _():
        issue_tile(i + 1, 1 - slot)

    drain_tile(slot)
    gates = gate_ref[...]
    acc = ybuf[slot, 0] * gates[:, 0:1]
    for kk in range(1, TOP_K):
        acc = acc + ybuf[slot, kk] * gates[:, kk:kk + 1]
    x_new = x_ref[...] + g2_ref[...] * acc
    if emit_x:
        o_ref[...] = x_new
    hn_ref[...] = _norm_mod(x_new, nw_ref[...], nsc_ref[...], nsh_ref[...]).astype(hn_ref.dtype)


def moe_combine(dest, x, gates, gate2, y, next_norm, next_dtype, emit_x):
    t, d = x.shape
    tm = 256
    tile = pl.BlockSpec((tm, d), lambda i, dst: (i, 0))
    vec = pl.BlockSpec((1, d), lambda i, dst: (0, 0))
    normed = jax.ShapeDtypeStruct((t, d), next_dtype)
    return pl.pallas_call(
        functools.partial(_combine_kernel, emit_x=emit_x),
        grid_spec=pltpu.PrefetchScalarGridSpec(
            num_scalar_prefetch=1,
            grid=(t // tm,),
            in_specs=[tile, pl.BlockSpec((tm, LANES), lambda i, dst: (i, 0)), vec, vec, vec, vec,
                      pl.BlockSpec(memory_space=pl.ANY)],
            out_specs=[tile, tile] if emit_x else tile,
            scratch_shapes=[pltpu.VMEM((2, TOP_K, tm, d), F32), pltpu.SemaphoreType.DMA((2,))],
        ),
        out_shape=[jax.ShapeDtypeStruct((t, d), F32), normed] if emit_x else normed,
        compiler_params=_params("arbitrary"),
        name="moe_combine",
    )(dest.reshape(-1), x, gates, gate2, *next_norm, y)


def moe_block(x, norm_w, sc, sh, gate2, layer, w_router, b_router, w_gate_up, b_gate_up, w_down, b_down,
              next_norm, next_dtype, emit_x):
    t, d = x.shape
    tm = MOE_TILE
    h, idx, gates, pos, counts = moe_router(x, norm_w, sc, sh, w_router, b_router)
    cnt = counts[0, :N_EXPERTS].astype(I32)
    padded = (cnt + tm - 1) // tm * tm
    pend = jnp.cumsum(padded)
    pstart = pend - padded
    n_tiles = (t * TOP_K + N_EXPERTS * (tm - 1) + tm - 1) // tm
    is_expert = idx[:, :TOP_K, None] == jnp.arange(N_EXPERTS, dtype=I32)
    dest = pos[:, :TOP_K] + jnp.sum(jnp.where(is_expert, pstart, 0), axis=-1)
    tile_start = jnp.arange(n_tiles, dtype=I32) * tm
    tile_expert = jnp.minimum(jnp.sum((pend[None, :] <= tile_start[:, None]).astype(I32), axis=1), N_EXPERTS - 1)
    n_used = (pend[-1:] // tm).astype(I32)
    zero_plan = jnp.concatenate([jnp.where(padded > 0, pend - tm, -1), n_used]).astype(I32)
    xs = moe_dispatch(dest, zero_plan, h, n_tiles * tm)
    experts = jnp.arange(N_EXPERTS, dtype=I32)
    group_tiles = padded // tm
    nonempty = group_tiles > 0
    later = (experts[None, :] > experts[:, None]) & nonempty[None, :]
    earlier = (experts[None, :] < experts[:, None]) & nonempty[None, :]
    next_e = jnp.min(jnp.where(later, experts[None, :], N_EXPERTS), axis=1)
    prev_e = jnp.max(jnp.where(earlier, experts[None, :], -1), axis=1)
    next_e = jnp.where(next_e < N_EXPERTS, next_e, -1)
    prev_tiles = jnp.where(prev_e >= 0, group_tiles[jnp.maximum(prev_e, 0)], 0)
    ordinal = jnp.cumsum(nonempty.astype(I32)) - nonempty.astype(I32)
    tile_in_group = jnp.arange(n_tiles, dtype=I32) - (pstart // tm)[tile_expert]
    rows_in_tile = jnp.clip(cnt[tile_expert] - tile_in_group * tm, 1, tm)
    row_blocks = (rows_in_tile + ROW_BLOCK - 1) // ROW_BLOCK
    plan = tuple(a.astype(I32) for a in (tile_expert, tile_in_group, prev_tiles[tile_expert], next_e[tile_expert],
                                         ordinal[tile_expert], row_blocks, n_used))
    y = moe_experts(plan, xs, layer, w_gate_up, b_gate_up, w_down, b_down)
    return moe_combine(dest, x, gates, gate2, y, next_norm, next_dtype, emit_x)


def kernel(x, c, ada_w, ada_b, norm_w, dn_w_in, dn_conv_w, dn_a_log, dn_dt_bias, dn_o_norm_w, dn_w_out, sgu_w_in, sgu_b_in, sgu_ln_w, sgu_ln_b, sgu_w_sp, sgu_b_sp, sgu_w_out, sgu_b_out, moe_w_router, moe_b_router, moe_w_gate_up, moe_b_gate_up, moe_w_down, moe_b_down, final_norm_w):
    bsz, seq, d = x.shape
    assert bsz == 1 and d == D_MODEL
    xt = x.reshape(seq, d)
    mod = ada_mod(c, ada_w, ada_b)
    mods = [[mod[i:i + 1, s * d:(s + 1) * d] for s in range(6)] for i in range(DEPTH)]
    zero = jnp.zeros((1, d), F32)
    h = norm_mod(xt, norm_w[0, 0:1], mods[0][1], mods[0][0], BF16)
    for i in range(DEPTH):
        sh1, sc1, gt1, sh2, sc2, gt2 = mods[i]
        last = i == DEPTH - 1
        next_norm = (final_norm_w.reshape(1, d), zero, zero) if last else (norm_w[i + 1, 0:1], mods[i + 1][1], mods[i + 1][0])
        j = i // 2
        if i % 2 == 0:
            xt = gated_deltanet_block(xt, h, gt1, j, dn_w_in, dn_conv_w[j], dn_a_log[j], dn_dt_bias[j],
                                      dn_o_norm_w[j], dn_w_out)
        else:
            xt = chunked_gmlp_block(xt, h, gt1, j, sgu_w_in, sgu_b_in[j], sgu_ln_w[j], sgu_ln_b[j], sgu_w_sp[j],
                                    sgu_b_sp[j], sgu_w_out, sgu_b_out[j])
        res = moe_block(xt, norm_w[i, 1:2], sc2, sh2, gt2, i, moe_w_router[i], moe_b_router[i],
                        moe_w_gate_up, moe_b_gate_up[i], moe_w_down, moe_b_down[i],
                        next_norm, F32 if last else BF16, not last)
        if last:
            return res.reshape(bsz, seq, d)
        xt, h = res
```

```python
import functools

import jax
import jax.numpy as jnp
from jax import lax
from jax.experimental import pallas as pl
from jax.experimental.pallas import tpu as pltpu

F32 = jnp.float32
BF16 = jnp.bfloat16
I32 = jnp.int32
HIGHEST = lax.Precision.HIGHEST

D_MODEL = 2048
DEPTH = 2
NORM_EPS = 1e-6
DN_HEAD_DIM = 128
DN_QK_HEADS = 16
DN_V_HEADS = 32
DN_QK_DIM = 2048
DN_V_DIM = 4096
DN_CONV_CH = 8192
DN_CONV = 4
DN_CHUNK = 64
SGU_WIDTH = 4096
SGU_CHUNK = 128
SGU_GROUPS = 32
N_EXPERTS = 32
TOP_K = 4
EXPERT_DIM = 2048
SWIGLU_LIMIT = 7.0
SWIGLU_ALPHA = 1.702

LANES = 128
VMEM_LIMIT = 58 * 1024 * 1024
MOE_TILE = 512
ROW_BLOCK = 128
GDN_TILE = 2048


def _params(*sem):
    return pltpu.CompilerParams(dimension_semantics=sem, vmem_limit_bytes=VMEM_LIMIT)


def _iota(shape, dim):
    return lax.broadcasted_iota(I32, shape, dim)


def _sigmoid(x):
    return 1.0 / (1.0 + jnp.exp(-x))


def _silu(x):
    return x * _sigmoid(x)


def _ada_kernel(c_ref, w_ref, b_ref, o_ref):
    c = c_ref[...]
    o_ref[...] = jnp.dot(_silu(c), w_ref[...], precision=HIGHEST, preferred_element_type=F32) + b_ref[...]


def ada_mod(c, ada_w, ada_b):
    depth, d, n = ada_w.shape
    tn = 1024
    c8 = jnp.broadcast_to(c, (8, d))
    out = pl.pallas_call(
        _ada_kernel,
        grid=(depth, n // tn),
        in_specs=[
            pl.BlockSpec((8, d), lambda l, j: (0, 0)),
            pl.BlockSpec((None, d, tn), lambda l, j: (l, 0, j)),
            pl.BlockSpec((None, 1, tn), lambda l, j: (l, 0, j)),
        ],
        out_specs=pl.BlockSpec((None, 8, tn), lambda l, j: (l, 0, j)),
        out_shape=jax.ShapeDtypeStruct((depth, 8, n), F32),
        compiler_params=_params("parallel", "parallel"),
        name="ada_mod",
    )(c8, ada_w, ada_b.reshape(depth, 1, n))
    return out[:, 0, :]


def _norm_mod(x, w, sc, sh):
    y = x * lax.rsqrt(jnp.mean(x * x, axis=-1, keepdims=True) + NORM_EPS)
    return (y * w) * (1.0 + sc) + sh


def _norm_mod_kernel(x_ref, w_ref, sc_ref, sh_ref, o_ref):
    o_ref[...] = _norm_mod(x_ref[...], w_ref[...], sc_ref[...], sh_ref[...]).astype(o_ref.dtype)


def norm_mod(x, w, sc, sh, out_dtype):
    t, d = x.shape
    tm = 512
    vec = pl.BlockSpec((1, d), lambda i: (0, 0))
    return pl.pallas_call(
        _norm_mod_kernel,
        grid=(t // tm,),
        in_specs=[pl.BlockSpec((tm, d), lambda i: (i, 0)), vec, vec, vec],
        out_specs=pl.BlockSpec((tm, d), lambda i: (i, 0)),
        out_shape=jax.ShapeDtypeStruct((t, d), out_dtype),
        compiler_params=_params("parallel"),
        name="norm_mod",
    )(x, w, sc, sh)


def _gelu_exact(x):
    return 0.5 * x * (1.0 + lax.erf(x * (2.0 ** -0.5)))


def _mm_kernel(a_ref, w_hbm, *refs, mode, layer):
    *refs, wb, stage, sem = refs
    j = pl.program_id(0)
    tn = stage.shape[1]

    def w_copy(jj):
        cols = pl.ds(pl.multiple_of(jj * tn, tn), tn)
        src = w_hbm.at[:, cols] if layer is None else w_hbm.at[layer, :, cols]
        return pltpu.make_async_copy(src, stage, sem)

    @pl.when(pl.program_id(1) == 0)
    def _():
        @pl.when(j == 0)
        def _():
            w_copy(0).start()

        w_copy(j).wait()
        wb[...] = stage[...].astype(BF16)

        @pl.when(j + 1 < pl.num_programs(0))
        def _():
            w_copy(j + 1).start()

    acc = jnp.dot(a_ref[...], wb[...], preferred_element_type=F32)
    if mode == "plain":
        (o_ref,) = refs
        o_ref[...] = acc.astype(o_ref.dtype)
    elif mode == "bias_gelu":
        b_ref, o_ref = refs
        o_ref[...] = _gelu_exact(acc + b_ref[...]).astype(o_ref.dtype)
    else:
        b_ref, res_ref, gate_ref, o_ref = refs
        o_ref[...] = res_ref[...] + gate_ref[...] * (acc + b_ref[...])


def matmul(a, w, n, *, mode, out_dtype, tm, tn, layer=None, bias=None, res=None, gate=None):
    m, k = a.shape
    assert m % tm == 0 and n % tn == 0 and w.shape[-1] >= n
    row = pl.BlockSpec((1, tn), lambda j, i: (0, j))
    in_specs = [pl.BlockSpec((tm, k), lambda j, i: (i, 0)), pl.BlockSpec(memory_space=pl.ANY)]
    args = [a, w]
    if mode != "plain":
        in_specs.append(row)
        args.append(bias)
    if mode == "residual":
        in_specs += [pl.BlockSpec((tm, tn), lambda j, i: (i, j)), row]
        args += [res, gate]
    return pl.pallas_call(
        functools.partial(_mm_kernel, mode=mode, layer=layer),
        grid=(n // tn, m // tm),
        in_specs=in_specs,
        out_specs=pl.BlockSpec((tm, tn), lambda j, i: (i, j)),
        out_shape=jax.ShapeDtypeStruct((m, n), out_dtype),
        scratch_shapes=[pltpu.VMEM((k, tn), BF16), pltpu.VMEM((k, tn), F32), pltpu.SemaphoreType.DMA(())],
        compiler_params=_params("arbitrary", "arbitrary"),
        name="matmul_" + mode,
    )(*args)


def _softplus(x):
    return jnp.maximum(x, 0.0) + jnp.log1p(jnp.exp(-jnp.abs(x)))


def _unit_lower_inverses(lows, n_block):
    n = lows[0].shape[0]
    r = _iota((n, n), 0)
    c = _iota((n, n), 1)
    eye = jnp.where(r == c, 1.0, 0.0).astype(F32)

    def mm(a, b):
        return jnp.dot(a, b, preferred_element_type=F32)

    def bf(ms):
        return [m.astype(BF16) for m in ms]

    base = (r // 8) == (c // 8)
    lows_b = bf(lows)
    ds = [jnp.where(base, low, 0.0) for low in lows]
    ds_b = [jnp.where(base, low, jnp.zeros_like(low)) for low in lows_b]
    xs = [eye - d for d in ds]
    ps_b = bf([mm(d, d) for d in ds_b])
    xs = [x + mm(xb, p) for x, xb, p in zip(xs, bf(xs), ps_b)]
    ps_b = bf([mm(p, p) for p in ps_b])
    xs = [x + mm(xb, p) for x, xb, p in zip(xs, bf(xs), ps_b)]
    s = 8
    while s < n_block:
        pair = ((r // (2 * s)) == (c // (2 * s))) & ((r // s) != (c // s))
        xs_b = bf(xs)
        ts_b = bf([mm(jnp.where(pair, low, jnp.zeros_like(low)), xb) for low, xb in zip(lows_b, xs_b)])
        xs = [x - mm(xb, t) for x, xb, t in zip(xs, xs_b, ts_b)]
        s *= 2
    return xs


def _gdn_prep_kernel(q_ref, k_ref, v_ref, qh_ref, kh_ref, vh_ref, wq_ref, wk_ref, wv_ref, ab_ref, alog_ref, dtb_ref,
                     qg_ref, kd_ref, u_ref, w_ref, a_ref, gcum_ref,
                     win_q, win_k, win_v, gcum_s, gtot_s, beta_s):
    i = pl.program_id(0)
    h = pl.program_id(1)
    tm = q_ref.shape[0]
    halo = qh_ref.shape[0]
    cs = DN_CHUNK
    hd = DN_HEAD_DIM

    def conv_silu(x_ref, halo_ref, w_ref, win_ref):
        hal = halo_ref[...].astype(F32)
        win_ref[0:halo, :] = jnp.where(i == 0, jnp.zeros_like(hal), hal)
        win_ref[halo:halo + tm, :] = x_ref[...].astype(F32)
        w = w_ref[...]
        acc = None
        for j in range(DN_CONV):
            start = halo - (DN_CONV - 1) + j
            term = win_ref[start:start + tm, :] * w[j:j + 1, :]
            acc = term if acc is None else acc + term
        return _silu(acc)

    def l2norm(x):
        return x * lax.rsqrt(jnp.sum(x * x, axis=-1, keepdims=True) + NORM_EPS)

    @pl.when(h == 0)
    def _():
        ab = ab_ref[...]
        g = -jnp.exp(alog_ref[...]) * _softplus(ab + dtb_ref[...])
        sub = 4 * DN_CHUNK
        r = _iota((sub, sub), 0)
        c = _iota((sub, sub), 1)
        same = (r // DN_CHUNK) == (c // DN_CHUNK)
        tri = jnp.where(same & (c <= r), 1.0, 0.0).astype(F32)
        blk = jnp.where(same, 1.0, 0.0).astype(F32)
        for s0 in range(0, tm, sub):
            g_sub = g[s0:s0 + sub]
            gcum = jnp.dot(tri, g_sub, precision=HIGHEST, preferred_element_type=F32)
            gcum_s[s0:s0 + sub, :] = gcum
            gcum_ref[s0:s0 + sub, :] = gcum
            gtot_s[s0:s0 + sub, :] = jnp.dot(blk, g_sub, precision=HIGHEST, preferred_element_type=F32)
        beta_s[...] = _sigmoid(ab)

    q = l2norm(conv_silu(q_ref, qh_ref, wq_ref, win_q)) * (DN_HEAD_DIM ** -0.5)
    k = l2norm(conv_silu(k_ref, kh_ref, wk_ref, win_k))
    v = conv_silu(v_ref, vh_ref, wv_ref, win_v)

    def head_gates(ref, first_lane):
        rolled = pltpu.roll(ref[...], (LANES - first_lane) % LANES, 1)
        return [jnp.broadcast_to(rolled[:, s:s + 1], (tm, hd)) for s in range(2)]

    gc = head_gates(gcum_s, 2 * h)
    gt = head_gates(gtot_s, 2 * h)
    be = head_gates(beta_s, DN_V_HEADS + 2 * h)
    vb, kbg = [], []
    for s in range(2):
        cols = slice(s * hd, (s + 1) * hd)
        eg = jnp.exp(gc[s])
        qg_ref[:, cols] = (q * eg).astype(BF16)
        kd_ref[:, cols] = (k * jnp.exp(gt[s] - gc[s])).astype(BF16)
        vb.append((v[:, cols] * be[s]).astype(BF16))
        kbg.append((k * be[s] * eg).astype(BF16))
    qb = q.astype(BF16)
    kb = k.astype(BF16)

    n2 = 2 * cs
    r = _iota((n2, n2), 0)
    c = _iota((n2, n2), 1)
    same_head = (r // cs) == (c // cs)
    causal = same_head & (c <= r)
    strict = same_head & (c < r)
    top = _iota((cs, n2), 1) < cs
    nt = (((1,), (1,)), ((), ()))
    chunks = [slice(ci * cs, (ci + 1) * cs) for ci in range(tm // cs)]

    def stack_heads(x, rows):
        return jnp.concatenate([x[0][rows], x[1][rows]], axis=0)

    k_st = [jnp.concatenate([kb[rows], kb[rows]], axis=0) for rows in chunks]
    q_st = [jnp.concatenate([qb[rows], qb[rows]], axis=0) for rows in chunks]
    kk = [lax.dot_general(ks, ks, nt, preferred_element_type=F32) for ks in k_st]
    qk = [lax.dot_general(qs, ks, nt, preferred_element_type=F32) for qs, ks in zip(q_st, k_st)]
    g_col = [stack_heads(gc, rows) for rows in chunks]
    b_col = [stack_heads(be, rows) for rows in chunks]
    decay = [jnp.exp(jnp.where(causal, g - g.T, 0.0)) for g in g_col]
    lows = [jnp.where(strict, b * kk_i * d, 0.0) for b, kk_i, d in zip(b_col, kk, decay)]
    tinvs = [t.astype(BF16) for t in _unit_lower_inverses(lows, cs)]
    us = [jnp.dot(t, stack_heads(vb, rows), preferred_element_type=F32).astype(BF16) for t, rows in zip(tinvs, chunks)]
    ws = [jnp.dot(t, stack_heads(kbg, rows), preferred_element_type=F32).astype(BF16) for t, rows in zip(tinvs, chunks)]
    for rows, u2, w2, qk_i, d in zip(chunks, us, ws, qk, decay):
        intra = jnp.where(causal, qk_i * d, 0.0)
        a_ref[rows, :] = jnp.where(top, intra[:cs], intra[cs:]).astype(BF16)
        u_ref[rows, :hd] = u2[:cs]
        u_ref[rows, hd:] = u2[cs:]
        w_ref[rows, :hd] = w2[:cs]
        w_ref[rows, hd:] = w2[cs:]


def gdn_prep(proj, ab, conv_w, a_log, dt_bias):
    t = proj.shape[0]
    tm = GDN_TILE
    halo = 16
    hb = tm // halo
    hd = DN_HEAD_DIM
    nq = DN_QK_HEADS

    def halo_map(off):
        return lambda i, h: (jnp.maximum(i * hb - 1, 0), off + h)

    pad = jnp.zeros((1, LANES - DN_V_HEADS), F32)
    alog = jnp.concatenate([a_log.reshape(1, -1), pad], axis=1)
    dtb = jnp.concatenate([dt_bias.reshape(1, -1), pad], axis=1)
    big = jax.ShapeDtypeStruct((t, DN_V_DIM), BF16)
    small = jax.ShapeDtypeStruct((t, DN_QK_DIM), BF16)
    gshape = jax.ShapeDtypeStruct((t, LANES), F32)
    big_spec = pl.BlockSpec((tm, 2 * hd), lambda i, h: (i, h))
    small_spec = pl.BlockSpec((tm, hd), lambda i, h: (i, h))
    g_spec = pl.BlockSpec((tm, LANES), lambda i, h: (i, 0))
    row = pl.BlockSpec((1, LANES), lambda i, h: (0, 0))
    return pl.pallas_call(
        _gdn_prep_kernel,
        grid=(t // tm, nq),
        in_specs=[
            pl.BlockSpec((tm, hd), lambda i, h: (i, h)),
            pl.BlockSpec((tm, hd), lambda i, h: (i, nq + h)),
            pl.BlockSpec((tm, 2 * hd), lambda i, h: (i, nq + h)),
            pl.BlockSpec((halo, hd), halo_map(0)),
            pl.BlockSpec((halo, hd), halo_map(nq)),
            pl.BlockSpec((halo, 2 * hd), lambda i, h: (jnp.maximum(i * hb - 1, 0), nq + h)),
            pl.BlockSpec((DN_CONV, hd), lambda i, h: (0, h)),
            pl.BlockSpec((DN_CONV, hd), lambda i, h: (0, nq + h)),
            pl.BlockSpec((DN_CONV, 2 * hd), lambda i, h: (0, nq + h)),
            g_spec, row, row,
        ],
        out_specs=[big_spec, big_spec, big_spec, big_spec, small_spec, g_spec],
        out_shape=[big, big, big, big, small, gshape],
        scratch_shapes=[
            pltpu.VMEM((tm + halo, hd), F32), pltpu.VMEM((tm + halo, hd), F32), pltpu.VMEM((tm + halo, 2 * hd), F32),
            pltpu.VMEM((tm, LANES), F32), pltpu.VMEM((tm, LANES), F32), pltpu.VMEM((tm, LANES), F32),
        ],
        compiler_params=_params("parallel", "arbitrary"),
        name="gdn_prep",
    )(proj, proj, proj, proj, proj, proj, conv_w, conv_w, conv_w, ab, alog, dtb)


def _gdn_scan_kernel(glast_ref, qg_ref, kd_ref, u_ref, w_ref, a_ref, z_ref, onw_ref, og_ref, s_ref):
    ci = pl.program_id(0)
    cs = DN_CHUNK
    hd = DN_HEAD_DIM
    nh = DN_V_HEADS

    @pl.when(ci == 0)
    def _():
        s_ref[...] = jnp.zeros_like(s_ref)

    onw = onw_ref[...]
    tn = (((0,), (0,)), ((), ()))
    lane = _iota((cs, 2 * cs), 1)
    cols = [slice(hv * hd, (hv + 1) * hd) for hv in range(nh)]
    states = [s_ref[hv] for hv in range(nh)]
    ws = [jnp.dot(jnp.concatenate([w_ref[:, cols[hv]], qg_ref[:, cols[hv]]], axis=0), states[hv].astype(BF16),
                  preferred_element_type=F32) for hv in range(nh)]
    v_new = [(u_ref[:, cols[hv]].astype(F32) - ws[hv][:cs]).astype(BF16) for hv in range(nh)]
    for hv in range(nh):
        decay = jnp.exp(jnp.full((1, hd), glast_ref[ci, hv], F32))
        s_ref[hv] = states[hv] * decay + lax.dot_general(kd_ref[:, cols[hv]], v_new[hv], tn, preferred_element_type=F32)
    for pair in range(nh // 2):
        a2 = a_ref[:, pair * 2 * cs:(pair + 1) * 2 * cs]
        v2 = jnp.concatenate([v_new[2 * pair], v_new[2 * pair + 1]], axis=0)
        for s in range(2):
            hv = 2 * pair + s
            a_s = jnp.where((lane // cs) == s, a2, jnp.zeros_like(a2))
            o = ws[hv][cs:] + jnp.dot(a_s, v2, preferred_element_type=F32)
            on = o * lax.rsqrt(jnp.mean(o * o, axis=-1, keepdims=True) + NORM_EPS) * onw
            og_ref[:, cols[hv]] = (on * _silu(z_ref[:, cols[hv]].astype(F32))).astype(BF16)


def gdn_scan(glast, qg, kd, u, w, intra, proj, o_norm_w):
    t = qg.shape[0]
    cs = DN_CHUNK
    big_spec = pl.BlockSpec((cs, DN_V_DIM), lambda c, g: (c, 0))
    return pl.pallas_call(
        _gdn_scan_kernel,
        grid_spec=pltpu.PrefetchScalarGridSpec(
            num_scalar_prefetch=1,
            grid=(t // cs,),
            in_specs=[big_spec, big_spec, big_spec, big_spec,
                      pl.BlockSpec((cs, DN_V_HEADS * cs), lambda c, g: (c, 0)),
                      pl.BlockSpec((cs, DN_V_DIM), lambda c, g: (c, DN_CONV_CH // DN_V_DIM)),
                      pl.BlockSpec((1, DN_HEAD_DIM), lambda c, g: (0, 0))],
            out_specs=big_spec,
            scratch_shapes=[pltpu.VMEM((DN_V_HEADS, DN_HEAD_DIM, DN_HEAD_DIM), F32)],
        ),
        out_shape=jax.ShapeDtypeStruct((t, DN_V_DIM), BF16),
        compiler_params=_params("arbitrary"),
        name="gdn_scan",
    )(glast, qg, kd, u, w, intra, proj, o_norm_w.reshape(1, -1))


def gated_deltanet_block(x, h, gate, layer, w_in, conv_w, a_log, dt_bias, o_norm_w, w_out):
    t = x.shape[0]
    n_main = DN_CONV_CH + DN_V_DIM
    w_ab = jnp.pad(w_in[layer, :, n_main:], ((0, 0), (0, LANES - 2 * DN_V_HEADS)))
    proj = matmul(h, w_in, n_main, layer=layer, mode="plain", out_dtype=BF16, tm=2048, tn=1024)
    ab = matmul(h, w_ab, LANES, mode="plain", out_dtype=F32, tm=1024, tn=LANES)
    qg, kd, u, w, intra, gcum = gdn_prep(proj, ab, conv_w, a_log, dt_bias)
    glast = gcum.reshape(t // DN_CHUNK, DN_CHUNK, LANES)[:, DN_CHUNK - 1, :DN_V_HEADS]
    og = gdn_scan(glast, qg, kd, u, w, intra, proj, o_norm_w)
    zero_bias = jnp.zeros((1, D_MODEL), F32)
    return matmul(og, w_out, D_MODEL, layer=layer, mode="residual", out_dtype=F32, tm=1024, tn=512,
                  bias=zero_bias, res=x, gate=gate)


def _sgu_spatial_kernel(u_ref, v_ref, lnw_ref, lnb_ref, wsp_ref, bsp_ref, o_ref, wm_ref):
    cs = SGU_CHUNK
    gd = SGU_WIDTH // SGU_GROUPS

    @pl.when(pl.program_id(0) == 0)
    def _():
        r = _iota((cs, cs), 0)
        c = _iota((cs, cs), 1)
        for g in range(SGU_GROUPS):
            wm_ref[g] = jnp.where(c <= r, wsp_ref[g], 0.0).astype(BF16)

    v = v_ref[...].astype(F32)
    mu = jnp.mean(v, axis=-1, keepdims=True)
    var = jnp.mean(jnp.square(v - mu), axis=-1, keepdims=True)
    vn = ((v - mu) * lax.rsqrt(var + NORM_EPS) * lnw_ref[...] + lnb_ref[...]).astype(BF16)
    bsp = bsp_ref[...]
    for g in range(SGU_GROUPS):
        cols = slice(g * gd, (g + 1) * gd)
        wg = wm_ref[g]
        for ci in range(v_ref.shape[0] // cs):
            rows = slice(ci * cs, (ci + 1) * cs)
            sp = jnp.dot(wg, vn[rows, cols], preferred_element_type=F32) + bsp[:, g:g + 1]
            o_ref[rows, cols] = (u_ref[rows, cols].astype(F32) * sp).astype(BF16)


def sgu_spatial(zz, ln_w, ln_b, w_sp, b_sp):
    t = zz.shape[0]
    cs = SGU_CHUNK
    wd = SGU_WIDTH
    bsp_t = jnp.pad(b_sp.T, ((0, 0), (0, LANES - SGU_GROUPS)))
    row = pl.BlockSpec((1, wd), lambda i: (0, 0))
    tm = 4 * cs
    return pl.pallas_call(
        _sgu_spatial_kernel,
        grid=(t // tm,),
        in_specs=[
            pl.BlockSpec((tm, wd), lambda i: (i, 0)),
            pl.BlockSpec((tm, wd), lambda i: (i, 1)),
            row, row,
            pl.BlockSpec((SGU_GROUPS, cs, cs), lambda i: (0, 0, 0)),
            pl.BlockSpec((cs, LANES), lambda i: (0, 0)),
        ],
        out_specs=pl.BlockSpec((tm, wd), lambda i: (i, 0)),
        out_shape=jax.ShapeDtypeStruct((t, wd), BF16),
        scratch_shapes=[pltpu.VMEM((SGU_GROUPS, cs, cs), BF16)],
        compiler_params=_params("arbitrary"),
        name="sgu_spatial",
    )(zz, zz, ln_w.reshape(1, -1), ln_b.reshape(1, -1), w_sp, bsp_t)


def chunked_gmlp_block(x, h, gate, layer, w_in, b_in, ln_w, ln_b, w_sp, b_sp, w_out, b_out):
    zz = matmul(h, w_in, 2 * SGU_WIDTH, layer=layer, mode="bias_gelu", out_dtype=BF16, tm=2048, tn=1024,
                bias=b_in.reshape(1, -1))
    su = sgu_spatial(zz, ln_w, ln_b, w_sp, b_sp)
    return matmul(su, w_out, D_MODEL, layer=layer, mode="residual", out_dtype=F32, tm=1024, tn=512,
                  bias=b_out.reshape(1, -1), res=x, gate=gate)


def _dot_split3(a, b):
    a_hi = a.astype(BF16)
    b_hi = b.astype(BF16)
    a_lo = (a - a_hi.astype(F32)).astype(BF16)
    b_lo = (b - b_hi.astype(F32)).astype(BF16)
    return (jnp.dot(a_hi, b_hi, preferred_element_type=F32)
            + (jnp.dot(a_lo, b_hi, preferred_element_type=F32) + jnp.dot(a_hi, b_lo, preferred_element_type=F32)))


def _router_kernel(x_ref, w_ref, sc_ref, sh_ref, wr_ref, br_ref, h_ref, idx_ref, gate_ref, pos_ref, cnt_ref, carry):
    i = pl.program_id(0)
    tm = x_ref.shape[0]

    @pl.when(i == 0)
    def _():
        carry[...] = jnp.zeros_like(carry)

    h = _norm_mod(x_ref[...], w_ref[...], sc_ref[...], sh_ref[...])
    h_ref[...] = h
    lane = _iota((tm, LANES), 1).astype(F32)
    neg = jnp.float32(-jnp.inf)
    logits = _dot_split3(h, wr_ref[...]) + br_ref[...]
    logits = jnp.where(lane < N_EXPERTS, logits, neg)
    vals, idxs = [], []
    for _ in range(TOP_K):
        m = jnp.max(logits, axis=-1, keepdims=True)
        ix = jnp.min(jnp.where(logits == m, lane, float(LANES)), axis=-1, keepdims=True)
        vals.append(m)
        idxs.append(ix)
        logits = jnp.where(lane == ix, neg, logits)
    es = [jnp.exp(v - vals[0]) for v in vals]
    denom = es[0] + es[1] + es[2] + es[3]
    multi = jnp.zeros((tm, LANES), F32)
    for ix in idxs:
        multi = jnp.where(lane == ix, 1.0, multi)
    r = _iota((tm, tm), 0)
    c = _iota((tm, tm), 1)
    before = jnp.where(c < r, 1.0, 0.0).astype(BF16)
    rank = jnp.dot(before, multi.astype(BF16), preferred_element_type=F32) + carry[0:1, :]
    idx_t = jnp.zeros((tm, LANES), F32)
    gate_t = jnp.zeros((tm, LANES), F32)
    pos_t = jnp.zeros((tm, LANES), F32)
    for kk in range(TOP_K):
        pk = jnp.sum(jnp.where(lane == idxs[kk], rank, 0.0), axis=-1, keepdims=True)
        idx_t = jnp.where(lane == kk, idxs[kk], idx_t)
        gate_t = jnp.where(lane == kk, es[kk] / denom, gate_t)
        pos_t = jnp.where(lane == kk, pk, pos_t)
    idx_ref[...] = idx_t.astype(I32)
    gate_ref[...] = gate_t
    pos_ref[...] = pos_t.astype(I32)
    carry[...] = carry[...] + jnp.sum(multi, axis=0, keepdims=True)
    cnt_ref[...] = carry[...]


def moe_router(x, w, sc, sh, w_router, b_router):
    t, d = x.shape
    tm = 1024
    wr = jnp.pad(w_router, ((0, 0), (0, LANES - N_EXPERTS)))
    br = jnp.pad(b_router.reshape(1, -1), ((0, 0), (0, LANES - N_EXPERTS)))
    vec = pl.BlockSpec((1, d), lambda i: (0, 0))
    tile = pl.BlockSpec((tm, LANES), lambda i: (i, 0))
    return pl.pallas_call(
        _router_kernel,
        grid=(t // tm,),
        in_specs=[pl.BlockSpec((tm, d), lambda i: (i, 0)), vec, vec, vec,
                  pl.BlockSpec((d, LANES), lambda i: (0, 0)), pl.BlockSpec((1, LANES), lambda i: (0, 0))],
        out_specs=[pl.BlockSpec((tm, d), lambda i: (i, 0)), tile, tile, tile, pl.BlockSpec((8, LANES), lambda i: (0, 0))],
        out_shape=[jax.ShapeDtypeStruct((t, d), F32), jax.ShapeDtypeStruct((t, LANES), I32),
                   jax.ShapeDtypeStruct((t, LANES), F32), jax.ShapeDtypeStruct((t, LANES), I32),
                   jax.ShapeDtypeStruct((8, LANES), F32)],
        scratch_shapes=[pltpu.VMEM((8, LANES), F32)],
        compiler_params=_params("arbitrary"),
        name="moe_router",
    )(x, w, sc, sh, wr, br)


def _dispatch_kernel(dest_ref, ztile_ref, h_ref, xs_ref, zbuf, sem, zsem):
    i = pl.program_id(0)
    tm = h_ref.shape[0]
    zt = zbuf.shape[0]

    @pl.when(i == 0)
    def _():
        zbuf[...] = jnp.zeros_like(zbuf)

        def zero_copy(e):
            row = pl.multiple_of(ztile_ref[e], zt)
            return pltpu.make_async_copy(zbuf, xs_ref.at[pl.ds(row, zt), :], zsem)

        for e in range(N_EXPERTS):
            @pl.when(ztile_ref[e] >= 0)
            def _():
                zero_copy(e).start()
        for e in range(N_EXPERTS):
            @pl.when(ztile_ref[e] >= 0)
            def _():
                zero_copy(e).wait()

        def zero_tail(j, carry):
            row = pl.multiple_of(j * zt, zt)
            cp = pltpu.make_async_copy(zbuf, xs_ref.at[pl.ds(row, zt), :], zsem)
            cp.start()
            cp.wait()
            return carry

        lax.fori_loop(ztile_ref[N_EXPERTS], xs_ref.shape[0] // zt, zero_tail, 0)

    def row_copy(r, d):
        return pltpu.make_async_copy(h_ref.at[pl.ds(r, 1), :], xs_ref.at[pl.ds(d, 1), :], sem)

    def issue(r, carry):
        base = (i * tm + r) * TOP_K
        for kk in range(TOP_K):
            row_copy(r, dest_ref[base + kk]).start(priority=kk % 2)
        return carry

    lax.fori_loop(0, tm, issue, 0, unroll=8)

    for kk in range(TOP_K):
        pltpu.make_async_copy(h_ref, xs_ref.at[pl.ds(0, tm), :], sem).wait()


def moe_dispatch(dest, zero_plan, h, n_rows):
    t, d = h.shape
    tm = 256
    return pl.pallas_call(
        _dispatch_kernel,
        grid_spec=pltpu.PrefetchScalarGridSpec(
            num_scalar_prefetch=2,
            grid=(t // tm,),
            in_specs=[pl.BlockSpec((tm, d), lambda i, dst, zt: (i, 0))],
            out_specs=pl.BlockSpec(memory_space=pl.ANY),
            scratch_shapes=[pltpu.VMEM((MOE_TILE, d), F32), pltpu.SemaphoreType.DMA(()), pltpu.SemaphoreType.DMA(())],
        ),
        out_shape=jax.ShapeDtypeStruct((n_rows, d), F32),
        compiler_params=_params("arbitrary"),
        name="moe_dispatch",
    )(dest.reshape(-1), zero_plan, h)


def _stream_expert_weights(w_hbm, wbuf, stage, sem, layer, e_cur, e_next, tig, gprev, cur, phase=0, phases=1):
    n_pairs = wbuf.shape[1] // 2
    cw = stage.shape[-1]
    prev_calls = gprev * phases
    q = tig * phases + phase

    def chunk_copy(e, c):
        return pltpu.make_async_copy(w_hbm.at[layer, e, :, pl.ds(c * cw, cw)], stage.at[c % 2], sem.at[c % 2])

    def start_pair(e, p):
        chunk_copy(e, 2 * p).start()
        chunk_copy(e, 2 * p + 1).start()

    def retire_pair(e, p, slot):
        for c in (2 * p, 2 * p + 1):
            chunk_copy(e, c).wait()
            wbuf[slot, c] = stage[c % 2].astype(BF16)

    if phase == 0:
        @pl.when(tig == 0)
        def _():
            for p in range(n_pairs):
                @pl.when(prev_calls <= p)
                def _():
                    start_pair(e_cur, p)

                @pl.when(prev_calls <= p + 1)
                def _():
                    retire_pair(e_cur, p, cur)

    @pl.when(e_next >= 0)
    def _():
        for j in range(phase, n_pairs + 1, phases):
            @pl.when(q == j)
            def _():
                if j >= 1:
                    retire_pair(e_next, j - 1, 1 - cur)
                if j < n_pairs:
                    start_pair(e_next, j)


def _for_row_bucket(n_blocks, tm, body):
    for k in range(1, tm // ROW_BLOCK + 1):
        @pl.when(n_blocks == k)
        def _():
            body(k * ROW_BLOCK)


def _ffn_up_kernel(te_ref, tig_ref, gprev_ref, nxt_ref, gidx_ref, nb_ref, nu_ref, x_ref, w_hbm, b_ref, o_ref,
                   wbuf, stage, sem, *, layer):
    i = pl.program_id(0)
    tm, f = o_ref.shape
    nc = wbuf.shape[1] // 2
    fc = wbuf.shape[-1]

    @pl.when(i < nu_ref[0])
    def _():
        cur = gidx_ref[i] % 2
        _stream_expert_weights(w_hbm, wbuf, stage, sem, layer, te_ref[i], nxt_ref[i], tig_ref[i], gprev_ref[i], cur)

        def compute(m):
            xb = x_ref[:m, :].astype(BF16)
            for c in range(nc):
                g = jnp.dot(xb, wbuf[cur, c], preferred_element_type=F32) + b_ref[:, c * fc:(c + 1) * fc]
                u = jnp.dot(xb, wbuf[cur, nc + c], preferred_element_type=F32) + b_ref[:, f + c * fc:f + (c + 1) * fc]
                x_glu = jnp.minimum(g, SWIGLU_LIMIT)
                x_lin = jnp.clip(u, -SWIGLU_LIMIT, SWIGLU_LIMIT)
                act = x_glu * _sigmoid(SWIGLU_ALPHA * x_glu) * (x_lin + 1.0)
                o_ref[:m, c * fc:(c + 1) * fc] = act.astype(BF16)
            if m < tm:
                o_ref[m:, :] = jnp.zeros((tm - m, f), BF16)

        _for_row_bucket(nb_ref[i], tm, compute)

    @pl.when(i >= nu_ref[0])
    def _():
        o_ref[...] = jnp.zeros_like(o_ref)


def _ffn_down_kernel(te_ref, tig_ref, gprev_ref, nxt_ref, gidx_ref, nb_ref, nu_ref, a_ref, w_hbm, b_ref, o_ref,
                     wbuf, stage, sem, *, layer):
    i = pl.program_id(0)
    tm, d = o_ref.shape
    fc = wbuf.shape[-1]

    @pl.when(i < nu_ref[0])
    def _():
        cur = gidx_ref[i] % 2
        _stream_expert_weights(w_hbm, wbuf, stage, sem, layer, te_ref[i], nxt_ref[i], tig_ref[i], gprev_ref[i], cur)

        def compute(m):
            a = a_ref[:m, :]
            for c in range(wbuf.shape[1]):
                cols = slice(c * fc, (c + 1) * fc)
                o_ref[:m, cols] = jnp.dot(a, wbuf[cur, c], preferred_element_type=F32) + b_ref[:, cols]
            if m < tm:
                o_ref[m:, :] = jnp.zeros((tm - m, d), F32)

        _for_row_bucket(nb_ref[i], tm, compute)

    @pl.when(i >= nu_ref[0])
    def _():
        o_ref[...] = jnp.zeros_like(o_ref)


def moe_experts(plan, xs, layer, w_gate_up, b_gate_up, w_down, b_down):
    n_rows, d = xs.shape
    tm = MOE_TILE
    n_tiles = n_rows // tm
    f = w_down.shape[2]
    cw = f // 4
    n_plan = len(plan)
    expert_row = lambda i, te, *_: (te[i], 0, 0)
    any_spec = pl.BlockSpec(memory_space=pl.ANY)
    act = pl.pallas_call(
        functools.partial(_ffn_up_kernel, layer=layer),
        grid_spec=pltpu.PrefetchScalarGridSpec(
            num_scalar_prefetch=n_plan,
            grid=(n_tiles,),
            in_specs=[pl.BlockSpec((tm, d), lambda i, *p: (jnp.minimum(i, p[-1][0] - 1), 0)),
                      any_spec,
                      pl.BlockSpec((None, 1, 2 * f), expert_row)],
            out_specs=pl.BlockSpec((tm, f), lambda i, *p: (i, 0)),
            scratch_shapes=[pltpu.VMEM((2, 2 * f // cw, d, cw), BF16), pltpu.VMEM((2, d, cw), F32),
                            pltpu.SemaphoreType.DMA((2,))],
        ),
        out_shape=jax.ShapeDtypeStruct((n_rows, f), BF16),
        compiler_params=_params("arbitrary"),
        name="moe_ffn_up",
    )(*plan, xs, w_gate_up, b_gate_up.reshape(N_EXPERTS, 1, 2 * f))
    return pl.pallas_call(
        functools.partial(_ffn_down_kernel, layer=layer),
        grid_spec=pltpu.PrefetchScalarGridSpec(
            num_scalar_prefetch=n_plan,
            grid=(n_tiles,),
            in_specs=[pl.BlockSpec((tm, f), lambda i, *p: (i, 0)),
                      any_spec,
                      pl.BlockSpec((None, 1, d), expert_row)],
            out_specs=pl.BlockSpec((tm, d), lambda i, *p: (i, 0)),
            scratch_shapes=[pltpu.VMEM((2, d // cw, f, cw), BF16), pltpu.VMEM((2, f, cw), F32),
                            pltpu.SemaphoreType.DMA((2,))],
        ),
        out_shape=jax.ShapeDtypeStruct((n_rows, d), F32),
        compiler_params=_params("arbitrary"),
        name="moe_ffn_down",
    )(*plan, act, w_down, b_down.reshape(N_EXPERTS, 1, d))


def _combine_kernel(dest_ref, x_ref, gate_ref, g2_ref, nw_ref, nsc_ref, nsh_ref, y_ref, *refs, emit_x):
    if emit_x:
        o_ref, hn_ref, ybuf, sem = refs
    else:
        hn_ref, ybuf, sem = refs
    i = pl.program_id(0)
    n = pl.num_programs(0)
    tm = x_ref.shape[0]

    def row_copy(tile, slot, r, kk):
        d = dest_ref[(tile * tm + r) * TOP_K + kk]
        return pltpu.make_async_copy(y_ref.at[pl.ds(d, 1), :], ybuf.at[slot, kk, pl.ds(r, 1), :], sem.at[slot])

    def issue_tile(tile, slot):
        def body(r, carry):
            for kk in range(TOP_K):
                row_copy(tile, slot, r, kk).start(priority=kk % 2)
            return carry
        lax.fori_loop(0, tm, body, 0, unroll=8)

    def drain_tile(slot):
        for kk in range(TOP_K):
            pltpu.make_async_copy(y_ref.at[pl.ds(0, tm), :], ybuf.at[slot, kk], sem.at[slot]).wait()

    slot = i % 2

    @pl.when(i == 0)
    def _():
        issue_tile(0, 0)

    @pl.when(i + 1 < n)
    def _():
        issue_tile(i + 1, 1 - slot)

    drain_tile(slot)
    gates = gate_ref[...]
    acc = ybuf[slot, 0] * gates[:, 0:1]
    for kk in range(1, TOP_K):
        acc = acc + ybuf[slot, kk] * gates[:, kk:kk + 1]
    x_new = x_ref[...] + g2_ref[...] * acc
    if emit_x:
        o_ref[...] = x_new
    hn_ref[...] = _norm_mod(x_new, nw_ref[...], nsc_ref[...], nsh_ref[...]).astype(hn_ref.dtype)


def moe_combine(dest, x, gates, gate2, y, next_norm, next_dtype, emit_x):
    t, d = x.shape
    tm = 256
    tile = pl.BlockSpec((tm, d), lambda i, dst: (i, 0))
    vec = pl.BlockSpec((1, d), lambda i, dst: (0, 0))
    normed = jax.ShapeDtypeStruct((t, d), next_dtype)
    return pl.pallas_call(
        functools.partial(_combine_kernel, emit_x=emit_x),
        grid_spec=pltpu.PrefetchScalarGridSpec(
            num_scalar_prefetch=1,
            grid=(t // tm,),
            in_specs=[tile, pl.BlockSpec((tm, LANES), lambda i, dst: (i, 0)), vec, vec, vec, vec,
                      pl.BlockSpec(memory_space=pl.ANY)],
            out_specs=[tile, tile] if emit_x else tile,
            scratch_shapes=[pltpu.VMEM((2, TOP_K, tm, d), F32), pltpu.SemaphoreType.DMA((2,))],
        ),
        out_shape=[jax.ShapeDtypeStruct((t, d), F32), normed] if emit_x else normed,
        compiler_params=_params("arbitrary"),
        name="moe_combine",
    )(dest.reshape(-1), x, gates, gate2, *next_norm, y)


def moe_block(x, norm_w, sc, sh, gate2, layer, w_router, b_router, w_gate_up, b_gate_up, w_down, b_down,
              next_norm, next_dtype, emit_x):
    t, d = x.shape
    tm = MOE_TILE
    h, idx, gates, pos, counts = moe_router(x, norm_w, sc, sh, w_router, b_router)
    cnt = counts[0, :N_EXPERTS].astype(I32)
    padded = (cnt + tm - 1) // tm * tm
    pend = jnp.cumsum(padded)
    pstart = pend - padded
    n_tiles = (t * TOP_K + N_EXPERTS * (tm - 1) + tm - 1) // tm
    is_expert = idx[:, :TOP_K, None] == jnp.arange(N_EXPERTS, dtype=I32)
    dest = pos[:, :TOP_K] + jnp.sum(jnp.where(is_expert, pstart, 0), axis=-1)
    tile_start = jnp.arange(n_tiles, dtype=I32) * tm
    tile_expert = jnp.minimum(jnp.sum((pend[None, :] <= tile_start[:, None]).astype(I32), axis=1), N_EXPERTS - 1)
    n_used = (pend[-1:] // tm).astype(I32)
    zero_plan = jnp.concatenate([jnp.where(padded > 0, pend - tm, -1), n_used]).astype(I32)
    xs = moe_dispatch(dest, zero_plan, h, n_tiles * tm)
    experts = jnp.arange(N_EXPERTS, dtype=I32)
    group_tiles = padded // tm
    nonempty = group_tiles > 0
    later = (experts[None, :] > experts[:, None]) & nonempty[None, :]
    earlier = (experts[None, :] < experts[:, None]) & nonempty[None, :]
    next_e = jnp.min(jnp.where(later, experts[None, :], N_EXPERTS), axis=1)
    prev_e = jnp.max(jnp.where(earlier, experts[None, :], -1), axis=1)
    next_e = jnp.where(next_e < N_EXPERTS, next_e, -1)
    prev_tiles = jnp.where(prev_e >= 0, group_tiles[jnp.maximum(prev_e, 0)], 0)
    ordinal = jnp.cumsum(nonempty.astype(I32)) - nonempty.astype(I32)
    tile_in_group = jnp.arange(n_tiles, dtype=I32) - (pstart // tm)[tile_expert]
    rows_in_tile = jnp.clip(cnt[tile_expert] - tile_in_group * tm, 1, tm)
    row_blocks = (rows_in_tile + ROW_BLOCK - 1) // ROW_BLOCK
    plan = tuple(a.astype(I32) for a in (tile_expert, tile_in_group, prev_tiles[tile_expert], next_e[tile_expert],
                                         ordinal[tile_expert], row_blocks, n_used))
    y = moe_experts(plan, xs, layer, w_gate_up, b_gate_up, w_down, b_down)
    return moe_combine(dest, x, gates, gate2, y, next_norm, next_dtype, emit_x)


def kernel(x, c, ada_w, ada_b, norm_w, dn_w_in, dn_conv_w, dn_a_log, dn_dt_bias, dn_o_norm_w, dn_w_out, sgu_w_in, sgu_b_in, sgu_ln_w, sgu_ln_b, sgu_w_sp, sgu_b_sp, sgu_w_out, sgu_b_out, moe_w_router, moe_b_router, moe_w_gate_up, moe_b_gate_up, moe_w_down, moe_b_down, final_norm_w):
    bsz, seq, d = x.shape
    assert bsz == 1 and d == D_MODEL
    xt = x.reshape(seq, d)
    mod = ada_mod(c, ada_w, ada_b)
    mods = [[mod[i:i + 1, s * d:(s + 1) * d] for s in range(6)] for i in range(DEPTH)]
    zero = jnp.zeros((1, d), F32)
    h = norm_mod(xt, norm_w[0, 0:1], mods[0][1], mods[0][0], BF16)
    for i in range(DEPTH):
        sh1, sc1, gt1, sh2, sc2, gt2 = mods[i]
        last = i == DEPTH - 1
        next_norm = (final_norm_w.reshape(1, d), zero, zero) if last else (norm_w[i + 1, 0:1], mods[i + 1][1], mods[i + 1][0])
        j = i // 2
        if i % 2 == 0:
            xt = gated_deltanet_block(xt, h, gt1, j, dn_w_in, dn_conv_w[j], dn_a_log[j], dn_dt_bias[j],
                                      dn_o_norm_w[j], dn_w_out)
        else:
            xt = chunked_gmlp_block(xt, h, gt1, j, sgu_w_in, sgu_b_in[j], sgu_ln_w[j], sgu_ln_b[j], sgu_w_sp[j],
                                    sgu_b_sp[j], sgu_w_out, sgu_b_out[j])
        res = moe_block(xt, norm_w[i, 1:2], sc2, sh2, gt2, i, moe_w_router[i], moe_b_router[i],
                        moe_w_gate_up, moe_b_gate_up[i], moe_w_down, moe_b_down[i],
                        next_norm, F32 if last else BF16, not last)
        if last:
            return res.reshape(bsz, seq, d)
        xt, h = res
```

```python
import functools

import jax
import jax.numpy as jnp
from jax import lax
from jax.experimental import pallas as pl
from jax.experimental.pallas import tpu as pltpu

F32 = jnp.float32
BF16 = jnp.bfloat16
I32 = jnp.int32
HIGHEST = lax.Precision.HIGHEST

D_MODEL = 2048
DEPTH = 2
NORM_EPS = 1e-6
DN_HEAD_DIM = 128
DN_QK_HEADS = 16
DN_V_HEADS = 32
DN_QK_DIM = 2048
DN_V_DIM = 4096
DN_CONV_CH = 8192
DN_CONV = 4
DN_CHUNK = 64
SGU_WIDTH = 4096
SGU_CHUNK = 128
SGU_GROUPS = 32
N_EXPERTS = 32
TOP_K = 4
EXPERT_DIM = 2048
SWIGLU_LIMIT = 7.0
SWIGLU_ALPHA = 1.702

LANES = 128
VMEM_LIMIT = 58 * 1024 * 1024
MOE_TILE = 512
ROW_BLOCK = 64
GDN_TILE = 2048


def _params(*sem):
    return pltpu.CompilerParams(dimension_semantics=sem, vmem_limit_bytes=VMEM_LIMIT)


def _iota(shape, dim):
    return lax.broadcasted_iota(I32, shape, dim)


def _sigmoid(x):
    return 1.0 / (1.0 + jnp.exp(-x))


def _silu(x):
    return x * _sigmoid(x)


def _ada_kernel(c_ref, w_ref, b_ref, o_ref):
    c = c_ref[...]
    o_ref[...] = jnp.dot(_silu(c), w_ref[...], precision=HIGHEST, preferred_element_type=F32) + b_ref[...]


def ada_mod(c, ada_w, ada_b):
    depth, d, n = ada_w.shape
    tn = 1024
    c8 = jnp.broadcast_to(c, (8, d))
    out = pl.pallas_call(
        _ada_kernel,
        grid=(depth, n // tn),
        in_specs=[
            pl.BlockSpec((8, d), lambda l, j: (0, 0)),
            pl.BlockSpec((None, d, tn), lambda l, j: (l, 0, j)),
            pl.BlockSpec((None, 1, tn), lambda l, j: (l, 0, j)),
        ],
        out_specs=pl.BlockSpec((None, 8, tn), lambda l, j: (l, 0, j)),
        out_shape=jax.ShapeDtypeStruct((depth, 8, n), F32),
        compiler_params=_params("parallel", "parallel"),
        name="ada_mod",
    )(c8, ada_w, ada_b.reshape(depth, 1, n))
    return out[:, 0, :]


def _norm_mod(x, w, sc, sh):
    y = x * lax.rsqrt(jnp.mean(x * x, axis=-1, keepdims=True) + NORM_EPS)
    return (y * w) * (1.0 + sc) + sh


def _norm_mod_kernel(x_ref, w_ref, sc_ref, sh_ref, o_ref):
    o_ref[...] = _norm_mod(x_ref[...], w_ref[...], sc_ref[...], sh_ref[...]).astype(o_ref.dtype)


def norm_mod(x, w, sc, sh, out_dtype):
    t, d = x.shape
    tm = 512
    vec = pl.BlockSpec((1, d), lambda i: (0, 0))
    return pl.pallas_call(
        _norm_mod_kernel,
        grid=(t // tm,),
        in_specs=[pl.BlockSpec((tm, d), lambda i: (i, 0)), vec, vec, vec],
        out_specs=pl.BlockSpec((tm, d), lambda i: (i, 0)),
        out_shape=jax.ShapeDtypeStruct((t, d), out_dtype),
        compiler_params=_params("parallel"),
        name="norm_mod",
    )(x, w, sc, sh)


def _gelu_exact(x):
    return 0.5 * x * (1.0 + lax.erf(x * (2.0 ** -0.5)))


def _mm_kernel(a_ref, w_hbm, *refs, mode, layer):
    *refs, wb, stage, sem = refs
    j = pl.program_id(0)
    tn = stage.shape[1]

    def w_copy(jj):
        cols = pl.ds(pl.multiple_of(jj * tn, tn), tn)
        src = w_hbm.at[:, cols] if layer is None else w_hbm.at[layer, :, cols]
        return pltpu.make_async_copy(src, stage, sem)

    @pl.when(pl.program_id(1) == 0)
    def _():
        @pl.when(j == 0)
        def _():
            w_copy(0).start()

        w_copy(j).wait()
        wb[...] = stage[...].astype(BF16)

        @pl.when(j + 1 < pl.num_programs(0))
        def _():
            w_copy(j + 1).start()

    acc = jnp.dot(a_ref[...], wb[...], preferred_element_type=F32)
    if mode == "plain":
        (o_ref,) = refs
        o_ref[...] = acc.astype(o_ref.dtype)
    elif mode == "bias_gelu":
        b_ref, o_ref = refs
        o_ref[...] = _gelu_exact(acc + b_ref[...]).astype(o_ref.dtype)
    else:
        b_ref, res_ref, gate_ref, o_ref = refs
        o_ref[...] = res_ref[...] + gate_ref[...] * (acc + b_ref[...])


def matmul(a, w, n, *, mode, out_dtype, tm, tn, layer=None, bias=None, res=None, gate=None):
    m, k = a.shape
    assert m % tm == 0 and n % tn == 0 and w.shape[-1] >= n
    row = pl.BlockSpec((1, tn), lambda j, i: (0, j))
    in_specs = [pl.BlockSpec((tm, k), lambda j, i: (i, 0)), pl.BlockSpec(memory_space=pl.ANY)]
    args = [a, w]
    if mode != "plain":
        in_specs.append(row)
        args.append(bias)
    if mode == "residual":
        in_specs += [pl.BlockSpec((tm, tn), lambda j, i: (i, j)), row]
        args += [res, gate]
    return pl.pallas_call(
        functools.partial(_mm_kernel, mode=mode, layer=layer),
        grid=(n // tn, m // tm),
        in_specs=in_specs,
        out_specs=pl.BlockSpec((tm, tn), lambda j, i: (i, j)),
        out_shape=jax.ShapeDtypeStruct((m, n), out_dtype),
        scratch_shapes=[pltpu.VMEM((k, tn), BF16), pltpu.VMEM((k, tn), F32), pltpu.SemaphoreType.DMA(())],
        compiler_params=_params("arbitrary", "arbitrary"),
        name="matmul_" + mode,
    )(*args)


def _softplus(x):
    return jnp.maximum(x, 0.0) + jnp.log1p(jnp.exp(-jnp.abs(x)))


def _unit_lower_inverses(lows, n_block):
    n = lows[0].shape[0]
    r = _iota((n, n), 0)
    c = _iota((n, n), 1)
    eye = jnp.where(r == c, 1.0, 0.0).astype(F32)

    def mm(a, b):
        return jnp.dot(a, b, preferred_element_type=F32)

    def bf(ms):
        return [m.astype(BF16) for m in ms]

    base = (r // 8) == (c // 8)
    lows_b = bf(lows)
    ds = [jnp.where(base, low, 0.0) for low in lows]
    ds_b = [jnp.where(base, low, jnp.zeros_like(low)) for low in lows_b]
    xs = [eye - d for d in ds]
    ps_b = bf([mm(d, d) for d in ds_b])
    xs = [x + mm(xb, p) for x, xb, p in zip(xs, bf(xs), ps_b)]
    ps_b = bf([mm(p, p) for p in ps_b])
    xs = [x + mm(xb, p) for x, xb, p in zip(xs, bf(xs), ps_b)]
    s = 8
    while s < n_block:
        pair = ((r // (2 * s)) == (c // (2 * s))) & ((r // s) != (c // s))
        xs_b = bf(xs)
        ts_b = bf([mm(jnp.where(pair, low, jnp.zeros_like(low)), xb) for low, xb in zip(lows_b, xs_b)])
        xs = [x - mm(xb, t) for x, xb, t in zip(xs, xs_b, ts_b)]
        s *= 2
    return xs


def _gdn_prep_kernel(q_ref, k_ref, v_ref, qh_ref, kh_ref, vh_ref, wq_ref, wk_ref, wv_ref, ab_ref, alog_ref, dtb_ref,
                     qg_ref, kd_ref, u_ref, w_ref, a_ref, gcum_ref,
                     win_q, win_k, win_v, gcum_s, gtot_s, beta_s):
    i = pl.program_id(0)
    h = pl.program_id(1)
    tm = q_ref.shape[0]
    halo = qh_ref.shape[0]
    cs = DN_CHUNK
    hd = DN_HEAD_DIM

    def conv_silu(x_ref, halo_ref, w_ref, win_ref):
        hal = halo_ref[...].astype(F32)
        win_ref[0:halo, :] = jnp.where(i == 0, jnp.zeros_like(hal), hal)
        win_ref[halo:halo + tm, :] = x_ref[...].astype(F32)
        w = w_ref[...]
        acc = None
        for j in range(DN_CONV):
            start = halo - (DN_CONV - 1) + j
            term = win_ref[start:start + tm, :] * w[j:j + 1, :]
            acc = term if acc is None else acc + term
        return _silu(acc)

    def l2norm(x):
        return x * lax.rsqrt(jnp.sum(x * x, axis=-1, keepdims=True) + NORM_EPS)

    @pl.when(h == 0)
    def _():
        ab = ab_ref[...]
        g = -jnp.exp(alog_ref[...]) * _softplus(ab + dtb_ref[...])
        sub = 4 * DN_CHUNK
        r = _iota((sub, sub), 0)
        c = _iota((sub, sub), 1)
        same = (r // DN_CHUNK) == (c // DN_CHUNK)
        tri = jnp.where(same & (c <= r), 1.0, 0.0).astype(F32)
        blk = jnp.where(same, 1.0, 0.0).astype(F32)
        for s0 in range(0, tm, sub):
            g_sub = g[s0:s0 + sub]
            gcum = jnp.dot(tri, g_sub, precision=HIGHEST, preferred_element_type=F32)
            gcum_s[s0:s0 + sub, :] = gcum
            gcum_ref[s0:s0 + sub, :] = gcum
            gtot_s[s0:s0 + sub, :] = jnp.dot(blk, g_sub, precision=HIGHEST, preferred_element_type=F32)
        beta_s[...] = _sigmoid(ab)

    q = l2norm(conv_silu(q_ref, qh_ref, wq_ref, win_q)) * (DN_HEAD_DIM ** -0.5)
    k = l2norm(conv_silu(k_ref, kh_ref, wk_ref, win_k))
    v = conv_silu(v_ref, vh_ref, wv_ref, win_v)

    def head_gates(ref, first_lane):
        rolled = pltpu.roll(ref[...], (LANES - first_lane) % LANES, 1)
        return [jnp.broadcast_to(rolled[:, s:s + 1], (tm, hd)) for s in range(2)]

    gc = head_gates(gcum_s, 2 * h)
    gt = head_gates(gtot_s, 2 * h)
    be = head_gates(beta_s, DN_V_HEADS + 2 * h)
    vb, kbg = [], []
    for s in range(2):
        cols = slice(s * hd, (s + 1) * hd)
        eg = jnp.exp(gc[s])
        qg_ref[:, cols] = (q * eg).astype(BF16)
        kd_ref[:, cols] = (k * jnp.exp(gt[s] - gc[s])).astype(BF16)
        vb.append((v[:, cols] * be[s]).astype(BF16))
        kbg.append((k * be[s] * eg).astype(BF16))
    qb = q.astype(BF16)
    kb = k.astype(BF16)

    n2 = 2 * cs
    r = _iota((n2, n2), 0)
    c = _iota((n2, n2), 1)
    same_head = (r // cs) == (c // cs)
    causal = same_head & (c <= r)
    strict = same_head & (c < r)
    top = _iota((cs, n2), 1) < cs
    nt = (((1,), (1,)), ((), ()))
    chunks = [slice(ci * cs, (ci + 1) * cs) for ci in range(tm // cs)]

    def stack_heads(x, rows):
        return jnp.concatenate([x[0][rows], x[1][rows]], axis=0)

    k_st = [jnp.concatenate([kb[rows], kb[rows]], axis=0) for rows in chunks]
    q_st = [jnp.concatenate([qb[rows], qb[rows]], axis=0) for rows in chunks]
    kk = [lax.dot_general(ks, ks, nt, preferred_element_type=F32) for ks in k_st]
    qk = [lax.dot_general(qs, ks, nt, preferred_element_type=F32) for qs, ks in zip(q_st, k_st)]
    g_col = [stack_heads(gc, rows) for rows in chunks]
    b_col = [stack_heads(be, rows) for rows in chunks]
    decay = [jnp.exp(jnp.where(causal, g - g.T, 0.0)) for g in g_col]
    lows = [jnp.where(strict, b * kk_i * d, 0.0) for b, kk_i, d in zip(b_col, kk, decay)]
    tinvs = [t.astype(BF16) for t in _unit_lower_inverses(lows, cs)]
    us = [jnp.dot(t, stack_heads(vb, rows), preferred_element_type=F32).astype(BF16) for t, rows in zip(tinvs, chunks)]
    ws = [jnp.dot(t, stack_heads(kbg, rows), preferred_element_type=F32).astype(BF16) for t, rows in zip(tinvs, chunks)]
    for rows, u2, w2, qk_i, d in zip(chunks, us, ws, qk, decay):
        intra = jnp.where(causal, qk_i * d, 0.0)
        a_ref[rows, :] = jnp.where(top, intra[:cs], intra[cs:]).astype(BF16)
        u_ref[rows, :hd] = u2[:cs]
        u_ref[rows, hd:] = u2[cs:]
        w_ref[rows, :hd] = w2[:cs]
        w_ref[rows, hd:] = w2[cs:]


def gdn_prep(proj, ab, conv_w, a_log, dt_bias):
    t = proj.shape[0]
    tm = GDN_TILE
    halo = 16
    hb = tm // halo
    hd = DN_HEAD_DIM
    nq = DN_QK_HEADS

    def halo_map(off):
        return lambda i, h: (jnp.maximum(i * hb - 1, 0), off + h)

    pad = jnp.zeros((1, LANES - DN_V_HEADS), F32)
    alog = jnp.concatenate([a_log.reshape(1, -1), pad], axis=1)
    dtb = jnp.concatenate([dt_bias.reshape(1, -1), pad], axis=1)
    big = jax.ShapeDtypeStruct((t, DN_V_DIM), BF16)
    small = jax.ShapeDtypeStruct((t, DN_QK_DIM), BF16)
    gshape = jax.ShapeDtypeStruct((t, LANES), F32)
    big_spec = pl.BlockSpec((tm, 2 * hd), lambda i, h: (i, h))
    small_spec = pl.BlockSpec((tm, hd), lambda i, h: (i, h))
    g_spec = pl.BlockSpec((tm, LANES), lambda i, h: (i, 0))
    row = pl.BlockSpec((1, LANES), lambda i, h: (0, 0))
    return pl.pallas_call(
        _gdn_prep_kernel,
        grid=(t // tm, nq),
        in_specs=[
            pl.BlockSpec((tm, hd), lambda i, h: (i, h)),
            pl.BlockSpec((tm, hd), lambda i, h: (i, nq + h)),
            pl.BlockSpec((tm, 2 * hd), lambda i, h: (i, nq + h)),
            pl.BlockSpec((halo, hd), halo_map(0)),
            pl.BlockSpec((halo, hd), halo_map(nq)),
            pl.BlockSpec((halo, 2 * hd), lambda i, h: (jnp.maximum(i * hb - 1, 0), nq + h)),
            pl.BlockSpec((DN_CONV, hd), lambda i, h: (0, h)),
            pl.BlockSpec((DN_CONV, hd), lambda i, h: (0, nq + h)),
            pl.BlockSpec((DN_CONV, 2 * hd), lambda i, h: (0, nq + h)),
            g_spec, row, row,
        ],
        out_specs=[big_spec, big_spec, big_spec, big_spec, small_spec, g_spec],
        out_shape=[big, big, big, big, small, gshape],
        scratch_shapes=[
            pltpu.VMEM((tm + halo, hd), F32), pltpu.VMEM((tm + halo, hd), F32), pltpu.VMEM((tm + halo, 2 * hd), F32),
            pltpu.VMEM((tm, LANES), F32), pltpu.VMEM((tm, LANES), F32), pltpu.VMEM((tm, LANES), F32),
        ],
        compiler_params=_params("parallel", "arbitrary"),
        name="gdn_prep",
    )(proj, proj, proj, proj, proj, proj, conv_w, conv_w, conv_w, ab, alog, dtb)


def _gdn_scan_kernel(glast_ref, qg_ref, kd_ref, u_ref, w_ref, a_ref, z_ref, onw_ref, og_ref, s_ref):
    ci = pl.program_id(0)
    cs = DN_CHUNK
    hd = DN_HEAD_DIM
    nh = DN_V_HEADS

    @pl.when(ci == 0)
    def _():
        s_ref[...] = jnp.zeros_like(s_ref)

    onw = onw_ref[...]
    tn = (((0,), (0,)), ((), ()))
    lane = _iota((cs, 2 * cs), 1)
    cols = [slice(hv * hd, (hv + 1) * hd) for hv in range(nh)]
    states = [s_ref[hv] for hv in range(nh)]
    ws = [jnp.dot(jnp.concatenate([w_ref[:, cols[hv]], qg_ref[:, cols[hv]]], axis=0), states[hv].astype(BF16),
                  preferred_element_type=F32) for hv in range(nh)]
    v_new = [(u_ref[:, cols[hv]].astype(F32) - ws[hv][:cs]).astype(BF16) for hv in range(nh)]
    for hv in range(nh):
        decay = jnp.exp(jnp.full((1, hd), glast_ref[ci, hv], F32))
        s_ref[hv] = states[hv] * decay + lax.dot_general(kd_ref[:, cols[hv]], v_new[hv], tn, preferred_element_type=F32)
    for pair in range(nh // 2):
        a2 = a_ref[:, pair * 2 * cs:(pair + 1) * 2 * cs]
        v2 = jnp.concatenate([v_new[2 * pair], v_new[2 * pair + 1]], axis=0)
        for s in range(2):
            hv = 2 * pair + s
            a_s = jnp.where((lane // cs) == s, a2, jnp.zeros_like(a2))
            o = ws[hv][cs:] + jnp.dot(a_s, v2, preferred_element_type=F32)
            on = o * lax.rsqrt(jnp.mean(o * o, axis=-1, keepdims=True) + NORM_EPS) * onw
            og_ref[:, cols[hv]] = (on * _silu(z_ref[:, cols[hv]].astype(F32))).astype(BF16)


def gdn_scan(glast, qg, kd, u, w, intra, proj, o_norm_w):
    t = qg.shape[0]
    cs = DN_CHUNK
    big_spec = pl.BlockSpec((cs, DN_V_DIM), lambda c, g: (c, 0))
    return pl.pallas_call(
        _gdn_scan_kernel,
        grid_spec=pltpu.PrefetchScalarGridSpec(
            num_scalar_prefetch=1,
            grid=(t // cs,),
            in_specs=[big_spec, big_spec, big_spec, big_spec,
                      pl.BlockSpec((cs, DN_V_HEADS * cs), lambda c, g: (c, 0)),
                      pl.BlockSpec((cs, DN_V_DIM), lambda c, g: (c, DN_CONV_CH // DN_V_DIM)),
                      pl.BlockSpec((1, DN_HEAD_DIM), lambda c, g: (0, 0))],
            out_specs=big_spec,
            scratch_shapes=[pltpu.VMEM((DN_V_HEADS, DN_HEAD_DIM, DN_HEAD_DIM), F32)],
        ),
        out_shape=jax.ShapeDtypeStruct((t, DN_V_DIM), BF16),
        compiler_params=_params("arbitrary"),
        name="gdn_scan",
    )(glast, qg, kd, u, w, intra, proj, o_norm_w.reshape(1, -1))


def gated_deltanet_block(x, h, gate, layer, w_in, conv_w, a_log, dt_bias, o_norm_w, w_out):
    t = x.shape[0]
    n_main = DN_CONV_CH + DN_V_DIM
    w_ab = jnp.pad(w_in[layer, :, n_main:], ((0, 0), (0, LANES - 2 * DN_V_HEADS)))
    proj = matmul(h, w_in, n_main, layer=layer, mode="plain", out_dtype=BF16, tm=2048, tn=1024)
    ab = matmul(h, w_ab, LANES, mode="plain", out_dtype=F32, tm=1024, tn=LANES)
    qg, kd, u, w, intra, gcum = gdn_prep(proj, ab, conv_w, a_log, dt_bias)
    glast = gcum.reshape(t // DN_CHUNK, DN_CHUNK, LANES)[:, DN_CHUNK - 1, :DN_V_HEADS]
    og = gdn_scan(glast, qg, kd, u, w, intra, proj, o_norm_w)
    zero_bias = jnp.zeros((1, D_MODEL), F32)
    return matmul(og, w_out, D_MODEL, layer=layer, mode="residual", out_dtype=F32, tm=1024, tn=512,
                  bias=zero_bias, res=x, gate=gate)


def _sgu_spatial_kernel(u_ref, v_ref, lnw_ref, lnb_ref, wsp_ref, bsp_ref, o_ref, wm_ref):
    cs = SGU_CHUNK
    gd = SGU_WIDTH // SGU_GROUPS

    @pl.when(pl.program_id(0) == 0)
    def _():
        r = _iota((cs, cs), 0)
        c = _iota((cs, cs), 1)
        for g in range(SGU_GROUPS):
            wm_ref[g] = jnp.where(c <= r, wsp_ref[g], 0.0).astype(BF16)

    v = v_ref[...].astype(F32)
    mu = jnp.mean(v, axis=-1, keepdims=True)
    var = jnp.mean(jnp.square(v - mu), axis=-1, keepdims=True)
    vn = ((v - mu) * lax.rsqrt(var + NORM_EPS) * lnw_ref[...] + lnb_ref[...]).astype(BF16)
    bsp = bsp_ref[...]
    for g in range(SGU_GROUPS):
        cols = slice(g * gd, (g + 1) * gd)
        wg = wm_ref[g]
        for ci in range(v_ref.shape[0] // cs):
            rows = slice(ci * cs, (ci + 1) * cs)
            sp = jnp.dot(wg, vn[rows, cols], preferred_element_type=F32) + bsp[:, g:g + 1]
            o_ref[rows, cols] = (u_ref[rows, cols].astype(F32) * sp).astype(BF16)


def sgu_spatial(zz, ln_w, ln_b, w_sp, b_sp):
    t = zz.shape[0]
    cs = SGU_CHUNK
    wd = SGU_WIDTH
    bsp_t = jnp.pad(b_sp.T, ((0, 0), (0, LANES - SGU_GROUPS)))
    row = pl.BlockSpec((1, wd), lambda i: (0, 0))
    tm = 4 * cs
    return pl.pallas_call(
        _sgu_spatial_kernel,
        grid=(t // tm,),
        in_specs=[
            pl.BlockSpec((tm, wd), lambda i: (i, 0)),
            pl.BlockSpec((tm, wd), lambda i: (i, 1)),
            row, row,
            pl.BlockSpec((SGU_GROUPS, cs, cs), lambda i: (0, 0, 0)),
            pl.BlockSpec((cs, LANES), lambda i: (0, 0)),
        ],
        out_specs=pl.BlockSpec((tm, wd), lambda i: (i, 0)),
        out_shape=jax.ShapeDtypeStruct((t, wd), BF16),
        scratch_shapes=[pltpu.VMEM((SGU_GROUPS, cs, cs), BF16)],
        compiler_params=_params("arbitrary"),
        name="sgu_spatial",
    )(zz, zz, ln_w.reshape(1, -1), ln_b.reshape(1, -1), w_sp, bsp_t)


def chunked_gmlp_block(x, h, gate, layer, w_in, b_in, ln_w, ln_b, w_sp, b_sp, w_out, b_out):
    zz = matmul(h, w_in, 2 * SGU_WIDTH, layer=layer, mode="bias_gelu", out_dtype=BF16, tm=2048, tn=1024,
                bias=b_in.reshape(1, -1))
    su = sgu_spatial(zz, ln_w, ln_b, w_sp, b_sp)
    return matmul(su, w_out, D_MODEL, layer=layer, mode="residual", out_dtype=F32, tm=1024, tn=512,
                  bias=b_out.reshape(1, -1), res=x, gate=gate)


def _dot_split3(a, b):
    a_hi = a.astype(BF16)
    b_hi = b.astype(BF16)
    a_lo = (a - a_hi.astype(F32)).astype(BF16)
    b_lo = (b - b_hi.astype(F32)).astype(BF16)
    return (jnp.dot(a_hi, b_hi, preferred_element_type=F32)
            + (jnp.dot(a_lo, b_hi, preferred_element_type=F32) + jnp.dot(a_hi, b_lo, preferred_element_type=F32)))


def _router_kernel(x_ref, w_ref, sc_ref, sh_ref, wr_ref, br_ref, h_ref, idx_ref, gate_ref, pos_ref, cnt_ref, carry):
    i = pl.program_id(0)
    tm = x_ref.shape[0]

    @pl.when(i == 0)
    def _():
        carry[...] = jnp.zeros_like(carry)

    h = _norm_mod(x_ref[...], w_ref[...], sc_ref[...], sh_ref[...])
    h_ref[...] = h
    lane = _iota((tm, LANES), 1).astype(F32)
    neg = jnp.float32(-jnp.inf)
    logits = _dot_split3(h, wr_ref[...]) + br_ref[...]
    logits = jnp.where(lane < N_EXPERTS, logits, neg)
    vals, idxs = [], []
    for _ in range(TOP_K):
        m = jnp.max(logits, axis=-1, keepdims=True)
        ix = jnp.min(jnp.where(logits == m, lane, float(LANES)), axis=-1, keepdims=True)
        vals.append(m)
        idxs.append(ix)
        logits = jnp.where(lane == ix, neg, logits)
    es = [jnp.exp(v - vals[0]) for v in vals]
    denom = es[0] + es[1] + es[2] + es[3]
    multi = jnp.zeros((tm, LANES), F32)
    for ix in idxs:
        multi = jnp.where(lane == ix, 1.0, multi)
    r = _iota((tm, tm), 0)
    c = _iota((tm, tm), 1)
    before = jnp.where(c < r, 1.0, 0.0).astype(BF16)
    rank = jnp.dot(before, multi.astype(BF16), preferred_element_type=F32) + carry[0:1, :]
    idx_t = jnp.zeros((tm, LANES), F32)
    gate_t = jnp.zeros((tm, LANES), F32)
    pos_t = jnp.zeros((tm, LANES), F32)
    for kk in range(TOP_K):
        pk = jnp.sum(jnp.where(lane == idxs[kk], rank, 0.0), axis=-1, keepdims=True)
        idx_t = jnp.where(lane == kk, idxs[kk], idx_t)
        gate_t = jnp.where(lane == kk, es[kk] / denom, gate_t)
        pos_t = jnp.where(lane == kk, pk, pos_t)
    idx_ref[...] = idx_t.astype(I32)
    gate_ref[...] = gate_t
    pos_ref[...] = pos_t.astype(I32)
    carry[...] = carry[...] + jnp.sum(multi, axis=0, keepdims=True)
    cnt_ref[...] = carry[...]


def moe_router(x, w, sc, sh, w_router, b_router):
    t, d = x.shape
    tm = 1024
    wr = jnp.pad(w_router, ((0, 0), (0, LANES - N_EXPERTS)))
    br = jnp.pad(b_router.reshape(1, -1), ((0, 0), (0, LANES - N_EXPERTS)))
    vec = pl.BlockSpec((1, d), lambda i: (0, 0))
    tile = pl.BlockSpec((tm, LANES), lambda i: (i, 0))
    return pl.pallas_call(
        _router_kernel,
        grid=(t // tm,),
        in_specs=[pl.BlockSpec((tm, d), lambda i: (i, 0)), vec, vec, vec,
                  pl.BlockSpec((d, LANES), lambda i: (0, 0)), pl.BlockSpec((1, LANES), lambda i: (0, 0))],
        out_specs=[pl.BlockSpec((tm, d), lambda i: (i, 0)), tile, tile, tile, pl.BlockSpec((8, LANES), lambda i: (0, 0))],
        out_shape=[jax.ShapeDtypeStruct((t, d), F32), jax.ShapeDtypeStruct((t, LANES), I32),
                   jax.ShapeDtypeStruct((t, LANES), F32), jax.ShapeDtypeStruct((t, LANES), I32),
                   jax.ShapeDtypeStruct((8, LANES), F32)],
        scratch_shapes=[pltpu.VMEM((8, LANES), F32)],
        compiler_params=_params("arbitrary"),
        name="moe_router",
    )(x, w, sc, sh, wr, br)


def _dispatch_kernel(dest_ref, ztile_ref, h_ref, xs_ref, zbuf, sem, zsem):
    i = pl.program_id(0)
    tm = h_ref.shape[0]
    zt = zbuf.shape[0]

    @pl.when(i == 0)
    def _():
        zbuf[...] = jnp.zeros_like(zbuf)

        def zero_copy(e):
            row = pl.multiple_of(ztile_ref[e], zt)
            return pltpu.make_async_copy(zbuf, xs_ref.at[pl.ds(row, zt), :], zsem)

        for e in range(N_EXPERTS):
            @pl.when(ztile_ref[e] >= 0)
            def _():
                zero_copy(e).start()
        for e in range(N_EXPERTS):
            @pl.when(ztile_ref[e] >= 0)
            def _():
                zero_copy(e).wait()

        def zero_tail(j, carry):
            row = pl.multiple_of(j * zt, zt)
            cp = pltpu.make_async_copy(zbuf, xs_ref.at[pl.ds(row, zt), :], zsem)
            cp.start()
            cp.wait()
            return carry

        lax.fori_loop(ztile_ref[N_EXPERTS], xs_ref.shape[0] // zt, zero_tail, 0)

    def row_copy(r, d):
        return pltpu.make_async_copy(h_ref.at[pl.ds(r, 1), :], xs_ref.at[pl.ds(d, 1), :], sem)

    def issue(r, carry):
        base = (i * tm + r) * TOP_K
        for kk in range(TOP_K):
            row_copy(r, dest_ref[base + kk]).start(priority=kk % 2)
        return carry

    lax.fori_loop(0, tm, issue, 0, unroll=8)

    for kk in range(TOP_K):
        pltpu.make_async_copy(h_ref, xs_ref.at[pl.ds(0, tm), :], sem).wait()


def moe_dispatch(dest, zero_plan, h, n_rows):
    t, d = h.shape
    tm = 512
    return pl.pallas_call(
        _dispatch_kernel,
        grid_spec=pltpu.PrefetchScalarGridSpec(
            num_scalar_prefetch=2,
            grid=(t // tm,),
            in_specs=[pl.BlockSpec((tm, d), lambda i, dst, zt: (i, 0))],
            out_specs=pl.BlockSpec(memory_space=pl.ANY),
            scratch_shapes=[pltpu.VMEM((MOE_TILE, d), F32), pltpu.SemaphoreType.DMA(()), pltpu.SemaphoreType.DMA(())],
        ),
        out_shape=jax.ShapeDtypeStruct((n_rows, d), F32),
        compiler_params=_params("arbitrary"),
        name="moe_dispatch",
    )(dest.reshape(-1), zero_plan, h)


def _stream_expert_weights(w_hbm, wbuf, stage, sem, layer, e_cur, e_next, tig, gprev, cur, phase=0, phases=1):
    n_pairs = wbuf.shape[1] // 2
    cw = stage.shape[-1]
    prev_calls = gprev * phases
    q = tig * phases + phase

    def chunk_copy(e, c):
        return pltpu.make_async_copy(w_hbm.at[layer, e, :, pl.ds(c * cw, cw)], stage.at[c % 2], sem.at[c % 2])

    def start_pair(e, p):
        chunk_copy(e, 2 * p).start()
        chunk_copy(e, 2 * p + 1).start()

    def retire_pair(e, p, slot):
        for c in (2 * p, 2 * p + 1):
            chunk_copy(e, c).wait()
            wbuf[slot, c] = stage[c % 2].astype(BF16)

    if phase == 0:
        @pl.when(tig == 0)
        def _():
            for p in range(n_pairs):
                @pl.when(prev_calls <= p)
                def _():
                    start_pair(e_cur, p)

                @pl.when(prev_calls <= p + 1)
                def _():
                    retire_pair(e_cur, p, cur)

    @pl.when(e_next >= 0)
    def _():
        for j in range(phase, n_pairs + 1, phases):
            @pl.when(q == j)
            def _():
                if j >= 1:
                    retire_pair(e_next, j - 1, 1 - cur)
                if j < n_pairs:
                    start_pair(e_next, j)


def _for_row_bucket(n_blocks, tm, body):
    for k in range(1, tm // ROW_BLOCK + 1):
        @pl.when(n_blocks == k)
        def _():
            body(k * ROW_BLOCK)


def _ffn_up_kernel(te_ref, tig_ref, gprev_ref, nxt_ref, gidx_ref, nb_ref, nu_ref, x_ref, w_hbm, b_ref, o_ref,
                   wbuf, stage, sem, *, layer):
    i = pl.program_id(0)
    tm, f = o_ref.shape
    nc = wbuf.shape[1] // 2
    fc = wbuf.shape[-1]

    @pl.when(i < nu_ref[0])
    def _():
        cur = gidx_ref[i] % 2
        _stream_expert_weights(w_hbm, wbuf, stage, sem, layer, te_ref[i], nxt_ref[i], tig_ref[i], gprev_ref[i], cur)

        def compute(m):
            xb = x_ref[:m, :].astype(BF16)
            for c in range(nc):
                g = jnp.dot(xb, wbuf[cur, c], preferred_element_type=F32) + b_ref[:, c * fc:(c + 1) * fc]
                u = jnp.dot(xb, wbuf[cur, nc + c], preferred_element_type=F32) + b_ref[:, f + c * fc:f + (c + 1) * fc]
                x_glu = jnp.minimum(g, SWIGLU_LIMIT)
                x_lin = jnp.clip(u, -SWIGLU_LIMIT, SWIGLU_LIMIT)
                act = x_glu * _sigmoid(SWIGLU_ALPHA * x_glu) * (x_lin + 1.0)
                o_ref[:m, c * fc:(c + 1) * fc] = act.astype(BF16)
            if m < tm:
                o_ref[m:, :] = jnp.zeros((tm - m, f), BF16)

        _for_row_bucket(nb_ref[i], tm, compute)

    @pl.when(i >= nu_ref[0])
    def _():
        o_ref[...] = jnp.zeros_like(o_ref)


def _ffn_down_kernel(te_ref, tig_ref, gprev_ref, nxt_ref, gidx_ref, nb_ref, nu_ref, a_ref, w_hbm, b_ref, o_ref,
                     wbuf, stage, sem, *, layer):
    i = pl.program_id(0)
    tm, d = o_ref.shape
    fc = wbuf.shape[-1]

    @pl.when(i < nu_ref[0])
    def _():
        cur = gidx_ref[i] % 2
        _stream_expert_weights(w_hbm, wbuf, stage, sem, layer, te_ref[i], nxt_ref[i], tig_ref[i], gprev_ref[i], cur)

        def compute(m):
            a = a_ref[:m, :]
            for c in range(wbuf.shape[1]):
                cols = slice(c * fc, (c + 1) * fc)
                o_ref[:m, cols] = jnp.dot(a, wbuf[cur, c], preferred_element_type=F32) + b_ref[:, cols]
            if m < tm:
                o_ref[m:, :] = jnp.zeros((tm - m, d), F32)

        _for_row_bucket(nb_ref[i], tm, compute)

    @pl.when(i >= nu_ref[0])
    def _():
        o_ref[...] = jnp.zeros_like(o_ref)


def moe_experts(plan, xs, layer, w_gate_up, b_gate_up, w_down, b_down):
    n_rows, d = xs.shape
    tm = MOE_TILE
    n_tiles = n_rows // tm
    f = w_down.shape[2]
    cw = f // 4
    n_plan = len(plan)
    expert_row = lambda i, te, *_: (te[i], 0, 0)
    any_spec = pl.BlockSpec(memory_space=pl.ANY)
    act = pl.pallas_call(
        functools.partial(_ffn_up_kernel, layer=layer),
        grid_spec=pltpu.PrefetchScalarGridSpec(
            num_scalar_prefetch=n_plan,
            grid=(n_tiles,),
            in_specs=[pl.BlockSpec((tm, d), lambda i, *p: (jnp.minimum(i, p[-1][0] - 1), 0)),
                      any_spec,
                      pl.BlockSpec((None, 1, 2 * f), expert_row)],
            out_specs=pl.BlockSpec((tm, f), lambda i, *p: (i, 0)),
            scratch_shapes=[pltpu.VMEM((2, 2 * f // cw, d, cw), BF16), pltpu.VMEM((2, d, cw), F32),
                            pltpu.SemaphoreType.DMA((2,))],
        ),
        out_shape=jax.ShapeDtypeStruct((n_rows, f), BF16),
        compiler_params=_params("arbitrary"),
        name="moe_ffn_up",
    )(*plan, xs, w_gate_up, b_gate_up.reshape(N_EXPERTS, 1, 2 * f))
    return pl.pallas_call(
        functools.partial(_ffn_down_kernel, layer=layer),
        grid_spec=pltpu.PrefetchScalarGridSpec(
            num_scalar_prefetch=n_plan,
            grid=(n_tiles,),
            in_specs=[pl.BlockSpec((tm, f), lambda i, *p: (i, 0)),
                      any_spec,
                      pl.BlockSpec((None, 1, d), expert_row)],
            out_specs=pl.BlockSpec((tm, d), lambda i, *p: (i, 0)),
            scratch_shapes=[pltpu.VMEM((2, d // cw, f, cw), BF16), pltpu.VMEM((2, f, cw), F32),
                            pltpu.SemaphoreType.DMA((2,))],
        ),
        out_shape=jax.ShapeDtypeStruct((n_rows, d), F32),
        compiler_params=_params("arbitrary"),
        name="moe_ffn_down",
    )(*plan, act, w_down, b_down.reshape(N_EXPERTS, 1, d))


def _combine_kernel(dest_ref, x_ref, gate_ref, g2_ref, nw_ref, nsc_ref, nsh_ref, y_ref, *refs, emit_x):
    if emit_x:
        o_ref, hn_ref, ybuf, sem = refs
    else:
        hn_ref, ybuf, sem = refs
    i = pl.program_id(0)
    n = pl.num_programs(0)
    tm = x_ref.shape[0]

    def row_copy(tile, slot, r, kk):
        d = dest_ref[(tile * tm + r) * TOP_K + kk]
        return pltpu.make_async_copy(y_ref.at[pl.ds(d, 1), :], ybuf.at[slot, kk, pl.ds(r, 1), :], sem.at[slot])

    def issue_tile(tile, slot):
        def body(r, carry):
            for kk in range(TOP_K):
                row_copy(tile, slot, r, kk).start(priority=kk % 2)
            return carry
        lax.fori_loop(0, tm, body, 0, unroll=8)

    def drain_tile(slot):
        for kk in range(TOP_K):
            pltpu.make_async_copy(y_ref.at[pl.ds(0, tm), :], ybuf.at[slot, kk], sem.at[slot]).wait()

    slot = i % 2

    @pl.when(i == 0)
    def _():
        issue_tile(0, 0)

    @pl.when(i + 1 < n)
    def _():
        issue_tile(i + 1, 1 - slot)

    drain_tile(slot)
    gates = gate_ref[...]
    acc = ybuf[slot, 0] * gates[:, 0:1]
    for kk in range(1, TOP_K):
        acc = acc + ybuf[slot, kk] * gates[:, kk:kk + 1]
    x_new = x_ref[...] + g2_ref[...] * acc
    if emit_x:
        o_ref[...] = x_new
    hn_ref[...] = _norm_mod(x_new, nw_ref[...], nsc_ref[...], nsh_ref[...]).astype(hn_ref.dtype)


def moe_combine(dest, x, gates, gate2, y, next_norm, next_dtype, emit_x):
    t, d = x.shape
    tm = 256
    tile = pl.BlockSpec((tm, d), lambda i, dst: (i, 0))
    vec = pl.BlockSpec((1, d), lambda i, dst: (0, 0))
    normed = jax.ShapeDtypeStruct((t, d), next_dtype)
    return pl.pallas_call(
        functools.partial(_combine_kernel, emit_x=emit_x),
        grid_spec=pltpu.PrefetchScalarGridSpec(
            num_scalar_prefetch=1,
            grid=(t // tm,),
            in_specs=[tile, pl.BlockSpec((tm, LANES), lambda i, dst: (i, 0)), vec, vec, vec, vec,
                      pl.BlockSpec(memory_space=pl.ANY)],
            out_specs=[tile, tile] if emit_x else tile,
            scratch_shapes=[pltpu.VMEM((2, TOP_K, tm, d), F32), pltpu.SemaphoreType.DMA((2,))],
        ),
        out_shape=[jax.ShapeDtypeStruct((t, d), F32), normed] if emit_x else normed,
        compiler_params=_params("arbitrary"),
        name="moe_combine",
    )(dest.reshape(-1), x, gates, gate2, *next_norm, y)


def moe_block(x, norm_w, sc, sh, gate2, layer, w_router, b_router, w_gate_up, b_gate_up, w_down, b_down,
              next_norm, next_dtype, emit_x):
    t, d = x.shape
    tm = MOE_TILE
    h, idx, gates, pos, counts = moe_router(x, norm_w, sc, sh, w_router, b_router)
    cnt = counts[0, :N_EXPERTS].astype(I32)
    padded = (cnt + tm - 1) // tm * tm
    pend = jnp.cumsum(padded)
    pstart = pend - padded
    n_tiles = (t * TOP_K + N_EXPERTS * (tm - 1) + tm - 1) // tm
    is_expert = idx[:, :TOP_K, None] == jnp.arange(N_EXPERTS, dtype=I32)
    dest = pos[:, :TOP_K] + jnp.sum(jnp.where(is_expert, pstart, 0), axis=-1)
    tile_start = jnp.arange(n_tiles, dtype=I32) * tm
    tile_expert = jnp.minimum(jnp.sum((pend[None, :] <= tile_start[:, None]).astype(I32), axis=1), N_EXPERTS - 1)
    n_used = (pend[-1:] // tm).astype(I32)
    zero_plan = jnp.concatenate([jnp.where(padded > 0, pend - tm, -1), n_used]).astype(I32)
    xs = moe_dispatch(dest, zero_plan, h, n_tiles * tm)
    experts = jnp.arange(N_EXPERTS, dtype=I32)
    group_tiles = padded // tm
    nonempty = group_tiles > 0
    later = (experts[None, :] > experts[:, None]) & nonempty[None, :]
    earlier = (experts[None, :] < experts[:, None]) & nonempty[None, :]
    next_e = jnp.min(jnp.where(later, experts[None, :], N_EXPERTS), axis=1)
    prev_e = jnp.max(jnp.where(earlier, experts[None, :], -1), axis=1)
    next_e = jnp.where(next_e < N_EXPERTS, next_e, -1)
    prev_tiles = jnp.where(prev_e >= 0, group_tiles[jnp.maximum(prev_e, 0)], 0)
    ordinal = jnp.cumsum(nonempty.astype(I32)) - nonempty.astype(I32)
    tile_in_group = jnp.arange(n_tiles, dtype=I32) - (pstart // tm)[tile_expert]
    rows_in_tile = jnp.clip(cnt[tile_expert] - tile_in_group * tm, 1, tm)
    row_blocks = (rows_in_tile + ROW_BLOCK - 1) // ROW_BLOCK
    plan = tuple(a.astype(I32) for a in (tile_expert, tile_in_group, prev_tiles[tile_expert], next_e[tile_expert],
                                         ordinal[tile_expert], row_blocks, n_used))
    y = moe_experts(plan, xs, layer, w_gate_up, b_gate_up, w_down, b_down)
    return moe_combine(dest, x, gates, gate2, y, next_norm, next_dtype, emit_x)


def kernel(x, c, ada_w, ada_b, norm_w, dn_w_in, dn_conv_w, dn_a_log, dn_dt_bias, dn_o_norm_w, dn_w_out, sgu_w_in, sgu_b_in, sgu_ln_w, sgu_ln_b, sgu_w_sp, sgu_b_sp, sgu_w_out, sgu_b_out, moe_w_router, moe_b_router, moe_w_gate_up, moe_b_gate_up, moe_w_down, moe_b_down, final_norm_w):
    bsz, seq, d = x.shape
    assert bsz == 1 and d == D_MODEL
    xt = x.reshape(seq, d)
    mod = ada_mod(c, ada_w, ada_b)
    mods = [[mod[i:i + 1, s * d:(s + 1) * d] for s in range(6)] for i in range(DEPTH)]
    zero = jnp.zeros((1, d), F32)
    h = norm_mod(xt, norm_w[0, 0:1], mods[0][1], mods[0][0], BF16)
    for i in range(DEPTH):
        sh1, sc1, gt1, sh2, sc2, gt2 = mods[i]
        last = i == DEPTH - 1
        next_norm = (final_norm_w.reshape(1, d), zero, zero) if last else (norm_w[i + 1, 0:1], mods[i + 1][1], mods[i + 1][0])
        j = i // 2
        if i % 2 == 0:
            xt = gated_deltanet_block(xt, h, gt1, j, dn_w_in, dn_conv_w[j], dn_a_log[j], dn_dt_bias[j],
                                      dn_o_norm_w[j], dn_w_out)
        else:
            xt = chunked_gmlp_block(xt, h, gt1, j, sgu_w_in, sgu_b_in[j], sgu_ln_w[j], sgu_ln_b[j], sgu_w_sp[j],
                                    sgu_b_sp[j], sgu_w_out, sgu_b_out[j])
        res = moe_block(xt, norm_w[i, 1:2], sc2, sh2, gt2, i, moe_w_router[i], moe_b_router[i],
                        moe_w_gate_up, moe_b_gate_up[i], moe_w_down, moe_b_down[i],
                        next_norm, F32 if last else BF16, not last)
        if last:
            return res.reshape(bsz, seq, d)
        xt, h = res
```
